```python
import jax
import jax.numpy as jnp
from jax import lax
import numpy as np

D_MODEL = 2048
BATCH = 8
SEQ = 2048
DEPTH = 1

POOL_WINDOWS = (2, 4, 8, 16)
POOL_GROUPS = 4
POOL_GROUP_DIM = D_MODEL // 8
POOL_WIDTH = POOL_GROUPS * POOL_GROUP_DIM
N_HEADS = 16
N_KV_HEADS = 4
HEAD_DIM = 128
ATTN_WIDTH = N_HEADS * HEAD_DIM
KV_WIDTH = N_KV_HEADS * HEAD_DIM
ROT_DIM = HEAD_DIM // 4
IDX_HEADS = 16
IDX_DIM = 64
IDX_ROT_DIM = IDX_DIM // 4
TOPK_MAX = 256
Q_BLOCK = 64
ROPE_THETA = 500000.0
N_BRANCHES = 2
IN_SPLITS = (POOL_WIDTH, ATTN_WIDTH, KV_WIDTH, KV_WIDTH, IDX_HEADS * IDX_DIM, IDX_DIM, IDX_HEADS, N_BRANCHES * D_MODEL)
IN_WIDTH = POOL_WIDTH + ATTN_WIDTH + 2 * KV_WIDTH + IDX_HEADS * IDX_DIM + IDX_DIM + IDX_HEADS + N_BRANCHES * D_MODEL
N_EXPERTS = 32
TOP_K = 4
D_FF = D_MODEL
SWIGLU_ALPHA = 1.702
SWIGLU_LIMIT = 7.0
MOE_BLOCK = 256
N_MOD = 6
EPS = 1e-6

kernel_name = 'hybrid_pool_dsa_moe_block'


def rms_norm(x, g):
    xf = x.astype(jnp.float32)
    y = xf * lax.rsqrt(jnp.mean(xf * xf, axis=-1, keepdims=True) + EPS)
    return (y * g.astype(jnp.float32)).astype(x.dtype)


def rope_tables(seq, rot_dim):
    inv = ROPE_THETA ** (-jnp.arange(0, rot_dim, 2, dtype=jnp.float32) / rot_dim)
    ang = jnp.arange(seq, dtype=jnp.float32)[:, None] * inv[None, :]
    return jnp.cos(ang), jnp.sin(ang)


def partial_rope(x, cos, sin):
    half = cos.shape[-1]
    rot = 2 * half
    shape = (1, x.shape[1]) + (1,) * (x.ndim - 3) + (half,)
    c = cos.reshape(shape).astype(x.dtype)
    s = sin.reshape(shape).astype(x.dtype)
    x1 = x[..., :half]
    x2 = x[..., half:rot]
    return jnp.concatenate([x1 * c - x2 * s, x2 * c + x1 * s, x[..., rot:]], axis=-1)


def pool_mixer(u, w_grp, scale):
    b_, s_, _ = u.shape
    uf = u.astype(jnp.float32).reshape(b_, s_, POOL_GROUPS, POOL_GROUP_DIM)
    cs = jnp.concatenate([jnp.zeros_like(uf[:, :1]), jnp.cumsum(uf, axis=1)], axis=1)
    t = jnp.arange(s_)
    outs = []
    for g, w in enumerate(POOL_WINDOWS):
        csg = cs[:, :, g]
        lo = jnp.maximum(t + 1 - w, 0)
        cnt = jnp.minimum(t + 1, w).astype(jnp.float32)[None, :, None]
        outs.append((csg[:, t + 1] - csg[:, lo]) / cnt)
    pooled = jnp.stack(outs, axis=2)
    mixed = (pooled - uf).astype(u.dtype)
    y = jnp.einsum('bsgc,gcd->bsgd', mixed, w_grp).reshape(b_, s_, POOL_WIDTH)
    return y * scale


def dsa_attention(q, k, v, qi, ki, wi):
    b_, s_ = q.shape[0], q.shape[1]
    n_keep = min(TOPK_MAX, s_ // 4)
    n_blk = s_ // Q_BLOCK
    grp = N_HEADS // N_KV_HEADS
    gather = jax.vmap(lambda a, i: a[i])
    key_pos = jnp.arange(s_)

    def block(bi):
        start = bi * Q_BLOCK
        qb = lax.dynamic_slice_in_dim(q, start, Q_BLOCK, axis=1)
        qib = lax.dynamic_slice_in_dim(qi, start, Q_BLOCK, axis=1)
        wib = lax.dynamic_slice_in_dim(wi, start, Q_BLOCK, axis=1)
        t = start + jnp.arange(Q_BLOCK)
        causal = key_pos[None, :] <= t[:, None]
        dots = jnp.einsum('bqhd,bsd->bqhs', qib, ki, preferred_element_type=jnp.float32)
        score = jnp.einsum('bqhs,bqh->bqs', jax.nn.relu(dots), wib.astype(jnp.float32))
        score = jnp.where(causal[None], score, -jnp.inf)
        _, idx = lax.top_k(score, n_keep)
        valid = idx <= t[None, :, None]
        ks = gather(k, idx)
        vs = gather(v, idx)
        qg = qb.reshape(b_, Q_BLOCK, N_KV_HEADS, grp, HEAD_DIM)
        logits = jnp.einsum('bqgrd,bqkgd->bqgrk', qg, ks, preferred_element_type=jnp.float32) * (HEAD_DIM ** -0.5)
        logits = jnp.where(valid[:, :, None, None, :], logits, -jnp.inf)
        p = jax.nn.softmax(logits, axis=-1)
        o = jnp.einsum('bqgrk,bqkgd->bqgrd', p.astype(vs.dtype), vs)
        return o.reshape(b_, Q_BLOCK, ATTN_WIDTH)

    out = lax.map(block, jnp.arange(n_blk))
    return jnp.transpose(out, (1, 0, 2, 3)).reshape(b_, s_, ATTN_WIDTH)


def token_mixer(h, w_in, w_pool_grp, pool_scale, w_up_pool, w_up_attn, w_out, rope_a, rope_i):
    b_, s_, _ = h.shape
    proj = h @ w_in
    offs = [int(o) for o in np.cumsum(IN_SPLITS)[:-1]]
    u, q, k, v, qi, ki, wi, gl = jnp.split(proj, offs, axis=-1)
    q = partial_rope(q.reshape(b_, s_, N_HEADS, HEAD_DIM), *rope_a)
    k = partial_rope(k.reshape(b_, s_, N_KV_HEADS, HEAD_DIM), *rope_a)
    v = v.reshape(b_, s_, N_KV_HEADS, HEAD_DIM)
    qi = partial_rope(qi.reshape(b_, s_, IDX_HEADS, IDX_DIM), *rope_i)
    ki = partial_rope(ki, *rope_i)
    wi = wi * (IDX_HEADS ** -0.5 * IDX_DIM ** -0.5)
    y_pool = pool_mixer(u, w_pool_grp, pool_scale)
    y_attn = dsa_attention(q, k, v, qi, ki, wi)
    g_pool, g_attn = jnp.split(jax.nn.sigmoid(gl), N_BRANCHES, axis=-1)
    merged = g_pool * (y_pool @ w_up_pool) + g_attn * (y_attn @ w_up_attn)
    return merged @ w_out


def moe_ffn(h, w_router, b_router, w1, b1, w2, b2):
    b_, s_, d_ = h.shape
    xt = h.reshape(-1, d_)
    n_tok = xt.shape[0]
    logits = (xt @ w_router + b_router).astype(jnp.float32)
    top_val, top_idx = lax.top_k(logits, TOP_K)
    gates = jax.nn.softmax(top_val, axis=-1)
    n_slots = n_tok * TOP_K
    e_flat = top_idx.reshape(-1)
    order = jnp.argsort(e_flat)
    sorted_e = e_flat[order]
    sorted_tok = order // TOP_K
    counts = jnp.zeros((N_EXPERTS,), jnp.int32).at[e_flat].add(1)
    starts = jnp.cumsum(counts) - counts
    padded = (counts + MOE_BLOCK - 1) // MOE_BLOCK * MOE_BLOCK
    pad_end = jnp.cumsum(padded)
    pad_start = pad_end - padded
    dest_sorted = pad_start[sorted_e] + (jnp.arange(n_slots) - starts[sorted_e])
    n_blocks = -(-n_slots // MOE_BLOCK) + N_EXPERTS
    n_rows = n_blocks * MOE_BLOCK
    row_tok = jnp.zeros((n_rows,), jnp.int32).at[dest_sorted].set(sorted_tok.astype(jnp.int32))
    row_gate = jnp.zeros((n_rows,), jnp.float32).at[dest_sorted].set(gates.reshape(-1)[order])
    blk_start = jnp.arange(n_blocks) * MOE_BLOCK
    blk_expert = jnp.minimum(jnp.searchsorted(pad_end, blk_start, side='right'), N_EXPERTS - 1)
    xs = xt[row_tok].reshape(n_blocks, MOE_BLOCK, d_)

    def expert_block(args):
        xb, e = args
        hb = xb @ w1[e] + b1[e]
        glu = jnp.minimum(hb[:, :D_FF], SWIGLU_LIMIT)
        lin = jnp.clip(hb[:, D_FF:], -SWIGLU_LIMIT, SWIGLU_LIMIT)
        act = glu * jax.nn.sigmoid(SWIGLU_ALPHA * glu) * (lin + 1)
        return act @ w2[e] + b2[e]

    ys = lax.map(expert_block, (xs, blk_expert)).reshape(n_rows, d_)
    y = jax.ops.segment_sum(ys * row_gate[:, None].astype(ys.dtype), row_tok, num_segments=n_tok)
    return y.reshape(b_, s_, d_)


def setup_inputs(seed: int = 0) -> dict:
    key = jax.random.key(seed)
    ks = jax.random.split(key, 20)
    f32 = jnp.float32
    nrm = lambda k, shape, s: jax.random.normal(k, shape, f32) * s
    L = DEPTH
    return {
        'x': nrm(ks[0], (BATCH, SEQ, D_MODEL), 1.0),
        'c': nrm(ks[1], (BATCH, D_MODEL), 1.0),
        'w_ada': nrm(ks[2], (L, D_MODEL, N_MOD * D_MODEL), 0.5 * D_MODEL ** -0.5),
        'b_ada': nrm(ks[3], (L, N_MOD * D_MODEL), 0.01),
        'g_pre_mix': 1.0 + nrm(ks[4], (L, D_MODEL), 0.02),
        'g_post_mix': 1.0 + nrm(ks[5], (L, D_MODEL), 0.02),
        'w_in': nrm(ks[6], (L, D_MODEL, IN_WIDTH), D_MODEL ** -0.5),
        'w_pool_grp': nrm(ks[7], (L, POOL_GROUPS, POOL_GROUP_DIM, POOL_GROUP_DIM), POOL_GROUP_DIM ** -0.5),
        'pool_scale': 1.0 + nrm(ks[8], (L, POOL_WIDTH), 0.02),
        'w_up_pool': nrm(ks[9], (L, POOL_WIDTH, D_MODEL), POOL_WIDTH ** -0.5),
        'w_up_attn': nrm(ks[10], (L, ATTN_WIDTH, D_MODEL), ATTN_WIDTH ** -0.5),
        'w_out': nrm(ks[11], (L, D_MODEL, D_MODEL), D_MODEL ** -0.5),
        'g_pre_ffn': 1.0 + nrm(ks[12], (L, D_MODEL), 0.02),
        'g_post_ffn': 1.0 + nrm(ks[13], (L, D_MODEL), 0.02),
        'w_router': nrm(ks[14], (L, D_MODEL, N_EXPERTS), D_MODEL ** -0.5),
        'b_router': nrm(ks[15], (L, N_EXPERTS), 0.01),
        'w1': nrm(ks[16], (L, N_EXPERTS, D_MODEL, 2 * D_FF), D_MODEL ** -0.5),
        'b1': nrm(ks[17], (L, N_EXPERTS, 2 * D_FF), 0.01),
        'w2': nrm(ks[18], (L, N_EXPERTS, D_FF, D_MODEL), D_FF ** -0.5),
        'b2': nrm(ks[19], (L, N_EXPERTS, D_MODEL), 0.01),
    }


def reference(x, c, w_ada, b_ada, g_pre_mix, g_post_mix, w_in, w_pool_grp, pool_scale, w_up_pool, w_up_attn, w_out, g_pre_ffn, g_post_ffn, w_router, b_router, w1, b1, w2, b2):
    s_ = x.shape[1]
    rope_a = rope_tables(s_, ROT_DIM)
    rope_i = rope_tables(s_, IDX_ROT_DIM)
    c_act = jax.nn.silu(c)
    for layer in range(DEPTH):
        mod = (c_act @ w_ada[layer] + b_ada[layer])[:, None, :]
        sh1, sc1, gt1, sh2, sc2, gt2 = jnp.split(mod, N_MOD, axis=-1)
        h = rms_norm(x, g_pre_mix[layer]) * (1 + sc1) + sh1
        y = token_mixer(h, w_in[layer], w_pool_grp[layer], pool_scale[layer], w_up_pool[layer], w_up_attn[layer], w_out[layer], rope_a, rope_i)
        x = x + gt1 * rms_norm(y, g_post_mix[layer])
        h = rms_norm(x, g_pre_ffn[layer]) * (1 + sc2) + sh2
        y = moe_ffn(h, w_router[layer], b_router[layer], w1[layer], b1[layer], w2[layer], b2[layer])
        x = x + gt2 * rms_norm(y, g_post_ffn[layer])
    return x
```

```python
import functools

import jax
import jax.numpy as jnp
import numpy as np
from jax import lax
from jax.experimental import pallas as pl
from jax.experimental.pallas import tpu as pltpu

F32 = jnp.float32
BF16 = jnp.bfloat16
I32 = jnp.int32
U32 = jnp.uint32

D_MODEL = 2048
POOL_WINDOWS = (2, 4, 8, 16)
POOL_GROUPS = 4
POOL_GROUP_DIM = D_MODEL // 8
POOL_WIDTH = POOL_GROUPS * POOL_GROUP_DIM
N_HEADS = 16
N_KV_HEADS = 4
HEAD_DIM = 128
ATTN_WIDTH = N_HEADS * HEAD_DIM
KV_WIDTH = N_KV_HEADS * HEAD_DIM
ROT_DIM = HEAD_DIM // 4
IDX_HEADS = 16
IDX_DIM = 64
IDX_ROT_DIM = IDX_DIM // 4
TOPK_MAX = 256
ROPE_THETA = 500000.0
N_BRANCHES = 2
N_EXPERTS = 32
TOP_K = 4
D_FF = D_MODEL
SWIGLU_ALPHA = 1.702
SWIGLU_LIMIT = 7.0
N_MOD = 6
EPS = 1e-6

LANES = 128
INT_MIN = np.int32(-(2 ** 31))

GL_OFF = 0
Q_OFF = GL_OFF + N_BRANCHES * D_MODEL
U_OFF = Q_OFF + ATTN_WIDTH
QI_OFF = U_OFF + POOL_WIDTH
K_OFF = QI_OFF + IDX_HEADS * IDX_DIM
V_OFF = K_OFF + KV_WIDTH
PROJ_WIDTH = V_OFF + KV_WIDTH

IN_TN = 1024
ATTN_TQ = 128
MERGE_TM = 256
MOE_BM = 512
FFN1_TN = 512
FFN2_TN = 1024
DISPATCH_TM = 256
COMBINE_TM = 128
VMEM_LIMIT = 56 * 1024 * 1024


def _cparams(sem):
    return pltpu.CompilerParams(dimension_semantics=sem, vmem_limit_bytes=VMEM_LIMIT)


def _rms(x, g):
    return x * lax.rsqrt(jnp.mean(x * x, axis=-1, keepdims=True) + EPS) * g


def _ada_kernel(c_ref, w_ref, b_ref, o_ref):
    c = c_ref[...]
    ca = (c * jax.nn.sigmoid(c)).astype(BF16)
    o_ref[...] = jnp.dot(ca, w_ref[...].astype(BF16), preferred_element_type=F32) + b_ref[...]


def _ada(c, w_ada, b_ada):
    bsz = c.shape[0]
    n = w_ada.shape[1]
    tn = 1024
    return pl.pallas_call(
        _ada_kernel,
        grid=(n // tn,),
        in_specs=[
            pl.BlockSpec((bsz, D_MODEL), lambda j: (0, 0)),
            pl.BlockSpec((D_MODEL, tn), lambda j: (0, j)),
            pl.BlockSpec((1, tn), lambda j: (0, j)),
        ],
        out_specs=pl.BlockSpec((bsz, tn), lambda j: (0, j)),
        out_shape=jax.ShapeDtypeStruct((bsz, n), F32),
        compiler_params=_cparams(("arbitrary",)),
        name="ada",
    )(c, w_ada, b_ada.reshape(1, n))


def _rope_slice(xs, c, s1, s2, shift):
    return xs * c + pltpu.roll(xs, LANES - shift, 1) * s1 + pltpu.roll(xs, shift, 1) * s2


def _in_kernel(x_ref, mod_ref, g_ref, w_ref, ws_ref, tab_ref, o_ref, os_ref, h_ref, acc_ref):
    j = pl.program_id(1)
    half_a = ROT_DIM // 2
    half_i = IDX_ROT_DIM // 2

    @pl.when(j == 0)
    def _():
        m = mod_ref[0]
        h = _rms(x_ref[...], g_ref[...]) * (1.0 + m[1:2]) + m[0:1]
        hb = h.astype(BF16)
        h_ref[...] = hb
        small = jnp.dot(hb, ws_ref[...], preferred_element_type=F32)
        os_ref[...] = _rope_slice(small, tab_ref[6], tab_ref[7], tab_ref[8], half_i)

    acc_ref[...] = jnp.dot(h_ref[...], w_ref[...], preferred_element_type=F32)
    n_sl = IN_TN // LANES
    j_q0, j_q1 = Q_OFF // IN_TN, U_OFF // IN_TN
    j_qi = QI_OFF // IN_TN
    j_kv = K_OFF // IN_TN
    k_sl = KV_WIDTH // LANES

    @pl.when((j < j_q0) | (j == U_OFF // IN_TN))
    def _():
        o_ref[...] = acc_ref[...].astype(BF16)

    @pl.when((j >= j_q0) & (j < j_q1))
    def _():
        for s in range(n_sl):
            sl = slice(s * LANES, (s + 1) * LANES)
            o_ref[:, sl] = _rope_slice(acc_ref[:, sl], tab_ref[0], tab_ref[1], tab_ref[2], half_a).astype(BF16)

    @pl.when(j == j_qi)
    def _():
        for s in range(n_sl):
            sl = slice(s * LANES, (s + 1) * LANES)
            o_ref[:, sl] = _rope_slice(acc_ref[:, sl], tab_ref[3], tab_ref[4], tab_ref[5], half_i).astype(BF16)

    @pl.when(j == j_kv)
    def _():
        for s in range(k_sl):
            sl = slice(s * LANES, (s + 1) * LANES)
            o_ref[:, sl] = _rope_slice(acc_ref[:, sl], tab_ref[0], tab_ref[1], tab_ref[2], half_a).astype(BF16)
        o_ref[:, KV_WIDTH:] = acc_ref[:, KV_WIDTH:].astype(BF16)


def _rope_tables(seq):
    def tabs(rot_dim, period, n_rep):
        half = rot_dim // 2
        inv = ROPE_THETA ** (-jnp.arange(0, rot_dim, 2, dtype=F32) / rot_dim)
        ang = jnp.arange(seq, dtype=F32)[:, None] * inv[None, :]
        cos, sin = jnp.cos(ang), jnp.sin(ang)
        zeros = jnp.zeros((seq, period - 2 * half), F32)
        z_h = jnp.zeros((seq, half), F32)
        c = jnp.concatenate([cos, cos, zeros + 1.0], axis=1)
        s1 = jnp.concatenate([-sin, z_h, zeros], axis=1)
        s2 = jnp.concatenate([z_h, sin, zeros], axis=1)
        return [jnp.tile(t, (1, n_rep)) for t in (c, s1, s2)]

    ta = tabs(ROT_DIM, HEAD_DIM, LANES // HEAD_DIM)
    ti = tabs(IDX_ROT_DIM, IDX_DIM, LANES // IDX_DIM)
    ts = tabs(IDX_ROT_DIM, IDX_DIM, 1)
    wi_scale = IDX_HEADS ** -0.5 * IDX_DIM ** -0.5
    pad = LANES - IDX_DIM
    lane = jnp.arange(pad)
    c_tail = jnp.where(lane < IDX_HEADS, wi_scale, 0.0).astype(F32)
    ts = [
        jnp.concatenate([ts[0], jnp.broadcast_to(c_tail, (seq, pad))], axis=1),
        jnp.concatenate([ts[1], jnp.zeros((seq, pad), F32)], axis=1),
        jnp.concatenate([ts[2], jnp.zeros((seq, pad), F32)], axis=1),
    ]
    return jnp.stack(ta + ti + ts, axis=0)


def _in_proj(x2, mod3, g_pre, w_in, seq):
    m_rows = x2.shape[0]
    tm = min(1024, seq)
    assert seq % tm == 0 and m_rows % tm == 0
    per_seq = seq // tm
    offs = np.cumsum((POOL_WIDTH, ATTN_WIDTH, KV_WIDTH, KV_WIDTH, IDX_HEADS * IDX_DIM, IDX_DIM, IDX_HEADS))
    w_u, w_q, w_k, w_v, w_qi, w_ki, w_wi, w_gl = jnp.split(w_in, [int(o) for o in offs], axis=1)
    w_main = jnp.concatenate([w_gl, w_q, w_u, w_qi, w_k, w_v], axis=1).astype(BF16)
    w_small = jnp.concatenate(
        [w_ki, w_wi, jnp.zeros((D_MODEL, LANES - IDX_DIM - IDX_HEADS), F32)], axis=1
    ).astype(BF16)
    tabs = _rope_tables(seq)
    n_j = PROJ_WIDTH // IN_TN
    return pl.pallas_call(
        _in_kernel,
        grid=(m_rows // tm, n_j),
        in_specs=[
            pl.BlockSpec((tm, D_MODEL), lambda i, j: (i, 0)),
            pl.BlockSpec((1, N_MOD, D_MODEL), lambda i, j: (i // per_seq, 0, 0)),
            pl.BlockSpec((1, D_MODEL), lambda i, j: (0, 0)),
            pl.BlockSpec((D_MODEL, IN_TN), lambda i, j: (0, j)),
            pl.BlockSpec((D_MODEL, LANES), lambda i, j: (0, 0)),
            pl.BlockSpec((9, tm, LANES), lambda i, j: (0, i % per_seq, 0)),
        ],
        out_specs=[
            pl.BlockSpec((tm, IN_TN), lambda i, j: (i, j)),
            pl.BlockSpec((tm, LANES), lambda i, j: (i, 0)),
        ],
        out_shape=[
            jax.ShapeDtypeStruct((m_rows, PROJ_WIDTH), BF16),
            jax.ShapeDtypeStruct((m_rows, LANES), F32),
        ],
        scratch_shapes=[pltpu.VMEM((tm, D_MODEL), BF16), pltpu.VMEM((tm, IN_TN), F32)],
        compiler_params=_cparams(("arbitrary", "arbitrary")),
        name="in_proj",
    )(x2, mod3, g_pre.reshape(1, D_MODEL), w_main, w_small, tabs)


def _attn_kernel(q_ref, qi_ref, k_ref, v_ref, kis_ref, wis_ref, o_ref, key_ref, bias_ref, *, seq, n_keep):
    tq = ATTN_TQ
    i = pl.program_id(1)
    nt = (((1,), (1,)), ((), ()))

    ki = kis_ref[:, :IDX_DIM].astype(BF16)
    w_t = wis_ref[...].T
    sc = jnp.zeros((seq, tq), F32)
    for h in range(IDX_HEADS):
        qh = qi_ref[:, h * IDX_DIM:(h + 1) * IDX_DIM]
        d = lax.dot_general(ki, qh, nt, preferred_element_type=F32)
        sc = sc + jnp.maximum(d, 0.0) * w_t[IDX_DIM + h:IDX_DIM + h + 1, :]

    bits = lax.bitcast_convert_type(sc, I32)
    key = jnp.where(bits < 0, bits ^ np.int32(0x7FFFFFFF), bits)
    row = lax.broadcasted_iota(I32, (seq, tq), 0)
    tpos = i * tq + lax.broadcasted_iota(I32, (seq, tq), 1)
    causal = row <= tpos
    key_ref[...] = jnp.where(causal, key, INT_MIN)

    def bit_step(it, tu):
        cand = tu | lax.shift_left(np.int32(1), 31 - it)
        cnt = jnp.sum((key_ref[...] >= (cand ^ INT_MIN)).astype(I32), axis=0, keepdims=True)
        return jnp.where(cnt >= n_keep, cand, tu)

    tu = lax.fori_loop(0, 32, bit_step, jnp.zeros((1, tq), I32))
    thr = tu ^ INT_MIN
    keyc = key_ref[...]
    ge = keyc >= thr
    bias_ref[...] = jnp.where(ge & causal, 0.0, -jnp.inf).astype(F32)
    cnt_ge = jnp.sum(ge.astype(I32), axis=0, keepdims=True)
    tie_i = ((cnt_ge > n_keep) & (tu != 0)).astype(I32)

    @pl.when(jnp.max(tie_i) > 0)
    def _():
        kc = key_ref[...]
        row_b = lax.broadcasted_iota(I32, (seq, tq), 0)
        need = n_keep - jnp.sum((kc > thr).astype(I32), axis=0, keepdims=True)

        def idx_step(it, p):
            cand = p | lax.shift_left(np.int32(1), (seq.bit_length() - 2) - it)
            below = jnp.sum(((key_ref[...] == thr) & (row_b < cand)).astype(I32), axis=0, keepdims=True)
            return jnp.where(below < need, cand, p)

        p = lax.fori_loop(0, seq.bit_length() - 1, idx_step, jnp.zeros((1, tq), I32))
        p = jnp.where(tie_i > 0, p, seq)
        sel = (kc > thr) | ((kc == thr) & (row_b <= p))
        sel = sel & (row_b <= i * tq + lax.broadcasted_iota(I32, (seq, tq), 1))
        bias_ref[...] = jnp.where(sel, 0.0, -jnp.inf).astype(F32)

    bias_t = bias_ref[...].T
    grp = N_HEADS // N_KV_HEADS
    scale = HEAD_DIM ** -0.5
    for g in range(N_KV_HEADS):
        qs = jnp.concatenate(
            [q_ref[:, (g * grp + r) * HEAD_DIM:(g * grp + r + 1) * HEAD_DIM] for r in range(grp)], axis=0
        )
        kg = k_ref[:, g * HEAD_DIM:(g + 1) * HEAD_DIM]
        vg = v_ref[:, g * HEAD_DIM:(g + 1) * HEAD_DIM]
        logits = lax.dot_general(qs, kg, nt, preferred_element_type=F32) * scale
        logits = logits.reshape(grp, tq, seq) + bias_t[None]
        mx = jnp.max(logits, axis=-1, keepdims=True)
        p = jnp.exp(logits - mx)
        den = jnp.sum(p, axis=-1, keepdims=True)
        o = jnp.dot(p.astype(BF16).reshape(grp * tq, seq), vg, preferred_element_type=F32)
        o = o.reshape(grp, tq, HEAD_DIM) / den
        for r in range(grp):
            hh = g * grp + r
            o_ref[:, hh * HEAD_DIM:(hh + 1) * HEAD_DIM] = o[r].astype(BF16)


def _attn(proj, small, bsz, seq):
    tq = ATTN_TQ
    assert seq % tq == 0 and seq & (seq - 1) == 0
    nq = seq // tq
    n_keep = min(TOPK_MAX, seq // 4)
    kern = functools.partial(_attn_kernel, seq=seq, n_keep=n_keep)
    return pl.pallas_call(
        kern,
        grid=(bsz, nq),
        in_specs=[
            pl.BlockSpec((tq, ATTN_WIDTH), lambda b, i: (b * nq + i, Q_OFF // ATTN_WIDTH)),
            pl.BlockSpec((tq, IDX_HEADS * IDX_DIM), lambda b, i: (b * nq + i, QI_OFF // (IDX_HEADS * IDX_DIM))),
            pl.BlockSpec((seq, KV_WIDTH), lambda b, i: (b, K_OFF // KV_WIDTH)),
            pl.BlockSpec((seq, KV_WIDTH), lambda b, i: (b, V_OFF // KV_WIDTH)),
            pl.BlockSpec((seq, LANES), lambda b, i: (b, 0)),
            pl.BlockSpec((tq, LANES), lambda b, i: (b * nq + i, 0)),
        ],
        out_specs=pl.BlockSpec((tq, ATTN_WIDTH), lambda b, i: (b * nq + i, 0)),
        out_shape=jax.ShapeDtypeStruct((bsz * seq, ATTN_WIDTH), BF16),
        scratch_shapes=[pltpu.VMEM((seq, tq), I32), pltpu.VMEM((seq, tq), F32)],
        compiler_params=_cparams(("arbitrary", "arbitrary")),
        name="attn",
    )(proj, proj, proj, proj, small, small)


def _merge_kernel(
    u_ref, halo_ref, ya_ref, gl_ref, x_ref, mod_ref, wgrp_ref, pscale_ref, wup_p_ref, wup_a_ref, wout_ref,
    gpost_ref, gffn_ref, wr_ref, br_ref,
    x1_ref, hp_ref, route_ref, cnt_ref,
    carry_ref, *, seq,
):
    tm = MERGE_TM
    i = pl.program_id(0)
    pos0 = (i * tm) % seq

    @pl.when(i == 0)
    def _():
        carry_ref[...] = jnp.zeros_like(carry_ref)

    halo_rows = halo_ref.shape[0]
    halo = jnp.where(pos0 == 0, 0.0, halo_ref[...].astype(F32))
    ext = jnp.concatenate([halo, u_ref[...].astype(F32)], axis=0)
    pos = pos0 + lax.broadcasted_iota(I32, (tm, 1), 0)
    ys = []
    for g, win in enumerate(POOL_WINDOWS):
        sl = slice(g * POOL_GROUP_DIM, (g + 1) * POOL_GROUP_DIM)
        e = ext[:, sl]
        acc = e
        span = 1
        while span < win:
            acc = acc + pltpu.roll(acc, span, 0)
            span *= 2
        cnt = jnp.minimum(pos + 1, win).astype(F32)
        mixed = (acc[halo_rows:] / cnt - e[halo_rows:]).astype(BF16)
        ys.append(jnp.dot(mixed, wgrp_ref[g], preferred_element_type=F32))
    y_pool = (jnp.concatenate(ys, axis=1) * pscale_ref[...]).astype(BF16)

    gates = jax.nn.sigmoid(gl_ref[...].astype(F32))
    up_p = jnp.dot(y_pool, wup_p_ref[...], preferred_element_type=F32)
    up_a = jnp.dot(ya_ref[...], wup_a_ref[...], preferred_element_type=F32)
    merged = (gates[:, :D_MODEL] * up_p + gates[:, D_MODEL:] * up_a).astype(BF16)
    y = jnp.dot(merged, wout_ref[...], preferred_element_type=F32)
    m = mod_ref[0]
    x1 = x_ref[...] + m[2:3] * _rms(y, gpost_ref[...])
    x1_ref[...] = x1

    h2 = (_rms(x1, gffn_ref[...]) * (1.0 + m[4:5]) + m[3:4]).astype(BF16)
    hbits = lax.bitcast_convert_type(h2.astype(F32), U32)
    half = D_MODEL // 2
    hp_ref[...] = hbits[:, half:] | lax.shift_right_logical(hbits[:, :half], np.uint32(16))

    lane = lax.broadcasted_iota(I32, (tm, LANES), 1)
    logits = jnp.dot(h2, wr_ref[...], preferred_element_type=F32) + br_ref[...]
    work = jnp.where(lane < N_EXPERTS, logits, -jnp.inf)
    vals, hots = [], []
    for _ in range(TOP_K):
        mx = jnp.max(work, axis=-1, keepdims=True)
        first = jnp.min(jnp.where(work == mx, lane, LANES), axis=-1, keepdims=True)
        hot = lane == first
        vals.append(mx)
        hots.append(hot)
        work = jnp.where(hot, -jnp.inf, work)
    ex = [jnp.exp(v - vals[0]) for v in vals]
    den = ex[0] + ex[1] + ex[2] + ex[3]

    onehot = jnp.zeros((tm, LANES), F32)
    for hot in hots:
        onehot = onehot + hot.astype(F32)
    r_i = lax.broadcasted_iota(I32, (tm, tm), 0)
    c_i = lax.broadcasted_iota(I32, (tm, tm), 1)
    tril = (c_i < r_i).astype(BF16)
    before = jnp.dot(tril, onehot.astype(BF16), preferred_element_type=F32) + carry_ref[0:1, :]
    out = jnp.zeros((tm, LANES), F32)
    lane_f = lane.astype(F32)
    for k in range(TOP_K):
        out = jnp.where(lane == k, ex[k] / den, out)
        idx_k = jnp.sum(jnp.where(hots[k], lane_f, 0.0), axis=-1, keepdims=True)
        out = jnp.where(lane == TOP_K + k, idx_k, out)
        rank_k = jnp.sum(jnp.where(hots[k], before, 0.0), axis=-1, keepdims=True)
        out = jnp.where(lane == 2 * TOP_K + k, rank_k, out)
    route_ref[...] = out
    total = carry_ref[0:1, :] + jnp.sum(onehot, axis=0, keepdims=True)
    carry_ref[...] = jnp.broadcast_to(total, carry_ref.shape)
    cnt_ref[...] = jnp.broadcast_to(total, cnt_ref.shape)


def _merge(proj, y_attn, x2, mod3, w_pool_grp, pool_scale, w_up_pool, w_up_attn, w_out, g_post_mix, g_pre_ffn,
           w_router, b_router, seq):
    m_rows = x2.shape[0]
    tm = MERGE_TM
    halo = 16
    assert seq % tm == 0 and max(POOL_WINDOWS) <= halo
    per_seq = seq // tm
    wr = jnp.concatenate([w_router, jnp.zeros((D_MODEL, LANES - N_EXPERTS), F32)], axis=1).astype(BF16)
    br = jnp.concatenate([b_router, jnp.zeros((LANES - N_EXPERTS,), F32)]).reshape(1, LANES)
    const = lambda shape: pl.BlockSpec(shape, lambda i: (0,) * len(shape), pipeline_mode=pl.Buffered(1))
    kern = functools.partial(_merge_kernel, seq=seq)
    return pl.pallas_call(
        kern,
        grid=(m_rows // tm,),
        in_specs=[
            pl.BlockSpec((tm, POOL_WIDTH), lambda i: (i, U_OFF // POOL_WIDTH)),
            pl.BlockSpec((halo, POOL_WIDTH), lambda i: (jnp.maximum(i * (tm // halo) - 1, 0), U_OFF // POOL_WIDTH)),
            pl.BlockSpec((tm, ATTN_WIDTH), lambda i: (i, 0)),
            pl.BlockSpec((tm, N_BRANCHES * D_MODEL), lambda i: (i, 0)),
            pl.BlockSpec((tm, D_MODEL), lambda i: (i, 0)),
            pl.BlockSpec((1, N_MOD, D_MODEL), lambda i: (i // per_seq, 0, 0)),
            const((POOL_GROUPS, POOL_GROUP_DIM, POOL_GROUP_DIM)),
            const((1, POOL_WIDTH)),
            const((POOL_WIDTH, D_MODEL)),
            const((ATTN_WIDTH, D_MODEL)),
            const((D_MODEL, D_MODEL)),
            const((1, D_MODEL)),
            const((1, D_MODEL)),
            const((D_MODEL, LANES)),
            const((1, LANES)),
        ],
        out_specs=[
            pl.BlockSpec((tm, D_MODEL), lambda i: (i, 0)),
            pl.BlockSpec((tm, D_MODEL // 2), lambda i: (i, 0)),
            pl.BlockSpec((tm, LANES), lambda i: (i, 0)),
            pl.BlockSpec((8, LANES), lambda i: (0, 0)),
        ],
        out_shape=[
            jax.ShapeDtypeStruct((m_rows, D_MODEL), F32),
            jax.ShapeDtypeStruct((m_rows, D_MODEL // 2), U32),
            jax.ShapeDtypeStruct((m_rows, LANES), F32),
            jax.ShapeDtypeStruct((8, LANES), F32),
        ],
        scratch_shapes=[pltpu.VMEM((8, LANES), F32)],
        compiler_params=_cparams(("arbitrary",)),
        name="merge",
    )(
        proj, proj, y_attn, proj, x2, mod3,
        w_pool_grp.astype(BF16), pool_scale.reshape(1, POOL_WIDTH), w_up_pool.astype(BF16),
        w_up_attn.astype(BF16), w_out.astype(BF16), g_post_mix.reshape(1, D_MODEL), g_pre_ffn.reshape(1, D_MODEL),
        wr, br,
    )


def _dispatch_kernel(pos_ref, h_ref, xs_in_ref, xs_ref, sem):
    del xs_in_ref
    tm = h_ref.shape[0]

    def row_copy(t, k):
        return pltpu.make_async_copy(
            h_ref.at[pl.ds(t, 1), :], xs_ref.at[pl.ds(pos_ref[t * TOP_K + k], 1), :], sem
        )

    def issue(t, carry):
        for k in range(TOP_K):
            row_copy(t, k).start()
        return carry

    def drain(t, carry):
        for k in range(TOP_K):
            row_copy(t, k).wait()
        return carry

    lax.fori_loop(0, tm, issue, 0)
    lax.fori_loop(0, tm, drain, 0)


def _dispatch(pos_flat, hp, n_rows):
    m_rows, width = hp.shape
    tm = DISPATCH_TM
    xs0 = jnp.zeros((n_rows, width), U32)
    return pl.pallas_call(
        _dispatch_kernel,
        grid=(m_rows // tm,),
        in_specs=[
            pl.BlockSpec((tm * TOP_K,), lambda i: (i,), memory_space=pltpu.SMEM),
            pl.BlockSpec((tm, width), lambda i: (i, 0)),
            pl.BlockSpec(memory_space=pl.ANY),
        ],
        out_specs=pl.BlockSpec(memory_space=pl.ANY),
        out_shape=jax.ShapeDtypeStruct((n_rows, width), U32),
        scratch_shapes=[pltpu.SemaphoreType.DMA(())],
        input_output_aliases={2: 0},
        compiler_params=_cparams(("arbitrary",)),
        name="dispatch",
    )(pos_flat, hp, xs0)


def _new_expert(be_ref, b):
    return (b == 0) | (be_ref[b] != be_ref[jnp.maximum(b - 1, 0)])


def _ffn1_kernel(be_ref, nu_ref, x_ref, wg_ref, wl_ref, bg_ref, bl_ref, o_ref, wg_s, wl_s):
    b = pl.program_id(1)

    @pl.when(_new_expert(be_ref, b))
    def _():
        wg_s[...] = wg_ref[0].astype(BF16)
        wl_s[...] = wl_ref[0].astype(BF16)

    @pl.when(b < nu_ref[0])
    def _():
        xp = x_ref[...]
        lo = lax.bitcast_convert_type(lax.shift_left(xp, np.uint32(16)), F32).astype(BF16)
        hi = lax.bitcast_convert_type(xp & np.uint32(0xFFFF0000), F32).astype(BF16)
        half = D_MODEL // 2

        def proj(w_s, b_ref):
            return (
                jnp.dot(lo, w_s[:half, :], preferred_element_type=F32)
                + jnp.dot(hi, w_s[half:, :], preferred_element_type=F32)
                + b_ref[0]
            )

        glu = jnp.minimum(proj(wg_s, bg_ref), SWIGLU_LIMIT)
        lin = jnp.clip(proj(wl_s, bl_ref), -SWIGLU_LIMIT, SWIGLU_LIMIT)
        o_ref[...] = (glu * jax.nn.sigmoid(SWIGLU_ALPHA * glu) * (lin + 1.0)).astype(BF16)

    @pl.when(b >= nu_ref[0])
    def _():
        o_ref[...] = jnp.zeros_like(o_ref)


def _ffn1(blk_expert, n_used, xs, w1, b1):
    n_rows = xs.shape[0]
    bm, tn = MOE_BM, FFN1_TN
    n_blocks = n_rows // bm
    nj = D_FF // tn
    b1r = b1.reshape(N_EXPERTS, 1, 2 * D_FF)
    row = lambda j, b, be, nu: jnp.minimum(b, nu[0] - 1)
    grid_spec = pltpu.PrefetchScalarGridSpec(
        num_scalar_prefetch=2,
        grid=(nj, n_blocks),
        in_specs=[
            pl.BlockSpec((bm, D_MODEL // 2), lambda j, b, be, nu: (row(j, b, be, nu), 0)),
            pl.BlockSpec((1, D_MODEL, tn), lambda j, b, be, nu: (be[b], 0, j)),
            pl.BlockSpec((1, D_MODEL, tn), lambda j, b, be, nu: (be[b], 0, nj + j)),
            pl.BlockSpec((1, 1, tn), lambda j, b, be, nu: (be[b], 0, j)),
            pl.BlockSpec((1, 1, tn), lambda j, b, be, nu: (be[b], 0, nj + j)),
        ],
        out_specs=pl.BlockSpec((bm, tn), lambda j, b, be, nu: (b, j)),
        scratch_shapes=[pltpu.VMEM((D_MODEL, tn), BF16), pltpu.VMEM((D_MODEL, tn), BF16)],
    )
    return pl.pallas_call(
        _ffn1_kernel,
        grid_spec=grid_spec,
        out_shape=jax.ShapeDtypeStruct((n_rows, D_FF), BF16),
        compiler_params=_cparams(("arbitrary", "arbitrary")),
        name="ffn1",
    )(blk_expert, n_used, xs, w1, w1, b1r, b1r)


def _ffn2_kernel(be_ref, nu_ref, a_ref, w_ref, b_ref, o_ref, w_s):
    b = pl.program_id(1)

    @pl.when(_new_expert(be_ref, b))
    def _():
        w_s[...] = w_ref[0].astype(BF16)

    @pl.when(b < nu_ref[0])
    def _():
        o_ref[...] = jnp.dot(a_ref[...], w_s[...], preferred_element_type=F32) + b_ref[0]

    @pl.when(b >= nu_ref[0])
    def _():
        o_ref[...] = jnp.zeros_like(o_ref)


def _ffn2(blk_expert, n_used, act, w2, b2):
    n_rows = act.shape[0]
    bm, tn = MOE_BM, FFN2_TN
    n_blocks = n_rows // bm
    nj = D_MODEL // tn
    b2r = b2.reshape(N_EXPERTS, 1, D_MODEL)
    row = lambda j, b, be, nu: jnp.minimum(b, nu[0] - 1)
    grid_spec = pltpu.PrefetchScalarGridSpec(
        num_scalar_prefetch=2,
        grid=(nj, n_blocks),
        in_specs=[
            pl.BlockSpec((bm, D_FF), lambda j, b, be, nu: (row(j, b, be, nu), 0)),
            pl.BlockSpec((1, D_FF, tn), lambda j, b, be, nu: (be[b], 0, j)),
            pl.BlockSpec((1, 1, tn), lambda j, b, be, nu: (be[b], 0, j)),
        ],
        out_specs=pl.BlockSpec((bm, tn), lambda j, b, be, nu: (b, j)),
        scratch_shapes=[pltpu.VMEM((D_FF, tn), BF16)],
    )
    return pl.pallas_call(
        _ffn2_kernel,
        grid_spec=grid_spec,
        out_shape=jax.ShapeDtypeStruct((n_rows, D_MODEL), F32),
        compiler_params=_cparams(("arbitrary", "arbitrary")),
        name="ffn2",
    )(blk_expert, n_used, act, w2, b2r)


def _combine_kernel(pos_ref, ys_ref, route_ref, x1_ref, mod_ref, g_ref, o_ref, buf, sem):
    tm = x1_ref.shape[0]

    def row_copy(t, k):
        return pltpu.make_async_copy(
            ys_ref.at[pl.ds(pos_ref[t * TOP_K + k], 1), :], buf.at[k, pl.ds(t, 1), :], sem
        )

    def issue(t, carry):
        for k in range(TOP_K):
            row_copy(t, k).start()
        return carry

    def drain(t, carry):
        for k in range(TOP_K):
            row_copy(t, k).wait()
        return carry

    lax.fori_loop(0, tm, issue, 0)
    lax.fori_loop(0, tm, drain, 0)
    route = route_ref[...]
    y = route[:, 0:1] * buf[0]
    for k in range(1, TOP_K):
        y = y + route[:, k:k + 1] * buf[k]
    m = mod_ref[0]
    o_ref[...] = x1_ref[...] + m[5:6] * _rms(y, g_ref[...])


def _combine(pos_flat, ys, route, x1, mod3, g_post_ffn, seq):
    m_rows = x1.shape[0]
    tm = COMBINE_TM
    per_seq = seq // tm
    return pl.pallas_call(
        _combine_kernel,
        grid=(m_rows // tm,),
        in_specs=[
            pl.BlockSpec((tm * TOP_K,), lambda i: (i,), memory_space=pltpu.SMEM),
            pl.BlockSpec(memory_space=pl.ANY),
            pl.BlockSpec((tm, LANES), lambda i: (i, 0)),
            pl.BlockSpec((tm, D_MODEL), lambda i: (i, 0)),
            pl.BlockSpec((1, N_MOD, D_MODEL), lambda i: (i // per_seq, 0, 0)),
            pl.BlockSpec((1, D_MODEL), lambda i: (0, 0)),
        ],
        out_specs=pl.BlockSpec((tm, D_MODEL), lambda i: (i, 0)),
        out_shape=jax.ShapeDtypeStruct((m_rows, D_MODEL), F32),
        scratch_shapes=[pltpu.VMEM((TOP_K, tm, D_MODEL), F32), pltpu.SemaphoreType.DMA(())],
        compiler_params=_cparams(("arbitrary",)),
        name="combine",
    )(pos_flat, ys, route, x1, mod3, g_post_ffn.reshape(1, D_MODEL))


def _layer(x2, mod3, bsz, seq, g_pre_mix, g_post_mix, w_in, w_pool_grp, pool_scale, w_up_pool, w_up_attn, w_out,
           g_pre_ffn, g_post_ffn, w_router, b_router, w1, b1, w2, b2):
    m_rows = x2.shape[0]
    proj, small = _in_proj(x2, mod3, g_pre_mix, w_in, seq)
    y_attn = _attn(proj, small, bsz, seq)
    x1, hp, route, cnt = _merge(proj, y_attn, x2, mod3, w_pool_grp, pool_scale, w_up_pool, w_up_attn, w_out,
                                g_post_mix, g_pre_ffn, w_router, b_router, seq)

    bm = MOE_BM
    n_slots = m_rows * TOP_K
    n_blocks = -(-n_slots // bm) + N_EXPERTS
    idx = route[:, TOP_K:2 * TOP_K].astype(I32)
    rank = route[:, 2 * TOP_K:3 * TOP_K].astype(I32)
    counts = cnt[0, :N_EXPERTS].astype(I32)
    padded = (counts + bm - 1) // bm * bm
    pad_end = jnp.cumsum(padded)
    pad_start = pad_end - padded
    pos_flat = (pad_start[idx] + rank).reshape(-1)
    n_used = (pad_end[-1] // bm).astype(I32).reshape(1)
    blk = jnp.minimum(jnp.arange(n_blocks, dtype=I32), n_used[0] - 1) * bm
    blk_expert = jnp.minimum(jnp.searchsorted(pad_end, blk, side="right"), N_EXPERTS - 1).astype(I32)

    xs = _dispatch(pos_flat, hp, n_blocks * bm)
    act = _ffn1(blk_expert, n_used, xs, w1, b1)
    ys = _ffn2(blk_expert, n_used, act, w2, b2)
    return _combine(pos_flat, ys, route, x1, mod3, g_post_ffn, seq)


def kernel(x, c, w_ada, b_ada, g_pre_mix, g_post_mix, w_in, w_pool_grp, pool_scale, w_up_pool, w_up_attn, w_out,
           g_pre_ffn, g_post_ffn, w_router, b_router, w1, b1, w2, b2):
    bsz, seq, d = x.shape
    assert d == D_MODEL
    depth = w_ada.shape[0]
    x2 = x.reshape(bsz * seq, d)
    for layer in range(depth):
        mod3 = _ada(c, w_ada[layer], b_ada[layer]).reshape(bsz, N_MOD, D_MODEL)
        x2 = _layer(x2, mod3, bsz, seq, g_pre_mix[layer], g_post_mix[layer], w_in[layer], w_pool_grp[layer],
                    pool_scale[layer], w_up_pool[layer], w_up_attn[layer], w_out[layer], g_pre_ffn[layer],
                    g_post_ffn[layer], w_router[layer], b_router[layer], w1[layer], b1[layer], w2[layer], b2[layer])
    return x2.reshape(bsz, seq, d)
```

```python
import functools

import jax
import jax.numpy as jnp
import numpy as np
from jax import lax
from jax.experimental import pallas as pl
from jax.experimental.pallas import tpu as pltpu

F32 = jnp.float32
BF16 = jnp.bfloat16
I32 = jnp.int32
U32 = jnp.uint32

D_MODEL = 2048
POOL_WINDOWS = (2, 4, 8, 16)
POOL_GROUPS = 4
POOL_GROUP_DIM = D_MODEL // 8
POOL_WIDTH = POOL_GROUPS * POOL_GROUP_DIM
N_HEADS = 16
N_KV_HEADS = 4
HEAD_DIM = 128
ATTN_WIDTH = N_HEADS * HEAD_DIM
KV_WIDTH = N_KV_HEADS * HEAD_DIM
ROT_DIM = HEAD_DIM // 4
IDX_HEADS = 16
IDX_DIM = 64
IDX_ROT_DIM = IDX_DIM // 4
TOPK_MAX = 256
ROPE_THETA = 500000.0
N_BRANCHES = 2
N_EXPERTS = 32
TOP_K = 4
D_FF = D_MODEL
SWIGLU_ALPHA = 1.702
SWIGLU_LIMIT = 7.0
N_MOD = 6
EPS = 1e-6

LANES = 128
INT_MIN = np.int32(-(2 ** 31))

GL_OFF = 0
Q_OFF = GL_OFF + N_BRANCHES * D_MODEL
U_OFF = Q_OFF + ATTN_WIDTH
QI_OFF = U_OFF + POOL_WIDTH
K_OFF = QI_OFF + IDX_HEADS * IDX_DIM
V_OFF = K_OFF + KV_WIDTH
PROJ_WIDTH = V_OFF + KV_WIDTH

IN_TN = 1024
ATTN_TQ = 256
ATTN_KBLK = 512
MERGE_TM = 256
MOE_BM = 512
FFN1_TN = 512
FFN2_TN = 1024
DISPATCH_TM = 256
COMBINE_TM = 128
VMEM_LIMIT = 56 * 1024 * 1024


def _cparams(sem):
    return pltpu.CompilerParams(dimension_semantics=sem, vmem_limit_bytes=VMEM_LIMIT)


def _rms(x, g):
    return x * lax.rsqrt(jnp.mean(x * x, axis=-1, keepdims=True) + EPS) * g


def _ada_kernel(c_ref, w_ref, b_ref, o_ref):
    c = c_ref[...]
    ca = (c * jax.nn.sigmoid(c)).astype(BF16)
    o_ref[...] = jnp.dot(ca, w_ref[...].astype(BF16), preferred_element_type=F32) + b_ref[...]


def _ada(c, w_ada, b_ada):
    bsz = c.shape[0]
    n = w_ada.shape[1]
    tn = 1024
    return pl.pallas_call(
        _ada_kernel,
        grid=(n // tn,),
        in_specs=[
            pl.BlockSpec((bsz, D_MODEL), lambda j: (0, 0)),
            pl.BlockSpec((D_MODEL, tn), lambda j: (0, j)),
            pl.BlockSpec((1, tn), lambda j: (0, j)),
        ],
        out_specs=pl.BlockSpec((bsz, tn), lambda j: (0, j)),
        out_shape=jax.ShapeDtypeStruct((bsz, n), F32),
        compiler_params=_cparams(("arbitrary",)),
        name="ada",
    )(c, w_ada, b_ada.reshape(1, n))


def _rope_slice(xs, c, s1, s2, shift):
    return xs * c + pltpu.roll(xs, LANES - shift, 1) * s1 + pltpu.roll(xs, shift, 1) * s2


def _in_kernel(x_ref, mod_ref, g_ref, w_ref, ws_ref, tab_ref, o_ref, os_ref, h_ref, acc_ref):
    j = pl.program_id(1)
    half_a = ROT_DIM // 2
    half_i = IDX_ROT_DIM // 2

    @pl.when(j == 0)
    def _():
        m = mod_ref[0]
        h = _rms(x_ref[...], g_ref[...]) * (1.0 + m[1:2]) + m[0:1]
        hb = h.astype(BF16)
        h_ref[...] = hb
        small = jnp.dot(hb, ws_ref[...], preferred_element_type=F32)
        os_ref[...] = _rope_slice(small, tab_ref[6], tab_ref[7], tab_ref[8], half_i)

    acc_ref[...] = jnp.dot(h_ref[...], w_ref[...], preferred_element_type=F32)
    n_sl = IN_TN // LANES
    j_q0, j_q1 = Q_OFF // IN_TN, U_OFF // IN_TN
    j_qi = QI_OFF // IN_TN
    j_kv = K_OFF // IN_TN
    k_sl = KV_WIDTH // LANES

    @pl.when((j < j_q0) | (j == U_OFF // IN_TN))
    def _():
        o_ref[...] = acc_ref[...].astype(BF16)

    @pl.when((j >= j_q0) & (j < j_q1))
    def _():
        for s in range(n_sl):
            sl = slice(s * LANES, (s + 1) * LANES)
            o_ref[:, sl] = _rope_slice(acc_ref[:, sl], tab_ref[0], tab_ref[1], tab_ref[2], half_a).astype(BF16)

    @pl.when(j == j_qi)
    def _():
        for s in range(n_sl):
            sl = slice(s * LANES, (s + 1) * LANES)
            o_ref[:, sl] = _rope_slice(acc_ref[:, sl], tab_ref[3], tab_ref[4], tab_ref[5], half_i).astype(BF16)

    @pl.when(j == j_kv)
    def _():
        for s in range(k_sl):
            sl = slice(s * LANES, (s + 1) * LANES)
            o_ref[:, sl] = _rope_slice(acc_ref[:, sl], tab_ref[0], tab_ref[1], tab_ref[2], half_a).astype(BF16)
        o_ref[:, KV_WIDTH:] = acc_ref[:, KV_WIDTH:].astype(BF16)


def _rope_tables(seq):
    def tabs(rot_dim, period, n_rep):
        half = rot_dim // 2
        inv = np.float32(ROPE_THETA) ** (-np.arange(0, rot_dim, 2, dtype=np.float32) / np.float32(rot_dim))
        ang = np.arange(seq, dtype=np.float32)[:, None] * inv.astype(np.float32)[None, :]
        cos, sin = np.cos(ang).astype(np.float32), np.sin(ang).astype(np.float32)
        ones = np.ones((seq, period - 2 * half), np.float32)
        z_h = np.zeros((seq, half), np.float32)
        c = np.concatenate([cos, cos, ones], axis=1)
        s1 = np.concatenate([-sin, z_h, 0 * ones], axis=1)
        s2 = np.concatenate([z_h, sin, 0 * ones], axis=1)
        return [np.tile(t, (1, n_rep)) for t in (c, s1, s2)]

    ta = tabs(ROT_DIM, HEAD_DIM, LANES // HEAD_DIM)
    ti = tabs(IDX_ROT_DIM, IDX_DIM, LANES // IDX_DIM)
    ts = tabs(IDX_ROT_DIM, IDX_DIM, 1)
    wi_scale = np.float32(IDX_HEADS ** -0.5 * IDX_DIM ** -0.5)
    pad = LANES - IDX_DIM
    c_tail = np.where(np.arange(pad) < IDX_HEADS, wi_scale, np.float32(0.0)).astype(np.float32)
    ts = [
        np.concatenate([ts[0], np.broadcast_to(c_tail, (seq, pad))], axis=1),
        np.concatenate([ts[1], np.zeros((seq, pad), np.float32)], axis=1),
        np.concatenate([ts[2], np.zeros((seq, pad), np.float32)], axis=1),
    ]
    return jnp.asarray(np.stack(ta + ti + ts, axis=0))


def _in_proj(x2, mod3, g_pre, w_in, seq):
    m_rows = x2.shape[0]
    tm = min(1024, seq)
    assert seq % tm == 0 and m_rows % tm == 0
    per_seq = seq // tm
    offs = np.cumsum((POOL_WIDTH, ATTN_WIDTH, KV_WIDTH, KV_WIDTH, IDX_HEADS * IDX_DIM, IDX_DIM, IDX_HEADS))
    w_u, w_q, w_k, w_v, w_qi, w_ki, w_wi, w_gl = jnp.split(w_in, [int(o) for o in offs], axis=1)
    w_main = jnp.concatenate([w_gl, w_q, w_u, w_qi, w_k, w_v], axis=1).astype(BF16)
    w_small = jnp.concatenate(
        [w_ki, w_wi, jnp.zeros((D_MODEL, LANES - IDX_DIM - IDX_HEADS), F32)], axis=1
    ).astype(BF16)
    tabs = _rope_tables(seq)
    n_j = PROJ_WIDTH // IN_TN
    return pl.pallas_call(
        _in_kernel,
        grid=(m_rows // tm, n_j),
        in_specs=[
            pl.BlockSpec((tm, D_MODEL), lambda i, j: (i, 0)),
            pl.BlockSpec((1, N_MOD, D_MODEL), lambda i, j: (i // per_seq, 0, 0)),
            pl.BlockSpec((1, D_MODEL), lambda i, j: (0, 0)),
            pl.BlockSpec((D_MODEL, IN_TN), lambda i, j: (0, j)),
            pl.BlockSpec((D_MODEL, LANES), lambda i, j: (0, 0)),
            pl.BlockSpec((9, tm, LANES), lambda i, j: (0, i % per_seq, 0)),
        ],
        out_specs=[
            pl.BlockSpec((tm, IN_TN), lambda i, j: (i, j)),
            pl.BlockSpec((tm, LANES), lambda i, j: (i, 0)),
        ],
        out_shape=[
            jax.ShapeDtypeStruct((m_rows, PROJ_WIDTH), BF16),
            jax.ShapeDtypeStruct((m_rows, LANES), F32),
        ],
        scratch_shapes=[pltpu.VMEM((tm, D_MODEL), BF16), pltpu.VMEM((tm, IN_TN), F32)],
        compiler_params=_cparams(("arbitrary", "arbitrary")),
        name="in_proj",
    )(x2, mod3, g_pre.reshape(1, D_MODEL), w_main, w_small, tabs)


def _attn_body(q_ref, qi_ref, k_ref, v_ref, kis_ref, wis_ref, o_ref, key_ref, bias_ref, *, tq, klen, n_keep):
    i = pl.program_id(1)
    nt = (((1,), (1,)), ((), ()))

    ki = kis_ref[:klen, :IDX_DIM].astype(BF16)
    w_t = wis_ref[...].T
    sc = jnp.zeros((klen, tq), F32)
    for h in range(IDX_HEADS):
        qh = qi_ref[:, h * IDX_DIM:(h + 1) * IDX_DIM]
        d = lax.dot_general(ki, qh, nt, preferred_element_type=F32)
        sc = sc + jnp.maximum(d, 0.0) * w_t[IDX_DIM + h:IDX_DIM + h + 1, :]

    bits = lax.bitcast_convert_type(sc, I32)
    key = jnp.where(bits < 0, bits ^ np.int32(0x7FFFFFFF), bits)
    row = lax.broadcasted_iota(I32, (klen, tq), 0)
    tpos = i * tq + lax.broadcasted_iota(I32, (klen, tq), 1)
    causal = row <= tpos
    key_ref[:klen, :] = jnp.where(causal, key, INT_MIN)

    def bit_step(it, tu):
        cand = tu | lax.shift_left(np.int32(1), 31 - it)
        cnt = jnp.sum((key_ref[:klen, :] >= (cand ^ INT_MIN)).astype(I32), axis=0, keepdims=True)
        return jnp.where(cnt >= n_keep, cand, tu)

    tu = lax.fori_loop(0, 32, bit_step, jnp.zeros((1, tq), I32))
    thr = tu ^ INT_MIN
    ge = key_ref[:klen, :] >= thr
    bias_ref[:klen, :] = jnp.where(ge & causal, 0.0, -jnp.inf).astype(F32)
    cnt_ge = jnp.sum(ge.astype(I32), axis=0, keepdims=True)
    tie_i = ((cnt_ge > n_keep) & (tu != 0)).astype(I32)

    @pl.when(jnp.max(tie_i) > 0)
    def _():
        kc = key_ref[:klen, :]
        row_b = lax.broadcasted_iota(I32, (klen, tq), 0)
        need = n_keep - jnp.sum((kc > thr).astype(I32), axis=0, keepdims=True)
        n_bits = (klen - 1).bit_length()

        def idx_step(it, p):
            cand = p | lax.shift_left(np.int32(1), (n_bits - 1) - it)
            below = jnp.sum(((key_ref[:klen, :] == thr) & (row_b < cand)).astype(I32), axis=0, keepdims=True)
            return jnp.where(below < need, cand, p)

        p = lax.fori_loop(0, n_bits, idx_step, jnp.zeros((1, tq), I32))
        p = jnp.where(tie_i > 0, p, klen)
        sel = (kc > thr) | ((kc == thr) & (row_b <= p))
        sel = sel & (row_b <= i * tq + lax.broadcasted_iota(I32, (klen, tq), 1))
        bias_ref[:klen, :] = jnp.where(sel, 0.0, -jnp.inf).astype(F32)

    bias_t = bias_ref[:klen, :].T
    grp = N_HEADS // N_KV_HEADS
    c_exp = HEAD_DIM ** -0.5 * np.log2(np.e)
    for g in range(N_KV_HEADS):
        qs = jnp.concatenate(
            [q_ref[:, (g * grp + r) * HEAD_DIM:(g * grp + r + 1) * HEAD_DIM] for r in range(grp)], axis=0
        )
        kg = k_ref[:klen, g * HEAD_DIM:(g + 1) * HEAD_DIM]
        vg = v_ref[:klen, g * HEAD_DIM:(g + 1) * HEAD_DIM]
        logits = lax.dot_general(qs, kg, nt, preferred_element_type=F32)
        logits = logits.reshape(grp, tq, klen) + bias_t[None]
        mx = jnp.max(logits, axis=-1, keepdims=True)
        p = jnp.exp2((logits - mx) * c_exp)
        den = jnp.sum(p, axis=-1, keepdims=True)
        o = jnp.dot(p.astype(BF16).reshape(grp * tq, klen), vg, preferred_element_type=F32)
        o = o.reshape(grp, tq, HEAD_DIM) / den
        for r in range(grp):
            hh = g * grp + r
            o_ref[:, hh * HEAD_DIM:(hh + 1) * HEAD_DIM] = o[r].astype(BF16)


def _attn_kernel(*refs, tq, kblk, seq, n_keep):
    i = pl.program_id(1)
    need = ((i + 1) * tq + kblk - 1) // kblk
    for nb in range(1, seq // kblk + 1):
        pl.when(need == nb)(functools.partial(_attn_body, *refs, tq=tq, klen=nb * kblk, n_keep=n_keep))


def _attn(proj, small, bsz, seq):
    tq = min(ATTN_TQ, seq)
    kblk = min(ATTN_KBLK, seq)
    assert seq % tq == 0 and seq % kblk == 0 and kblk % tq == 0
    nq = seq // tq
    n_keep = min(TOPK_MAX, seq // 4)
    kern = functools.partial(_attn_kernel, tq=tq, kblk=kblk, seq=seq, n_keep=n_keep)
    return pl.pallas_call(
        kern,
        grid=(bsz, nq),
        in_specs=[
            pl.BlockSpec((tq, ATTN_WIDTH), lambda b, i: (b * nq + i, Q_OFF // ATTN_WIDTH)),
            pl.BlockSpec((tq, IDX_HEADS * IDX_DIM), lambda b, i: (b * nq + i, QI_OFF // (IDX_HEADS * IDX_DIM))),
            pl.BlockSpec((seq, KV_WIDTH), lambda b, i: (b, K_OFF // KV_WIDTH)),
            pl.BlockSpec((seq, KV_WIDTH), lambda b, i: (b, V_OFF // KV_WIDTH)),
            pl.BlockSpec((seq, LANES), lambda b, i: (b, 0)),
            pl.BlockSpec((tq, LANES), lambda b, i: (b * nq + i, 0)),
        ],
        out_specs=pl.BlockSpec((tq, ATTN_WIDTH), lambda b, i: (b * nq + i, 0)),
        out_shape=jax.ShapeDtypeStruct((bsz * seq, ATTN_WIDTH), BF16),
        scratch_shapes=[pltpu.VMEM((seq, tq), I32), pltpu.VMEM((seq, tq), F32)],
        compiler_params=_cparams(("arbitrary", "arbitrary")),
        name="attn",
    )(proj, proj, proj, proj, small, small)


def _merge_kernel(
    u_ref, halo_ref, ya_ref, gl_ref, x_ref, mod_ref, wgrp_ref, pscale_ref, wup_p_ref, wup_a_ref, wout_ref,
    gpost_ref, gffn_ref, wr_ref, br_ref,
    x1_ref, hp_ref, route_ref, cnt_ref,
    carry_ref, *, seq,
):
    tm = MERGE_TM
    i = pl.program_id(0)
    pos0 = (i * tm) % seq

    @pl.when(i == 0)
    def _():
        carry_ref[...] = jnp.zeros_like(carry_ref)

    halo_rows = halo_ref.shape[0]
    halo = jnp.where(pos0 == 0, 0.0, halo_ref[...].astype(F32))
    ext = jnp.concatenate([halo, u_ref[...].astype(F32)], axis=0)
    pos = pos0 + lax.broadcasted_iota(I32, (tm, 1), 0)
    ys = []
    for g, win in enumerate(POOL_WINDOWS):
        sl = slice(g * POOL_GROUP_DIM, (g + 1) * POOL_GROUP_DIM)
        e = ext[:, sl]
        acc = e
        span = 1
        while span < win:
            acc = acc + pltpu.roll(acc, span, 0)
            span *= 2
        cnt = jnp.minimum(pos + 1, win).astype(F32)
        mixed = (acc[halo_rows:] / cnt - e[halo_rows:]).astype(BF16)
        ys.append(jnp.dot(mixed, wgrp_ref[g], preferred_element_type=F32))
    y_pool = (jnp.concatenate(ys, axis=1) * pscale_ref[...]).astype(BF16)

    gates = jax.nn.sigmoid(gl_ref[...].astype(F32))
    up_p = jnp.dot(y_pool, wup_p_ref[...], preferred_element_type=F32)
    up_a = jnp.dot(ya_ref[...], wup_a_ref[...], preferred_element_type=F32)
    merged = (gates[:, :D_MODEL] * up_p + gates[:, D_MODEL:] * up_a).astype(BF16)
    y = jnp.dot(merged, wout_ref[...], preferred_element_type=F32)
    m = mod_ref[0]
    x1 = x_ref[...] + m[2:3] * _rms(y, gpost_ref[...])
    x1_ref[...] = x1

    h2 = (_rms(x1, gffn_ref[...]) * (1.0 + m[4:5]) + m[3:4]).astype(BF16)
    hbits = lax.bitcast_convert_type(h2.astype(F32), U32)
    half = D_MODEL // 2
    hp_ref[...] = hbits[:, half:] | lax.shift_right_logical(hbits[:, :half], np.uint32(16))

    lane = lax.broadcasted_iota(I32, (tm, LANES), 1)
    logits = jnp.dot(h2, wr_ref[...], preferred_element_type=F32) + br_ref[...]
    work = jnp.where(lane < N_EXPERTS, logits, -jnp.inf)
    vals, hots = [], []
    for _ in range(TOP_K):
        mx = jnp.max(work, axis=-1, keepdims=True)
        first = jnp.min(jnp.where(work == mx, lane, LANES), axis=-1, keepdims=True)
        hot = lane == first
        vals.append(mx)
        hots.append(hot)
        work = jnp.where(hot, -jnp.inf, work)
    ex = [jnp.exp(v - vals[0]) for v in vals]
    den = ex[0] + ex[1] + ex[2] + ex[3]

    onehot = jnp.zeros((tm, LANES), F32)
    for hot in hots:
        onehot = onehot + hot.astype(F32)
    r_i = lax.broadcasted_iota(I32, (tm, tm), 0)
    c_i = lax.broadcasted_iota(I32, (tm, tm), 1)
    tril = (c_i < r_i).astype(BF16)
    before = jnp.dot(tril, onehot.astype(BF16), preferred_element_type=F32) + carry_ref[0:1, :]
    out = jnp.zeros((tm, LANES), F32)
    lane_f = lane.astype(F32)
    for k in range(TOP_K):
        out = jnp.where(lane == k, ex[k] / den, out)
        idx_k = jnp.sum(jnp.where(hots[k], lane_f, 0.0), axis=-1, keepdims=True)
        out = jnp.where(lane == TOP_K + k, idx_k, out)
        rank_k = jnp.sum(jnp.where(hots[k], before, 0.0), axis=-1, keepdims=True)
        out = jnp.where(lane == 2 * TOP_K + k, rank_k, out)
    route_ref[...] = out
    total = carry_ref[0:1, :] + jnp.sum(onehot, axis=0, keepdims=True)
    carry_ref[...] = jnp.broadcast_to(total, carry_ref.shape)
    cnt_ref[...] = jnp.broadcast_to(total, cnt_ref.shape)


def _merge(proj, y_attn, x2, mod3, w_pool_grp, pool_scale, w_up_pool, w_up_attn, w_out, g_post_mix, g_pre_ffn,
           w_router, b_router, seq):
    m_rows = x2.shape[0]
    tm = MERGE_TM
    halo = 16
    assert seq % tm == 0 and max(POOL_WINDOWS) <= halo
    per_seq = seq // tm
    wr = jnp.concatenate([w_router, jnp.zeros((D_MODEL, LANES - N_EXPERTS), F32)], axis=1).astype(BF16)
    br = jnp.concatenate([b_router, jnp.zeros((LANES - N_EXPERTS,), F32)]).reshape(1, LANES)
    const = lambda shape: pl.BlockSpec(shape, lambda i: (0,) * len(shape), pipeline_mode=pl.Buffered(1))
    kern = functools.partial(_merge_kernel, seq=seq)
    return pl.pallas_call(
        kern,
        grid=(m_rows // tm,),
        in_specs=[
            pl.BlockSpec((tm, POOL_WIDTH), lambda i: (i, U_OFF // POOL_WIDTH)),
            pl.BlockSpec((halo, POOL_WIDTH), lambda i: (jnp.maximum(i * (tm // halo) - 1, 0), U_OFF // POOL_WIDTH)),
            pl.BlockSpec((tm, ATTN_WIDTH), lambda i: (i, 0)),
            pl.BlockSpec((tm, N_BRANCHES * D_MODEL), lambda i: (i, 0)),
            pl.BlockSpec((tm, D_MODEL), lambda i: (i, 0)),
            pl.BlockSpec((1, N_MOD, D_MODEL), lambda i: (i // per_seq, 0, 0)),
            const((POOL_GROUPS, POOL_GROUP_DIM, POOL_GROUP_DIM)),
            const((1, POOL_WIDTH)),
            const((POOL_WIDTH, D_MODEL)),
            const((ATTN_WIDTH, D_MODEL)),
            const((D_MODEL, D_MODEL)),
            const((1, D_MODEL)),
            const((1, D_MODEL)),
            const((D_MODEL, LANES)),
            const((1, LANES)),
        ],
        out_specs=[
            pl.BlockSpec((tm, D_MODEL), lambda i: (i, 0)),
            pl.BlockSpec((tm, D_MODEL // 2), lambda i: (i, 0)),
            pl.BlockSpec((tm, LANES), lambda i: (i, 0)),
            pl.BlockSpec((8, LANES), lambda i: (0, 0)),
        ],
        out_shape=[
            jax.ShapeDtypeStruct((m_rows, D_MODEL), F32),
            jax.ShapeDtypeStruct((m_rows, D_MODEL // 2), U32),
            jax.ShapeDtypeStruct((m_rows, LANES), F32),
            jax.ShapeDtypeStruct((8, LANES), F32),
        ],
        scratch_shapes=[pltpu.VMEM((8, LANES), F32)],
        compiler_params=_cparams(("arbitrary",)),
        name="merge",
    )(
        proj, proj, y_attn, proj, x2, mod3,
        w_pool_grp.astype(BF16), pool_scale.reshape(1, POOL_WIDTH), w_up_pool.astype(BF16),
        w_up_attn.astype(BF16), w_out.astype(BF16), g_post_mix.reshape(1, D_MODEL), g_pre_ffn.reshape(1, D_MODEL),
        wr, br,
    )


def _dispatch_kernel(zf_ref, pos_ref, h_ref, xs_ref, zbuf, sem, zsem):
    tm = h_ref.shape[0]
    bm = zbuf.shape[0]

    @pl.when(pl.program_id(0) == 0)
    def _():
        zbuf[...] = jnp.zeros_like(zbuf)

        def zero_copy(b):
            return pltpu.make_async_copy(zbuf, xs_ref.at[pl.ds(b * bm, bm), :], zsem)

        def z_issue(b, carry):
            pl.when(zf_ref[b] != 0)(lambda: zero_copy(b).start())
            return carry

        def z_drain(b, carry):
            pl.when(zf_ref[b] != 0)(lambda: zero_copy(b).wait())
            return carry

        lax.fori_loop(0, zf_ref.shape[0], z_issue, 0)
        lax.fori_loop(0, zf_ref.shape[0], z_drain, 0)

    def row_copy(t, k):
        return pltpu.make_async_copy(
            h_ref.at[pl.ds(t, 1), :], xs_ref.at[pl.ds(pos_ref[t * TOP_K + k], 1), :], sem
        )

    def issue(t, carry):
        for k in range(TOP_K):
            row_copy(t, k).start()
        return carry

    def drain(t, carry):
        for k in range(TOP_K):
            row_copy(t, k).wait()
        return carry

    lax.fori_loop(0, tm, issue, 0)
    lax.fori_loop(0, tm, drain, 0)


def _dispatch(zero_flag, pos_flat, hp, n_rows):
    m_rows, width = hp.shape
    tm = DISPATCH_TM
    grid_spec = pltpu.PrefetchScalarGridSpec(
        num_scalar_prefetch=1,
        grid=(m_rows // tm,),
        in_specs=[
            pl.BlockSpec((tm * TOP_K,), lambda i, zf: (i,), memory_space=pltpu.SMEM),
            pl.BlockSpec((tm, width), lambda i, zf: (i, 0)),
        ],
        out_specs=pl.BlockSpec(memory_space=pl.ANY),
        scratch_shapes=[pltpu.VMEM((MOE_BM, width), U32), pltpu.SemaphoreType.DMA(()), pltpu.SemaphoreType.DMA(())],
    )
    return pl.pallas_call(
        _dispatch_kernel,
        grid_spec=grid_spec,
        out_shape=jax.ShapeDtypeStruct((n_rows, width), U32),
        compiler_params=_cparams(("arbitrary",)),
        name="dispatch",
    )(zero_flag, pos_flat, hp)


def _new_expert(be_ref, b):
    return (b == 0) | (be_ref[b] != be_ref[jnp.maximum(b - 1, 0)])


def _ffn1_kernel(be_ref, nu_ref, x_ref, wg_ref, wl_ref, bg_ref, bl_ref, o_ref, wg_s, wl_s):
    b = pl.program_id(1)

    @pl.when(_new_expert(be_ref, b))
    def _():
        wg_s[...] = wg_ref[0].astype(BF16)
        wl_s[...] = wl_ref[0].astype(BF16)

    @pl.when(b < nu_ref[0])
    def _():
        xp = x_ref[...]
        lo = lax.bitcast_convert_type(lax.shift_left(xp, np.uint32(16)), F32).astype(BF16)
        hi = lax.bitcast_convert_type(xp & np.uint32(0xFFFF0000), F32).astype(BF16)
        half = D_MODEL // 2

        def proj(w_s, b_ref):
            return (
                jnp.dot(lo, w_s[:half, :], preferred_element_type=F32)
                + jnp.dot(hi, w_s[half:, :], preferred_element_type=F32)
                + b_ref[0]
            )

        glu = jnp.minimum(proj(wg_s, bg_ref), SWIGLU_LIMIT)
        lin = jnp.clip(proj(wl_s, bl_ref), -SWIGLU_LIMIT, SWIGLU_LIMIT)
        o_ref[...] = (glu * jax.nn.sigmoid(SWIGLU_ALPHA * glu) * (lin + 1.0)).astype(BF16)

    @pl.when(b >= nu_ref[0])
    def _():
        o_ref[...] = jnp.zeros_like(o_ref)


def _ffn1(blk_expert, n_used, xs, w1, b1):
    n_rows = xs.shape[0]
    bm, tn = MOE_BM, FFN1_TN
    n_blocks = n_rows // bm
    nj = D_FF // tn
    b1r = b1.reshape(N_EXPERTS, 1, 2 * D_FF)
    row = lambda j, b, be, nu: jnp.minimum(b, nu[0] - 1)
    grid_spec = pltpu.PrefetchScalarGridSpec(
        num_scalar_prefetch=2,
        grid=(nj, n_blocks),
        in_specs=[
            pl.BlockSpec((bm, D_MODEL // 2), lambda j, b, be, nu: (row(j, b, be, nu), 0)),
            pl.BlockSpec((1, D_MODEL, tn), lambda j, b, be, nu: (be[b], 0, j)),
            pl.BlockSpec((1, D_MODEL, tn), lambda j, b, be, nu: (be[b], 0, nj + j)),
            pl.BlockSpec((1, 1, tn), lambda j, b, be, nu: (be[b], 0, j)),
            pl.BlockSpec((1, 1, tn), lambda j, b, be, nu: (be[b], 0, nj + j)),
        ],
        out_specs=pl.BlockSpec((bm, tn), lambda j, b, be, nu: (b, j)),
        scratch_shapes=[pltpu.VMEM((D_MODEL, tn), BF16), pltpu.VMEM((D_MODEL, tn), BF16)],
    )
    return pl.pallas_call(
        _ffn1_kernel,
        grid_spec=grid_spec,
        out_shape=jax.ShapeDtypeStruct((n_rows, D_FF), BF16),
        compiler_params=_cparams(("arbitrary", "arbitrary")),
        name="ffn1",
    )(blk_expert, n_used, xs, w1, w1, b1r, b1r)


def _ffn2_kernel(be_ref, nu_ref, a_ref, w_ref, b_ref, o_ref, w_s):
    b = pl.program_id(1)

    @pl.when(_new_expert(be_ref, b))
    def _():
        w_s[...] = w_ref[0].astype(BF16)

    @pl.when(b < nu_ref[0])
    def _():
        o_ref[...] = jnp.dot(a_ref[...], w_s[...], preferred_element_type=F32) + b_ref[0]

    @pl.when(b >= nu_ref[0])
    def _():
        o_ref[...] = jnp.zeros_like(o_ref)


def _ffn2(blk_expert, n_used, act, w2, b2):
    n_rows = act.shape[0]
    bm, tn = MOE_BM, FFN2_TN
    n_blocks = n_rows // bm
    nj = D_MODEL // tn
    b2r = b2.reshape(N_EXPERTS, 1, D_MODEL)
    row = lambda j, b, be, nu: jnp.minimum(b, nu[0] - 1)
    grid_spec = pltpu.PrefetchScalarGridSpec(
        num_scalar_prefetch=2,
        grid=(nj, n_blocks),
        in_specs=[
            pl.BlockSpec((bm, D_FF), lambda j, b, be, nu: (row(j, b, be, nu), 0)),
            pl.BlockSpec((1, D_FF, tn), lambda j, b, be, nu: (be[b], 0, j)),
            pl.BlockSpec((1, 1, tn), lambda j, b, be, nu: (be[b], 0, j)),
        ],
        out_specs=pl.BlockSpec((bm, tn), lambda j, b, be, nu: (b, j)),
        scratch_shapes=[pltpu.VMEM((D_FF, tn), BF16)],
    )
    return pl.pallas_call(
        _ffn2_kernel,
        grid_spec=grid_spec,
        out_shape=jax.ShapeDtypeStruct((n_rows, D_MODEL), F32),
        compiler_params=_cparams(("arbitrary", "arbitrary")),
        name="ffn2",
    )(blk_expert, n_used, act, w2, b2r)


def _combine_kernel(pos_ref, ys_ref, route_ref, x1_ref, mod_ref, g_ref, o_ref, buf, sem):
    tm = x1_ref.shape[0]

    def row_copy(t, k):
        return pltpu.make_async_copy(
            ys_ref.at[pl.ds(pos_ref[t * TOP_K + k], 1), :], buf.at[k, pl.ds(t, 1), :], sem
        )

    def issue(t, carry):
        for k in range(TOP_K):
            row_copy(t, k).start()
        return carry

    def drain(t, carry):
        for k in range(TOP_K):
            row_copy(t, k).wait()
        return carry

    lax.fori_loop(0, tm, issue, 0)
    lax.fori_loop(0, tm, drain, 0)
    route = route_ref[...]
    y = route[:, 0:1] * buf[0]
    for k in range(1, TOP_K):
        y = y + route[:, k:k + 1] * buf[k]
    m = mod_ref[0]
    o_ref[...] = x1_ref[...] + m[5:6] * _rms(y, g_ref[...])


def _combine(pos_flat, ys, route, x1, mod3, g_post_ffn, seq):
    m_rows = x1.shape[0]
    tm = COMBINE_TM
    per_seq = seq // tm
    return pl.pallas_call(
        _combine_kernel,
        grid=(m_rows // tm,),
        in_specs=[
            pl.BlockSpec((tm * TOP_K,), lambda i: (i,), memory_space=pltpu.SMEM),
            pl.BlockSpec(memory_space=pl.ANY),
            pl.BlockSpec((tm, LANES), lambda i: (i, 0)),
            pl.BlockSpec((tm, D_MODEL), lambda i: (i, 0)),
            pl.BlockSpec((1, N_MOD, D_MODEL), lambda i: (i // per_seq, 0, 0)),
            pl.BlockSpec((1, D_MODEL), lambda i: (0, 0)),
        ],
        out_specs=pl.BlockSpec((tm, D_MODEL), lambda i: (i, 0)),
        out_shape=jax.ShapeDtypeStruct((m_rows, D_MODEL), F32),
        scratch_shapes=[pltpu.VMEM((TOP_K, tm, D_MODEL), F32), pltpu.SemaphoreType.DMA(())],
        compiler_params=_cparams(("arbitrary",)),
        name="combine",
    )(pos_flat, ys, route, x1, mod3, g_post_ffn.reshape(1, D_MODEL))


def _layer(x2, mod3, bsz, seq, g_pre_mix, g_post_mix, w_in, w_pool_grp, pool_scale, w_up_pool, w_up_attn, w_out,
           g_pre_ffn, g_post_ffn, w_router, b_router, w1, b1, w2, b2):
    m_rows = x2.shape[0]
    proj, small = _in_proj(x2, mod3, g_pre_mix, w_in, seq)
    y_attn = _attn(proj, small, bsz, seq)
    x1, hp, route, cnt = _merge(proj, y_attn, x2, mod3, w_pool_grp, pool_scale, w_up_pool, w_up_attn, w_out,
                                g_post_mix, g_pre_ffn, w_router, b_router, seq)

    bm = MOE_BM
    n_slots = m_rows * TOP_K
    n_blocks = -(-n_slots // bm) + N_EXPERTS
    idx = route[:, TOP_K:2 * TOP_K].astype(I32)
    rank = route[:, 2 * TOP_K:3 * TOP_K].astype(I32)
    counts = cnt[0, :N_EXPERTS].astype(I32)
    padded = (counts + bm - 1) // bm * bm
    pad_end = jnp.cumsum(padded)
    pad_start = pad_end - padded
    pos_flat = (pad_start[idx] + rank).reshape(-1)
    n_used = (pad_end[-1] // bm).astype(I32).reshape(1)
    blk_ids = jnp.arange(n_blocks, dtype=I32)
    blk = jnp.minimum(blk_ids, n_used[0] - 1) * bm
    blk_expert = jnp.minimum(jnp.sum((blk[:, None] >= pad_end[None, :]).astype(I32), axis=1), N_EXPERTS - 1)
    last_of_expert = jnp.any((blk_ids[:, None] + 1) * bm == pad_end[None, :], axis=1)
    zero_flag = (last_of_expert | (blk_ids >= n_used[0])).astype(I32)

    xs = _dispatch(zero_flag, pos_flat, hp, n_blocks * bm)
    act = _ffn1(blk_expert, n_used, xs, w1, b1)
    ys = _ffn2(blk_expert, n_used, act, w2, b2)
    return _combine(pos_flat, ys, route, x1, mod3, g_post_ffn, seq)


def kernel(x, c, w_ada, b_ada, g_pre_mix, g_post_mix, w_in, w_pool_grp, pool_scale, w_up_pool, w_up_attn, w_out,
           g_pre_ffn, g_post_ffn, w_router, b_router, w1, b1, w2, b2):
    bsz, seq, d = x.shape
    assert d == D_MODEL
    depth = w_ada.shape[0]
    x2 = x.reshape(bsz * seq, d)
    for layer in range(depth):
        mod3 = _ada(c, w_ada[layer], b_ada[layer]).reshape(bsz, N_MOD, D_MODEL)
        x2 = _layer(x2, mod3, bsz, seq, g_pre_mix[layer], g_post_mix[layer], w_in[layer], w_pool_grp[layer],
                    pool_scale[layer], w_up_pool[layer], w_up_attn[layer], w_out[layer], g_pre_ffn[layer],
                    g_post_ffn[layer], w_router[layer], b_router[layer], w1[layer], b1[layer], w2[layer], b2[layer])
    return x2.reshape(bsz, seq, d)
```

```python
import functools

import jax
import jax.numpy as jnp
import numpy as np
from jax import lax
from jax.experimental import pallas as pl
from jax.experimental.pallas import tpu as pltpu

F32 = jnp.float32
BF16 = jnp.bfloat16
I32 = jnp.int32
U32 = jnp.uint32

D_MODEL = 2048
POOL_WINDOWS = (2, 4, 8, 16)
POOL_GROUPS = 4
POOL_GROUP_DIM = D_MODEL // 8
POOL_WIDTH = POOL_GROUPS * POOL_GROUP_DIM
N_HEADS = 16
N_KV_HEADS = 4
HEAD_DIM = 128
ATTN_WIDTH = N_HEADS * HEAD_DIM
KV_WIDTH = N_KV_HEADS * HEAD_DIM
ROT_DIM = HEAD_DIM // 4
IDX_HEADS = 16
IDX_DIM = 64
IDX_ROT_DIM = IDX_DIM // 4
TOPK_MAX = 256
ROPE_THETA = 500000.0
N_BRANCHES = 2
N_EXPERTS = 32
TOP_K = 4
D_FF = D_MODEL
SWIGLU_ALPHA = 1.702
SWIGLU_LIMIT = 7.0
N_MOD = 6
EPS = 1e-6

LANES = 128
INT_MIN = np.int32(-(2 ** 31))

GL_OFF = 0
Q_OFF = GL_OFF + N_BRANCHES * D_MODEL
U_OFF = Q_OFF + ATTN_WIDTH
QI_OFF = U_OFF + POOL_WIDTH
K_OFF = QI_OFF + IDX_HEADS * IDX_DIM
V_OFF = K_OFF + KV_WIDTH
PROJ_WIDTH = V_OFF + KV_WIDTH

IN_TN = 1024
ATTN_TQ = 256
ATTN_KC = 512
MASK_NEG = -1e30
MERGE_TM = 256
MOE_BM = 512
FFN1_TN = 512
FFN2_TN = 1024
DISPATCH_TM = 256
COMBINE_TM = 128
VMEM_LIMIT = 56 * 1024 * 1024


def _cparams(sem):
    return pltpu.CompilerParams(dimension_semantics=sem, vmem_limit_bytes=VMEM_LIMIT)


def _rms(x, g):
    return x * lax.rsqrt(jnp.mean(x * x, axis=-1, keepdims=True) + EPS) * g


def _ada_kernel(c_ref, w_ref, b_ref, o_ref):
    c = c_ref[...]
    ca = (c * jax.nn.sigmoid(c)).astype(BF16)
    o_ref[...] = jnp.dot(ca, w_ref[...].astype(BF16), preferred_element_type=F32) + b_ref[...]


def _ada(c, w_ada, b_ada):
    bsz = c.shape[0]
    n = w_ada.shape[1]
    tn = 1024
    return pl.pallas_call(
        _ada_kernel,
        grid=(n // tn,),
        in_specs=[
            pl.BlockSpec((bsz, D_MODEL), lambda j: (0, 0)),
            pl.BlockSpec((D_MODEL, tn), lambda j: (0, j)),
            pl.BlockSpec((1, tn), lambda j: (0, j)),
        ],
        out_specs=pl.BlockSpec((bsz, tn), lambda j: (0, j)),
        out_shape=jax.ShapeDtypeStruct((bsz, n), F32),
        compiler_params=_cparams(("arbitrary",)),
        name="ada",
    )(c, w_ada, b_ada.reshape(1, n))


def _rope_slice(xs, c, s1, s2, shift):
    return xs * c + pltpu.roll(xs, LANES - shift, 1) * s1 + pltpu.roll(xs, shift, 1) * s2


def _in_kernel(x_ref, mod_ref, g_ref, w_ref, ws_ref, tab_ref, o_ref, os_ref, h_ref, acc_ref):
    j = pl.program_id(1)
    half_a = ROT_DIM // 2
    half_i = IDX_ROT_DIM // 2

    @pl.when(j == 0)
    def _():
        m = mod_ref[0]
        h = _rms(x_ref[...], g_ref[...]) * (1.0 + m[1:2]) + m[0:1]
        hb = h.astype(BF16)
        h_ref[...] = hb
        small = jnp.dot(hb, ws_ref[...], preferred_element_type=F32)
        os_ref[...] = _rope_slice(small, tab_ref[6], tab_ref[7], tab_ref[8], half_i)

    acc_ref[...] = jnp.dot(h_ref[...], w_ref[...], preferred_element_type=F32)
    n_sl = IN_TN // LANES
    j_q0, j_q1 = Q_OFF // IN_TN, U_OFF // IN_TN
    j_qi = QI_OFF // IN_TN
    j_kv = K_OFF // IN_TN
    k_sl = KV_WIDTH // LANES

    @pl.when((j < j_q0) | (j == U_OFF // IN_TN))
    def _():
        o_ref[...] = acc_ref[...].astype(BF16)

    @pl.when((j >= j_q0) & (j < j_q1))
    def _():
        for s in range(n_sl):
            sl = slice(s * LANES, (s + 1) * LANES)
            o_ref[:, sl] = _rope_slice(acc_ref[:, sl], tab_ref[0], tab_ref[1], tab_ref[2], half_a).astype(BF16)

    @pl.when(j == j_qi)
    def _():
        for s in range(n_sl):
            sl = slice(s * LANES, (s + 1) * LANES)
            o_ref[:, sl] = _rope_slice(acc_ref[:, sl], tab_ref[3], tab_ref[4], tab_ref[5], half_i).astype(BF16)

    @pl.when(j == j_kv)
    def _():
        for s in range(k_sl):
            sl = slice(s * LANES, (s + 1) * LANES)
            o_ref[:, sl] = _rope_slice(acc_ref[:, sl], tab_ref[0], tab_ref[1], tab_ref[2], half_a).astype(BF16)
        o_ref[:, KV_WIDTH:] = acc_ref[:, KV_WIDTH:].astype(BF16)


def _rope_tables(seq):
    def tabs(rot_dim, period, n_rep):
        half = rot_dim // 2
        inv = np.float32(ROPE_THETA) ** (-np.arange(0, rot_dim, 2, dtype=np.float32) / np.float32(rot_dim))
        ang = np.arange(seq, dtype=np.float32)[:, None] * inv.astype(np.float32)[None, :]
        cos, sin = np.cos(ang).astype(np.float32), np.sin(ang).astype(np.float32)
        ones = np.ones((seq, period - 2 * half), np.float32)
        z_h = np.zeros((seq, half), np.float32)
        c = np.concatenate([cos, cos, ones], axis=1)
        s1 = np.concatenate([-sin, z_h, 0 * ones], axis=1)
        s2 = np.concatenate([z_h, sin, 0 * ones], axis=1)
        return [np.tile(t, (1, n_rep)) for t in (c, s1, s2)]

    ta = tabs(ROT_DIM, HEAD_DIM, LANES // HEAD_DIM)
    ti = tabs(IDX_ROT_DIM, IDX_DIM, LANES // IDX_DIM)
    ts = tabs(IDX_ROT_DIM, IDX_DIM, 1)
    wi_scale = np.float32(IDX_HEADS ** -0.5 * IDX_DIM ** -0.5)
    pad = LANES - IDX_DIM
    c_tail = np.where(np.arange(pad) < IDX_HEADS, wi_scale, np.float32(0.0)).astype(np.float32)
    ts = [
        np.concatenate([ts[0], np.broadcast_to(c_tail, (seq, pad))], axis=1),
        np.concatenate([ts[1], np.zeros((seq, pad), np.float32)], axis=1),
        np.concatenate([ts[2], np.zeros((seq, pad), np.float32)], axis=1),
    ]
    return jnp.asarray(np.stack(ta + ti + ts, axis=0))


def _in_proj(x2, mod3, g_pre, w_in, seq):
    m_rows = x2.shape[0]
    tm = min(1024, seq)
    assert seq % tm == 0 and m_rows % tm == 0
    per_seq = seq // tm
    offs = np.cumsum((POOL_WIDTH, ATTN_WIDTH, KV_WIDTH, KV_WIDTH, IDX_HEADS * IDX_DIM, IDX_DIM, IDX_HEADS))
    w_u, w_q, w_k, w_v, w_qi, w_ki, w_wi, w_gl = jnp.split(w_in, [int(o) for o in offs], axis=1)
    w_main = jnp.concatenate([w_gl, w_q, w_u, w_qi, w_k, w_v], axis=1).astype(BF16)
    w_small = jnp.concatenate(
        [w_ki, w_wi, jnp.zeros((D_MODEL, LANES - IDX_DIM - IDX_HEADS), F32)], axis=1
    ).astype(BF16)
    tabs = _rope_tables(seq)
    n_j = PROJ_WIDTH // IN_TN
    return pl.pallas_call(
        _in_kernel,
        grid=(m_rows // tm, n_j),
        in_specs=[
            pl.BlockSpec((tm, D_MODEL), lambda i, j: (i, 0)),
            pl.BlockSpec((1, N_MOD, D_MODEL), lambda i, j: (i // per_seq, 0, 0)),
            pl.BlockSpec((1, D_MODEL), lambda i, j: (0, 0)),
            pl.BlockSpec((D_MODEL, IN_TN), lambda i, j: (0, j)),
            pl.BlockSpec((D_MODEL, LANES), lambda i, j: (0, 0)),
            pl.BlockSpec((9, tm, LANES), lambda i, j: (0, i % per_seq, 0)),
        ],
        out_specs=[
            pl.BlockSpec((tm, IN_TN), lambda i, j: (i, j)),
            pl.BlockSpec((tm, LANES), lambda i, j: (i, 0)),
        ],
        out_shape=[
            jax.ShapeDtypeStruct((m_rows, PROJ_WIDTH), BF16),
            jax.ShapeDtypeStruct((m_rows, LANES), F32),
        ],
        scratch_shapes=[pltpu.VMEM((tm, D_MODEL), BF16), pltpu.VMEM((tm, IN_TN), F32)],
        compiler_params=_cparams(("arbitrary", "arbitrary")),
        name="in_proj",
    )(x2, mod3, g_pre.reshape(1, D_MODEL), w_main, w_small, tabs)


def _attn_kernel(q_ref, qi_ref, k_ref, v_ref, kis_ref, wis_ref, o_ref,
                 key_ref, plim_ref, qs_ref, m_ref, l_ref, acc_ref, *, tq, kc, seq, n_keep):
    i = pl.program_id(1)
    nk = ((i + 1) * tq + kc - 1) // kc
    nt = (((1,), (1,)), ((), ()))
    grp = N_HEADS // N_KV_HEADS
    row_io = lax.broadcasted_iota(I32, (kc, tq), 0)
    tpos = i * tq + lax.broadcasted_iota(I32, (kc, tq), 1)
    w_t = wis_ref[...].T

    def idx_chunk(c, carry):
        r0 = pl.multiple_of(c * kc, kc)
        ki = kis_ref[pl.ds(r0, kc), :IDX_DIM].astype(BF16)
        sc = jnp.zeros((kc, tq), F32)
        for h in range(IDX_HEADS):
            qh = qi_ref[:, h * IDX_DIM:(h + 1) * IDX_DIM]
            d = lax.dot_general(ki, qh, nt, preferred_element_type=F32)
            sc = sc + jnp.maximum(d, 0.0) * w_t[IDX_DIM + h:IDX_DIM + h + 1, :]
        bits = lax.bitcast_convert_type(sc, I32)
        key = jnp.where(bits < 0, bits ^ np.int32(0x7FFFFFFF), bits)
        key_ref[pl.ds(r0, kc), :] = jnp.where(r0 + row_io <= tpos, key, INT_MIN)
        return carry

    lax.fori_loop(0, nk, idx_chunk, 0)

    def count(pred):
        def body(c, acc):
            r0 = pl.multiple_of(c * kc, kc)
            hit = pred(key_ref[pl.ds(r0, kc), :], r0)
            return acc + jnp.sum(hit.astype(I32), axis=0, keepdims=True)

        return lax.fori_loop(0, nk, body, jnp.zeros((1, tq), I32))

    def bit_step(it, tu):
        cand = tu | lax.shift_left(np.int32(1), 31 - it)
        cnt = count(lambda kch, r0: kch >= (cand ^ INT_MIN))
        return jnp.where(cnt >= n_keep, cand, tu)

    tu = lax.fori_loop(0, 32, bit_step, jnp.zeros((1, tq), I32))
    thr = tu ^ INT_MIN
    cnt_gt = count(lambda kch, r0: kch > thr)
    cnt_ge = count(lambda kch, r0: kch >= thr)
    tie_i = ((cnt_ge > n_keep) & (tu != 0)).astype(I32)
    need = n_keep - cnt_gt

    plim_ref[...] = jnp.full(plim_ref.shape, seq, I32)

    @pl.when(jnp.max(tie_i) > 0)
    def _():
        n_bits = (seq - 1).bit_length()

        def idx_step(it, p):
            cand = p | lax.shift_left(np.int32(1), (n_bits - 1) - it)
            below = count(lambda kch, r0: (kch == thr) & (r0 + row_io < cand))
            return jnp.where(below < need, cand, p)

        p = lax.fori_loop(0, n_bits, idx_step, jnp.zeros((1, tq), I32))
        plim_ref[...] = jnp.broadcast_to(jnp.where(tie_i > 0, p, seq), plim_ref.shape)

    for g in range(N_KV_HEADS):
        for r in range(grp):
            hh = g * grp + r
            qs_ref[g, r * tq:(r + 1) * tq, :] = q_ref[:, hh * HEAD_DIM:(hh + 1) * HEAD_DIM]
    m_ref[...] = jnp.full(m_ref.shape, MASK_NEG, F32)
    l_ref[...] = jnp.zeros_like(l_ref)
    acc_ref[...] = jnp.zeros_like(acc_ref)
    c_exp = HEAD_DIM ** -0.5 * np.log2(np.e)

    def att_chunk(c, carry):
        r0 = pl.multiple_of(c * kc, kc)
        kch = key_ref[pl.ds(r0, kc), :]
        row = r0 + row_io
        sel = ((kch > thr) | ((kch == thr) & (row <= plim_ref[0:1, :]))) & (row <= tpos)
        bias_t = jnp.where(sel, 0.0, MASK_NEG).astype(F32).T
        for g in range(N_KV_HEADS):
            kg = k_ref[pl.ds(r0, kc), g * HEAD_DIM:(g + 1) * HEAD_DIM]
            vg = v_ref[pl.ds(r0, kc), g * HEAD_DIM:(g + 1) * HEAD_DIM]
            s = lax.dot_general(qs_ref[g], kg, nt, preferred_element_type=F32)
            s = (s.reshape(grp, tq, kc) + bias_t[None]).reshape(grp * tq, kc)
            m_old = m_ref[g]
            m_new = jnp.maximum(m_old, jnp.max(s, axis=-1, keepdims=True))
            alpha = jnp.exp2((m_old - m_new) * c_exp)
            p = jnp.exp2((s - m_new) * c_exp)
            l_ref[g] = alpha * l_ref[g] + jnp.sum(p, axis=-1, keepdims=True)
            acc_ref[g] = alpha * acc_ref[g] + jnp.dot(p.astype(BF16), vg, preferred_element_type=F32)
            m_ref[g] = m_new
        return carry

    lax.fori_loop(0, nk, att_chunk, 0)
    for g in range(N_KV_HEADS):
        o = acc_ref[g] / l_ref[g]
        for r in range(grp):
            hh = g * grp + r
            o_ref[:, hh * HEAD_DIM:(hh + 1) * HEAD_DIM] = o[r * tq:(r + 1) * tq, :].astype(BF16)


def _attn(proj, small, bsz, seq):
    tq = min(ATTN_TQ, seq)
    kc = min(ATTN_KC, seq)
    assert seq % tq == 0 and seq % kc == 0 and kc % tq == 0
    nq = seq // tq
    n_keep = min(TOPK_MAX, seq // 4)
    grp = N_HEADS // N_KV_HEADS
    kern = functools.partial(_attn_kernel, tq=tq, kc=kc, seq=seq, n_keep=n_keep)
    return pl.pallas_call(
        kern,
        grid=(bsz, nq),
        in_specs=[
            pl.BlockSpec((tq, ATTN_WIDTH), lambda b, i: (b * nq + i, Q_OFF // ATTN_WIDTH)),
            pl.BlockSpec((tq, IDX_HEADS * IDX_DIM), lambda b, i: (b * nq + i, QI_OFF // (IDX_HEADS * IDX_DIM))),
            pl.BlockSpec((seq, KV_WIDTH), lambda b, i: (b, K_OFF // KV_WIDTH)),
            pl.BlockSpec((seq, KV_WIDTH), lambda b, i: (b, V_OFF // KV_WIDTH)),
            pl.BlockSpec((seq, LANES), lambda b, i: (b, 0)),
            pl.BlockSpec((tq, LANES), lambda b, i: (b * nq + i, 0)),
        ],
        out_specs=pl.BlockSpec((tq, ATTN_WIDTH), lambda b, i: (b * nq + i, 0)),
        out_shape=jax.ShapeDtypeStruct((bsz * seq, ATTN_WIDTH), BF16),
        scratch_shapes=[
            pltpu.VMEM((seq, tq), I32),
            pltpu.VMEM((8, tq), I32),
            pltpu.VMEM((N_KV_HEADS, grp * tq, HEAD_DIM), BF16),
            pltpu.VMEM((N_KV_HEADS, grp * tq, 1), F32),
            pltpu.VMEM((N_KV_HEADS, grp * tq, 1), F32),
            pltpu.VMEM((N_KV_HEADS, grp * tq, HEAD_DIM), F32),
        ],
        compiler_params=_cparams(("arbitrary", "arbitrary")),
        name="attn",
    )(proj, proj, proj, proj, small, small)


def _merge_kernel(
    u_ref, halo_ref, ya_ref, gl_ref, x_ref, mod_ref, wgrp_ref, pscale_ref, wup_p_ref, wup_a_ref, wout_ref,
    gpost_ref, gffn_ref, wr_ref, br_ref,
    x1_ref, hp_ref, route_ref, cnt_ref,
    carry_ref, *, seq,
):
    tm = MERGE_TM
    i = pl.program_id(0)
    pos0 = (i * tm) % seq

    @pl.when(i == 0)
    def _():
        carry_ref[...] = jnp.zeros_like(carry_ref)

    halo_rows = halo_ref.shape[0]
    halo = jnp.where(pos0 == 0, 0.0, halo_ref[...].astype(F32))
    ext = jnp.concatenate([halo, u_ref[...].astype(F32)], axis=0)
    pos = pos0 + lax.broadcasted_iota(I32, (tm, 1), 0)
    ys = []
    for g, win in enumerate(POOL_WINDOWS):
        sl = slice(g * POOL_GROUP_DIM, (g + 1) * POOL_GROUP_DIM)
        e = ext[:, sl]
        acc = e
        span = 1
        while span < win:
            acc = acc + pltpu.roll(acc, span, 0)
            span *= 2
        cnt = jnp.minimum(pos + 1, win).astype(F32)
        mixed = (acc[halo_rows:] / cnt - e[halo_rows:]).astype(BF16)
        ys.append(jnp.dot(mixed, wgrp_ref[g], preferred_element_type=F32))
    y_pool = (jnp.concatenate(ys, axis=1) * pscale_ref[...]).astype(BF16)

    gates = jax.nn.sigmoid(gl_ref[...].astype(F32))
    up_p = jnp.dot(y_pool, wup_p_ref[...], preferred_element_type=F32)
    up_a = jnp.dot(ya_ref[...], wup_a_ref[...], preferred_element_type=F32)
    merged = (gates[:, :D_MODEL] * up_p + gates[:, D_MODEL:] * up_a).astype(BF16)
    y = jnp.dot(merged, wout_ref[...], preferred_element_type=F32)
    m = mod_ref[0]
    x1 = x_ref[...] + m[2:3] * _rms(y, gpost_ref[...])
    x1_ref[...] = x1

    h2 = (_rms(x1, gffn_ref[...]) * (1.0 + m[4:5]) + m[3:4]).astype(BF16)
    hbits = lax.bitcast_convert_type(h2.astype(F32), U32)
    half = D_MODEL // 2
    hp_ref[...] = hbits[:, half:] | lax.shift_right_logical(hbits[:, :half], np.uint32(16))

    lane = lax.broadcasted_iota(I32, (tm, LANES), 1)
    logits = jnp.dot(h2, wr_ref[...], preferred_element_type=F32) + br_ref[...]
    work = jnp.where(lane < N_EXPERTS, logits, -jnp.inf)
    vals, hots = [], []
    for _ in range(TOP_K):
        mx = jnp.max(work, axis=-1, keepdims=True)
        first = jnp.min(jnp.where(work == mx, lane, LANES), axis=-1, keepdims=True)
        hot = lane == first
        vals.append(mx)
        hots.append(hot)
        work = jnp.where(hot, -jnp.inf, work)
    ex = [jnp.exp(v - vals[0]) for v in vals]
    den = ex[0] + ex[1] + ex[2] + ex[3]

    onehot = jnp.zeros((tm, LANES), F32)
    for hot in hots:
        onehot = onehot + hot.astype(F32)
    r_i = lax.broadcasted_iota(I32, (tm, tm), 0)
    c_i = lax.broadcasted_iota(I32, (tm, tm), 1)
    tril = (c_i < r_i).astype(BF16)
    before = jnp.dot(tril, onehot.astype(BF16), preferred_element_type=F32) + carry_ref[0:1, :]
    out = jnp.zeros((tm, LANES), F32)
    lane_f = lane.astype(F32)
    for k in range(TOP_K):
        out = jnp.where(lane == k, ex[k] / den, out)
        idx_k = jnp.sum(jnp.where(hots[k], lane_f, 0.0), axis=-1, keepdims=True)
        out = jnp.where(lane == TOP_K + k, idx_k, out)
        rank_k = jnp.sum(jnp.where(hots[k], before, 0.0), axis=-1, keepdims=True)
        out = jnp.where(lane == 2 * TOP_K + k, rank_k, out)
    route_ref[...] = out
    total = carry_ref[0:1, :] + jnp.sum(onehot, axis=0, keepdims=True)
    carry_ref[...] = jnp.broadcast_to(total, carry_ref.shape)
    cnt_ref[...] = jnp.broadcast_to(total, cnt_ref.shape)


def _merge(proj, y_attn, x2, mod3, w_pool_grp, pool_scale, w_up_pool, w_up_attn, w_out, g_post_mix, g_pre_ffn,
           w_router, b_router, seq):
    m_rows = x2.shape[0]
    tm = MERGE_TM
    halo = 16
    assert seq % tm == 0 and max(POOL_WINDOWS) <= halo
    per_seq = seq // tm
    wr = jnp.concatenate([w_router, jnp.zeros((D_MODEL, LANES - N_EXPERTS), F32)], axis=1).astype(BF16)
    br = jnp.concatenate([b_router, jnp.zeros((LANES - N_EXPERTS,), F32)]).reshape(1, LANES)
    const = lambda shape: pl.BlockSpec(shape, lambda i: (0,) * len(shape), pipeline_mode=pl.Buffered(1))
    kern = functools.partial(_merge_kernel, seq=seq)
    return pl.pallas_call(
        kern,
        grid=(m_rows // tm,),
        in_specs=[
            pl.BlockSpec((tm, POOL_WIDTH), lambda i: (i, U_OFF // POOL_WIDTH)),
            pl.BlockSpec((halo, POOL_WIDTH), lambda i: (jnp.maximum(i * (tm // halo) - 1, 0), U_OFF // POOL_WIDTH)),
            pl.BlockSpec((tm, ATTN_WIDTH), lambda i: (i, 0)),
            pl.BlockSpec((tm, N_BRANCHES * D_MODEL), lambda i: (i, 0)),
            pl.BlockSpec((tm, D_MODEL), lambda i: (i, 0)),
            pl.BlockSpec((1, N_MOD, D_MODEL), lambda i: (i // per_seq, 0, 0)),
            const((POOL_GROUPS, POOL_GROUP_DIM, POOL_GROUP_DIM)),
            const((1, POOL_WIDTH)),
            const((POOL_WIDTH, D_MODEL)),
            const((ATTN_WIDTH, D_MODEL)),
            const((D_MODEL, D_MODEL)),
            const((1, D_MODEL)),
            const((1, D_MODEL)),
            const((D_MODEL, LANES)),
            const((1, LANES)),
        ],
        out_specs=[
            pl.BlockSpec((tm, D_MODEL), lambda i: (i, 0)),
            pl.BlockSpec((tm, D_MODEL // 2), lambda i: (i, 0)),
            pl.BlockSpec((tm, LANES), lambda i: (i, 0)),
            pl.BlockSpec((8, LANES), lambda i: (0, 0)),
        ],
        out_shape=[
            jax.ShapeDtypeStruct((m_rows, D_MODEL), F32),
            jax.ShapeDtypeStruct((m_rows, D_MODEL // 2), U32),
            jax.ShapeDtypeStruct((m_rows, LANES), F32),
            jax.ShapeDtypeStruct((8, LANES), F32),
        ],
        scratch_shapes=[pltpu.VMEM((8, LANES), F32)],
        compiler_params=_cparams(("arbitrary",)),
        name="merge",
    )(
        proj, proj, y_attn, proj, x2, mod3,
        w_pool_grp.astype(BF16), pool_scale.reshape(1, POOL_WIDTH), w_up_pool.astype(BF16),
        w_up_attn.astype(BF16), w_out.astype(BF16), g_post_mix.reshape(1, D_MODEL), g_pre_ffn.reshape(1, D_MODEL),
        wr, br,
    )


def _dispatch_kernel(zf_ref, pos_ref, h_ref, xs_ref, zbuf, sem, zsem):
    tm = h_ref.shape[0]
    bm = zbuf.shape[0]

    @pl.when(pl.program_id(0) == 0)
    def _():
        zbuf[...] = jnp.zeros_like(zbuf)

        def zero_copy(b):
            return pltpu.make_async_copy(zbuf, xs_ref.at[pl.ds(b * bm, bm), :], zsem)

        def z_issue(b, carry):
            pl.when(zf_ref[b] != 0)(lambda: zero_copy(b).start())
            return carry

        def z_drain(b, carry):
            pl.when(zf_ref[b] != 0)(lambda: zero_copy(b).wait())
            return carry

        lax.fori_loop(0, zf_ref.shape[0], z_issue, 0)
        lax.fori_loop(0, zf_ref.shape[0], z_drain, 0)

    def row_copy(t, k):
        return pltpu.make_async_copy(
            h_ref.at[pl.ds(t, 1), :], xs_ref.at[pl.ds(pos_ref[t * TOP_K + k], 1), :], sem
        )

    def issue(t, carry):
        for k in range(TOP_K):
            row_copy(t, k).start()
        return carry

    def drain(t, carry):
        for k in range(TOP_K):
            row_copy(t, k).wait()
        return carry

    lax.fori_loop(0, tm, issue, 0)
    lax.fori_loop(0, tm, drain, 0)


def _dispatch(zero_flag, pos_flat, hp, n_rows):
    m_rows, width = hp.shape
    tm = DISPATCH_TM
    grid_spec = pltpu.PrefetchScalarGridSpec(
        num_scalar_prefetch=1,
        grid=(m_rows // tm,),
        in_specs=[
            pl.BlockSpec((tm * TOP_K,), lambda i, zf: (i,), memory_space=pltpu.SMEM),
            pl.BlockSpec((tm, width), lambda i, zf: (i, 0)),
        ],
        out_specs=pl.BlockSpec(memory_space=pl.ANY),
        scratch_shapes=[pltpu.VMEM((MOE_BM, width), U32), pltpu.SemaphoreType.DMA(()), pltpu.SemaphoreType.DMA(())],
    )
    return pl.pallas_call(
        _dispatch_kernel,
        grid_spec=grid_spec,
        out_shape=jax.ShapeDtypeStruct((n_rows, width), U32),
        compiler_params=_cparams(("arbitrary",)),
        name="dispatch",
    )(zero_flag, pos_flat, hp)


def _expert_rows(e, j, bs_ref, nb_ref, nu_ref, x_hbm, o_hbm, xbuf, obuf, xsem, osem, compute, prepare, *, tn):
    bm = xbuf.shape[1]
    n_blocks = x_hbm.shape[0] // bm
    b0 = bs_ref[e]
    nb = nb_ref[e]

    def x_copy(rb, slot):
        return pltpu.make_async_copy(x_hbm.at[pl.ds((b0 + rb) * bm, bm), :], xbuf.at[slot], xsem.at[slot])

    def o_copy(blk, slot):
        return pltpu.make_async_copy(obuf.at[slot], o_hbm.at[pl.ds(blk * bm, bm), pl.ds(j * tn, tn)], osem.at[slot])

    @pl.when(nb > 0)
    def _():
        x_copy(0, 0).start()
        prepare()

        def body(rb, carry):
            slot = rb % 2
            x_copy(rb, slot).wait()

            @pl.when(rb + 1 < nb)
            def _():
                x_copy(rb + 1, 1 - slot).start()

            @pl.when(rb >= 2)
            def _():
                o_copy(b0 + rb - 2, slot).wait()

            obuf[slot] = compute(xbuf[slot])
            o_copy(b0 + rb, slot).start()
            return carry

        lax.fori_loop(0, nb, body, 0)

        @pl.when(nb >= 2)
        def _():
            o_copy(b0 + nb - 2, nb % 2).wait()

        o_copy(b0 + nb - 1, (nb - 1) % 2).wait()

    @pl.when(e == N_EXPERTS - 1)
    def _():
        obuf[0] = jnp.zeros(obuf.shape[1:], obuf.dtype)

        def z_issue(blk, carry):
            o_copy(blk, 0).start()
            return carry

        def z_drain(blk, carry):
            o_copy(blk, 0).wait()
            return carry

        lax.fori_loop(nu_ref[0], n_blocks, z_issue, 0)
        lax.fori_loop(nu_ref[0], n_blocks, z_drain, 0)


def _ffn1_kernel(bs_ref, nb_ref, nu_ref, x_hbm, wg_ref, wl_ref, bg_ref, bl_ref, o_hbm,
                 wg_s, wl_s, xbuf, obuf, xsem, osem, *, tn):
    j, e = pl.program_id(0), pl.program_id(1)
    half = D_MODEL // 2

    def prepare():
        wg_s[...] = wg_ref[0].astype(BF16)
        wl_s[...] = wl_ref[0].astype(BF16)

    def compute(xp):
        lo = lax.bitcast_convert_type(lax.shift_left(xp, np.uint32(16)), F32).astype(BF16)
        hi = lax.bitcast_convert_type(xp & np.uint32(0xFFFF0000), F32).astype(BF16)

        def proj(w_s, b_ref):
            return (
                jnp.dot(lo, w_s[:half, :], preferred_element_type=F32)
                + jnp.dot(hi, w_s[half:, :], preferred_element_type=F32)
                + b_ref[0]
            )

        glu = jnp.minimum(proj(wg_s, bg_ref), SWIGLU_LIMIT)
        lin = jnp.clip(proj(wl_s, bl_ref), -SWIGLU_LIMIT, SWIGLU_LIMIT)
        return (glu * jax.nn.sigmoid(SWIGLU_ALPHA * glu) * (lin + 1.0)).astype(BF16)

    _expert_rows(e, j, bs_ref, nb_ref, nu_ref, x_hbm, o_hbm, xbuf, obuf, xsem, osem, compute, prepare, tn=tn)


def _ffn1(blk_start, blk_count, n_used, xs, w1, b1):
    n_rows = xs.shape[0]
    bm, tn = MOE_BM, FFN1_TN
    nj = D_FF // tn
    b1r = b1.reshape(N_EXPERTS, 1, 2 * D_FF)
    grid_spec = pltpu.PrefetchScalarGridSpec(
        num_scalar_prefetch=3,
        grid=(nj, N_EXPERTS),
        in_specs=[
            pl.BlockSpec(memory_space=pl.ANY),
            pl.BlockSpec((1, D_MODEL, tn), lambda j, e, *_: (e, 0, j)),
            pl.BlockSpec((1, D_MODEL, tn), lambda j, e, *_: (e, 0, nj + j)),
            pl.BlockSpec((1, 1, tn), lambda j, e, *_: (e, 0, j)),
            pl.BlockSpec((1, 1, tn), lambda j, e, *_: (e, 0, nj + j)),
        ],
        out_specs=pl.BlockSpec(memory_space=pl.ANY),
        scratch_shapes=[
            pltpu.VMEM((D_MODEL, tn), BF16),
            pltpu.VMEM((D_MODEL, tn), BF16),
            pltpu.VMEM((2, bm, D_MODEL // 2), U32),
            pltpu.VMEM((2, bm, tn), BF16),
            pltpu.SemaphoreType.DMA((2,)),
            pltpu.SemaphoreType.DMA((2,)),
        ],
    )
    return pl.pallas_call(
        functools.partial(_ffn1_kernel, tn=tn),
        grid_spec=grid_spec,
        out_shape=jax.ShapeDtypeStruct((n_rows, D_FF), BF16),
        compiler_params=_cparams(("arbitrary", "arbitrary")),
        name="ffn1",
    )(blk_start, blk_count, n_used, xs, w1, w1, b1r, b1r)


def _ffn2_kernel(bs_ref, nb_ref, nu_ref, a_hbm, w_ref, b_ref, o_hbm, w_s, xbuf, obuf, xsem, osem, *, tn):
    j, e = pl.program_id(0), pl.program_id(1)

    def prepare():
        w_s[...] = w_ref[0].astype(BF16)

    def compute(a):
        return jnp.dot(a, w_s[...], preferred_element_type=F32) + b_ref[0]

    _expert_rows(e, j, bs_ref, nb_ref, nu_ref, a_hbm, o_hbm, xbuf, obuf, xsem, osem, compute, prepare, tn=tn)


def _ffn2(blk_start, blk_count, n_used, act, w2, b2):
    n_rows = act.shape[0]
    bm, tn = MOE_BM, FFN2_TN
    nj = D_MODEL // tn
    b2r = b2.reshape(N_EXPERTS, 1, D_MODEL)
    grid_spec = pltpu.PrefetchScalarGridSpec(
        num_scalar_prefetch=3,
        grid=(nj, N_EXPERTS),
        in_specs=[
            pl.BlockSpec(memory_space=pl.ANY),
            pl.BlockSpec((1, D_FF, tn), lambda j, e, *_: (e, 0, j)),
            pl.BlockSpec((1, 1, tn), lambda j, e, *_: (e, 0, j)),
        ],
        out_specs=pl.BlockSpec(memory_space=pl.ANY),
        scratch_shapes=[
            pltpu.VMEM((D_FF, tn), BF16),
            pltpu.VMEM((2, bm, D_FF), BF16),
            pltpu.VMEM((2, bm, tn), F32),
            pltpu.SemaphoreType.DMA((2,)),
            pltpu.SemaphoreType.DMA((2,)),
        ],
    )
    return pl.pallas_call(
        functools.partial(_ffn2_kernel, tn=tn),
        grid_spec=grid_spec,
        out_shape=jax.ShapeDtypeStruct((n_rows, D_MODEL), F32),
        compiler_params=_cparams(("arbitrary", "arbitrary")),
        name="ffn2",
    )(blk_start, blk_count, n_used, act, w2, b2r)


def _combine_kernel(pos_ref, ys_ref, route_ref, x1_ref, mod_ref, g_ref, o_ref, buf, sem):
    tm = x1_ref.shape[0]

    def row_copy(t, k):
        return pltpu.make_async_copy(
            ys_ref.at[pl.ds(pos_ref[t * TOP_K + k], 1), :], buf.at[k, pl.ds(t, 1), :], sem
        )

    def issue(t, carry):
        for k in range(TOP_K):
            row_copy(t, k).start()
        return carry

    def drain(t, carry):
        for k in range(TOP_K):
            row_copy(t, k).wait()
        return carry

    lax.fori_loop(0, tm, issue, 0)
    lax.fori_loop(0, tm, drain, 0)
    route = route_ref[...]
    y = route[:, 0:1] * buf[0]
    for k in range(1, TOP_K):
        y = y + route[:, k:k + 1] * buf[k]
    m = mod_ref[0]
    o_ref[...] = x1_ref[...] + m[5:6] * _rms(y, g_ref[...])


def _combine(pos_flat, ys, route, x1, mod3, g_post_ffn, seq):
    m_rows = x1.shape[0]
    tm = COMBINE_TM
    per_seq = seq // tm
    return pl.pallas_call(
        _combine_kernel,
        grid=(m_rows // tm,),
        in_specs=[
            pl.BlockSpec((tm * TOP_K,), lambda i: (i,), memory_space=pltpu.SMEM),
            pl.BlockSpec(memory_space=pl.ANY),
            pl.BlockSpec((tm, LANES), lambda i: (i, 0)),
            pl.BlockSpec((tm, D_MODEL), lambda i: (i, 0)),
            pl.BlockSpec((1, N_MOD, D_MODEL), lambda i: (i // per_seq, 0, 0)),
            pl.BlockSpec((1, D_MODEL), lambda i: (0, 0)),
        ],
        out_specs=pl.BlockSpec((tm, D_MODEL), lambda i: (i, 0)),
        out_shape=jax.ShapeDtypeStruct((m_rows, D_MODEL), F32),
        scratch_shapes=[pltpu.VMEM((TOP_K, tm, D_MODEL), F32), pltpu.SemaphoreType.DMA(())],
        compiler_params=_cparams(("arbitrary",)),
        name="combine",
    )(pos_flat, ys, route, x1, mod3, g_post_ffn.reshape(1, D_MODEL))


def _layer(x2, mod3, bsz, seq, g_pre_mix, g_post_mix, w_in, w_pool_grp, pool_scale, w_up_pool, w_up_attn, w_out,
           g_pre_ffn, g_post_ffn, w_router, b_router, w1, b1, w2, b2):
    m_rows = x2.shape[0]
    proj, small = _in_proj(x2, mod3, g_pre_mix, w_in, seq)
    y_attn = _attn(proj, small, bsz, seq)
    x1, hp, route, cnt = _merge(proj, y_attn, x2, mod3, w_pool_grp, pool_scale, w_up_pool, w_up_attn, w_out,
                                g_post_mix, g_pre_ffn, w_router, b_router, seq)

    bm = MOE_BM
    n_slots = m_rows * TOP_K
    n_blocks = -(-n_slots // bm) + N_EXPERTS
    idx = route[:, TOP_K:2 * TOP_K].astype(I32)
    rank = route[:, 2 * TOP_K:3 * TOP_K].astype(I32)
    counts = cnt[0, :N_EXPERTS].astype(I32)
    padded = (counts + bm - 1) // bm * bm
    pad_end = jnp.cumsum(padded)
    pad_start = pad_end - padded
    pos_flat = (pad_start[idx] + rank).reshape(-1)
    n_used = (pad_end[-1] // bm).astype(I32).reshape(1)
    blk_ids = jnp.arange(n_blocks, dtype=I32)
    blk_start = (pad_start // bm).astype(I32)
    blk_count = (padded // bm).astype(I32)
    last_of_expert = jnp.any((blk_ids[:, None] + 1) * bm == pad_end[None, :], axis=1)
    zero_flag = (last_of_expert | (blk_ids >= n_used[0])).astype(I32)

    xs = _dispatch(zero_flag, pos_flat, hp, n_blocks * bm)
    act = _ffn1(blk_start, blk_count, n_used, xs, w1, b1)
    ys = _ffn2(blk_start, blk_count, n_used, act, w2, b2)
    return _combine(pos_flat, ys, route, x1, mod3, g_post_ffn, seq)


def kernel(x, c, w_ada, b_ada, g_pre_mix, g_post_mix, w_in, w_pool_grp, pool_scale, w_up_pool, w_up_attn, w_out,
           g_pre_ffn, g_post_ffn, w_router, b_router, w1, b1, w2, b2):
    bsz, seq, d = x.shape
    assert d == D_MODEL
    depth = w_ada.shape[0]
    x2 = x.reshape(bsz * seq, d)
    for layer in range(depth):
        mod3 = _ada(c, w_ada[layer], b_ada[layer]).reshape(bsz, N_MOD, D_MODEL)
        x2 = _layer(x2, mod3, bsz, seq, g_pre_mix[layer], g_post_mix[layer], w_in[layer], w_pool_grp[layer],
                    pool_scale[layer], w_up_pool[layer], w_up_attn[layer], w_out[layer], g_pre_ffn[layer],
                    g_post_ffn[layer], w_router[layer], b_router[layer], w1[layer], b1[layer], w2[layer], b2[layer])
    return x2.reshape(bsz, seq, d)
```

```python
import functools

import jax
import jax.numpy as jnp
import numpy as np
from jax import lax
from jax.experimental import pallas as pl
from jax.experimental.pallas import tpu as pltpu

F32 = jnp.float32
BF16 = jnp.bfloat16
I32 = jnp.int32
U32 = jnp.uint32

D_MODEL = 2048
POOL_WINDOWS = (2, 4, 8, 16)
POOL_GROUPS = 4
POOL_GROUP_DIM = D_MODEL // 8
POOL_WIDTH = POOL_GROUPS * POOL_GROUP_DIM
N_HEADS = 16
N_KV_HEADS = 4
HEAD_DIM = 128
ATTN_WIDTH = N_HEADS * HEAD_DIM
KV_WIDTH = N_KV_HEADS * HEAD_DIM
ROT_DIM = HEAD_DIM // 4
IDX_HEADS = 16
IDX_DIM = 64
IDX_ROT_DIM = IDX_DIM // 4
TOPK_MAX = 256
ROPE_THETA = 500000.0
N_BRANCHES = 2
N_EXPERTS = 32
TOP_K = 4
D_FF = D_MODEL
SWIGLU_ALPHA = 1.702
SWIGLU_LIMIT = 7.0
N_MOD = 6
EPS = 1e-6

LANES = 128
INT_MIN = np.int32(-(2 ** 31))

GL_OFF = 0
Q_OFF = GL_OFF + N_BRANCHES * D_MODEL
U_OFF = Q_OFF + ATTN_WIDTH
QI_OFF = U_OFF + POOL_WIDTH
K_OFF = QI_OFF + IDX_HEADS * IDX_DIM
V_OFF = K_OFF + KV_WIDTH
PROJ_WIDTH = V_OFF + KV_WIDTH

IN_TN = 1024
ATTN_TQ = 256
ATTN_KC = 512
MASK_NEG = -1e30
MERGE_TM = 256
MOE_BM = 512
FFN1_TN = 512
FFN2_TN = 1024
DISPATCH_TM = 256
COMBINE_TM = 128
VMEM_LIMIT = 56 * 1024 * 1024


def _cparams(sem):
    return pltpu.CompilerParams(dimension_semantics=sem, vmem_limit_bytes=VMEM_LIMIT)


def _rms(x, g):
    return x * lax.rsqrt(jnp.mean(x * x, axis=-1, keepdims=True) + EPS) * g


def _ada_kernel(c_ref, w_ref, b_ref, o_ref):
    c = c_ref[...]
    ca = (c * jax.nn.sigmoid(c)).astype(BF16)
    o_ref[...] = jnp.dot(ca, w_ref[...].astype(BF16), preferred_element_type=F32) + b_ref[...]


def _ada(c, w_ada, b_ada):
    bsz = c.shape[0]
    n = w_ada.shape[1]
    tn = 1024
    return pl.pallas_call(
        _ada_kernel,
        grid=(n // tn,),
        in_specs=[
            pl.BlockSpec((bsz, D_MODEL), lambda j: (0, 0)),
            pl.BlockSpec((D_MODEL, tn), lambda j: (0, j)),
            pl.BlockSpec((1, tn), lambda j: (0, j)),
        ],
        out_specs=pl.BlockSpec((bsz, tn), lambda j: (0, j)),
        out_shape=jax.ShapeDtypeStruct((bsz, n), F32),
        compiler_params=_cparams(("arbitrary",)),
        name="ada",
    )(c, w_ada, b_ada.reshape(1, n))


def _rope_slice(xs, c, s1, s2, shift):
    return xs * c + pltpu.roll(xs, LANES - shift, 1) * s1 + pltpu.roll(xs, shift, 1) * s2


def _in_kernel(x_ref, mod_ref, g_ref, w_ref, ws_ref, tab_ref, o_ref, os_ref, h_ref, acc_ref):
    j = pl.program_id(1)
    half_a = ROT_DIM // 2
    half_i = IDX_ROT_DIM // 2

    @pl.when(j == 0)
    def _():
        m = mod_ref[0]
        h = _rms(x_ref[...], g_ref[...]) * (1.0 + m[1:2]) + m[0:1]
        hb = h.astype(BF16)
        h_ref[...] = hb
        small = jnp.dot(hb, ws_ref[...], preferred_element_type=F32)
        os_ref[...] = _rope_slice(small, tab_ref[6], tab_ref[7], tab_ref[8], half_i)

    acc_ref[...] = jnp.dot(h_ref[...], w_ref[...], preferred_element_type=F32)
    n_sl = IN_TN // LANES
    j_q0, j_q1 = Q_OFF // IN_TN, U_OFF // IN_TN
    j_qi = QI_OFF // IN_TN
    j_kv = K_OFF // IN_TN
    k_sl = KV_WIDTH // LANES

    @pl.when((j < j_q0) | (j == U_OFF // IN_TN))
    def _():
        o_ref[...] = acc_ref[...].astype(BF16)

    @pl.when((j >= j_q0) & (j < j_q1))
    def _():
        for s in range(n_sl):
            sl = slice(s * LANES, (s + 1) * LANES)
            o_ref[:, sl] = _rope_slice(acc_ref[:, sl], tab_ref[0], tab_ref[1], tab_ref[2], half_a).astype(BF16)

    @pl.when(j == j_qi)
    def _():
        for s in range(n_sl):
            sl = slice(s * LANES, (s + 1) * LANES)
            o_ref[:, sl] = _rope_slice(acc_ref[:, sl], tab_ref[3], tab_ref[4], tab_ref[5], half_i).astype(BF16)

    @pl.when(j == j_kv)
    def _():
        for s in range(k_sl):
            sl = slice(s * LANES, (s + 1) * LANES)
            o_ref[:, sl] = _rope_slice(acc_ref[:, sl], tab_ref[0], tab_ref[1], tab_ref[2], half_a).astype(BF16)
        o_ref[:, KV_WIDTH:] = acc_ref[:, KV_WIDTH:].astype(BF16)


def _rope_tables(seq):
    def tabs(rot_dim, period, n_rep):
        half = rot_dim // 2
        inv = np.float32(ROPE_THETA) ** (-np.arange(0, rot_dim, 2, dtype=np.float32) / np.float32(rot_dim))
        ang = np.arange(seq, dtype=np.float32)[:, None] * inv.astype(np.float32)[None, :]
        cos, sin = np.cos(ang).astype(np.float32), np.sin(ang).astype(np.float32)
        ones = np.ones((seq, period - 2 * half), np.float32)
        z_h = np.zeros((seq, half), np.float32)
        c = np.concatenate([cos, cos, ones], axis=1)
        s1 = np.concatenate([-sin, z_h, 0 * ones], axis=1)
        s2 = np.concatenate([z_h, sin, 0 * ones], axis=1)
        return [np.tile(t, (1, n_rep)) for t in (c, s1, s2)]

    ta = tabs(ROT_DIM, HEAD_DIM, LANES // HEAD_DIM)
    ti = tabs(IDX_ROT_DIM, IDX_DIM, LANES // IDX_DIM)
    ts = tabs(IDX_ROT_DIM, IDX_DIM, 1)
    wi_scale = np.float32(IDX_HEADS ** -0.5 * IDX_DIM ** -0.5)
    pad = LANES - IDX_DIM
    c_tail = np.where(np.arange(pad) < IDX_HEADS, wi_scale, np.float32(0.0)).astype(np.float32)
    ts = [
        np.concatenate([ts[0], np.broadcast_to(c_tail, (seq, pad))], axis=1),
        np.concatenate([ts[1], np.zeros((seq, pad), np.float32)], axis=1),
        np.concatenate([ts[2], np.zeros((seq, pad), np.float32)], axis=1),
    ]
    return jnp.asarray(np.stack(ta + ti + ts, axis=0))


def _in_proj(x2, mod3, g_pre, w_in, seq):
    m_rows = x2.shape[0]
    tm = min(1024, seq)
    assert seq % tm == 0 and m_rows % tm == 0
    per_seq = seq // tm
    offs = np.cumsum((POOL_WIDTH, ATTN_WIDTH, KV_WIDTH, KV_WIDTH, IDX_HEADS * IDX_DIM, IDX_DIM, IDX_HEADS))
    w_u, w_q, w_k, w_v, w_qi, w_ki, w_wi, w_gl = jnp.split(w_in, [int(o) for o in offs], axis=1)
    w_main = jnp.concatenate([w_gl, w_q, w_u, w_qi, w_k, w_v], axis=1).astype(BF16)
    w_small = jnp.concatenate(
        [w_ki, w_wi, jnp.zeros((D_MODEL, LANES - IDX_DIM - IDX_HEADS), F32)], axis=1
    ).astype(BF16)
    tabs = _rope_tables(seq)
    n_j = PROJ_WIDTH // IN_TN
    return pl.pallas_call(
        _in_kernel,
        grid=(m_rows // tm, n_j),
        in_specs=[
            pl.BlockSpec((tm, D_MODEL), lambda i, j: (i, 0)),
            pl.BlockSpec((1, N_MOD, D_MODEL), lambda i, j: (i // per_seq, 0, 0)),
            pl.BlockSpec((1, D_MODEL), lambda i, j: (0, 0)),
            pl.BlockSpec((D_MODEL, IN_TN), lambda i, j: (0, j)),
            pl.BlockSpec((D_MODEL, LANES), lambda i, j: (0, 0)),
            pl.BlockSpec((9, tm, LANES), lambda i, j: (0, i % per_seq, 0)),
        ],
        out_specs=[
            pl.BlockSpec((tm, IN_TN), lambda i, j: (i, j)),
            pl.BlockSpec((tm, LANES), lambda i, j: (i, 0)),
        ],
        out_shape=[
            jax.ShapeDtypeStruct((m_rows, PROJ_WIDTH), BF16),
            jax.ShapeDtypeStruct((m_rows, LANES), F32),
        ],
        scratch_shapes=[pltpu.VMEM((tm, D_MODEL), BF16), pltpu.VMEM((tm, IN_TN), F32)],
        compiler_params=_cparams(("arbitrary", "arbitrary")),
        name="in_proj",
    )(x2, mod3, g_pre.reshape(1, D_MODEL), w_main, w_small, tabs)


def _attn_kernel(q_ref, qi_ref, k_ref, v_ref, kis_ref, wis_ref, o_ref,
                 key_ref, plim_ref, qs_ref, m_ref, l_ref, acc_ref, *, tq, kc, seq, n_keep):
    i = pl.program_id(1)
    nk = ((i + 1) * tq + kc - 1) // kc
    nt = (((1,), (1,)), ((), ()))
    grp = N_HEADS // N_KV_HEADS
    row_io = lax.broadcasted_iota(I32, (kc, tq), 0)
    tpos = i * tq + lax.broadcasted_iota(I32, (kc, tq), 1)
    w_t = wis_ref[...].T

    def idx_chunk(c, carry):
        r0 = pl.multiple_of(c * kc, kc)
        ki = kis_ref[pl.ds(r0, kc), :IDX_DIM].astype(BF16)
        sc = jnp.zeros((kc, tq), F32)
        for h in range(IDX_HEADS):
            qh = qi_ref[:, h * IDX_DIM:(h + 1) * IDX_DIM]
            d = lax.dot_general(ki, qh, nt, preferred_element_type=F32)
            sc = sc + jnp.maximum(d, 0.0) * w_t[IDX_DIM + h:IDX_DIM + h + 1, :]
        bits = lax.bitcast_convert_type(sc, I32)
        key = jnp.where(bits < 0, bits ^ np.int32(0x7FFFFFFF), bits)
        key_ref[pl.ds(r0, kc), :] = jnp.where(r0 + row_io <= tpos, key, INT_MIN)
        return carry

    lax.fori_loop(0, nk, idx_chunk, 0)

    def count(pred):
        def body(c, acc):
            r0 = pl.multiple_of(c * kc, kc)
            hit = pred(key_ref[pl.ds(r0, kc), :], r0)
            return acc + jnp.sum(hit.astype(I32), axis=0, keepdims=True)

        return lax.fori_loop(0, nk, body, jnp.zeros((1, tq), I32))

    def bit_step(it, tu):
        cand = tu | lax.shift_left(np.int32(1), 31 - it)
        cnt = count(lambda kch, r0: kch >= (cand ^ INT_MIN))
        return jnp.where(cnt >= n_keep, cand, tu)

    tu = lax.fori_loop(0, 32, bit_step, jnp.zeros((1, tq), I32))
    thr = tu ^ INT_MIN
    cnt_gt = count(lambda kch, r0: kch > thr)
    cnt_ge = count(lambda kch, r0: kch >= thr)
    tie_i = ((cnt_ge > n_keep) & (tu != 0)).astype(I32)
    need = n_keep - cnt_gt

    plim_ref[...] = jnp.full(plim_ref.shape, seq, I32)

    @pl.when(jnp.max(tie_i) > 0)
    def _():
        n_bits = (seq - 1).bit_length()

        def idx_step(it, p):
            cand = p | lax.shift_left(np.int32(1), (n_bits - 1) - it)
            below = count(lambda kch, r0: (kch == thr) & (r0 + row_io < cand))
            return jnp.where(below < need, cand, p)

        p = lax.fori_loop(0, n_bits, idx_step, jnp.zeros((1, tq), I32))
        plim_ref[...] = jnp.broadcast_to(jnp.where(tie_i > 0, p, seq), plim_ref.shape)

    for g in range(N_KV_HEADS):
        for r in range(grp):
            hh = g * grp + r
            qs_ref[g, r * tq:(r + 1) * tq, :] = q_ref[:, hh * HEAD_DIM:(hh + 1) * HEAD_DIM]
    m_ref[...] = jnp.full(m_ref.shape, MASK_NEG, F32)
    l_ref[...] = jnp.zeros_like(l_ref)
    acc_ref[...] = jnp.zeros_like(acc_ref)
    c_exp = HEAD_DIM ** -0.5 * np.log2(np.e)

    def att_chunk(c, carry):
        r0 = pl.multiple_of(c * kc, kc)
        kch = key_ref[pl.ds(r0, kc), :]
        row = r0 + row_io
        sel = ((kch > thr) | ((kch == thr) & (row <= plim_ref[0:1, :]))) & (row <= tpos)
        bias_t = jnp.where(sel, 0.0, MASK_NEG).astype(F32).T
        for g in range(N_KV_HEADS):
            kg = k_ref[pl.ds(r0, kc), g * HEAD_DIM:(g + 1) * HEAD_DIM]
            vg = v_ref[pl.ds(r0, kc), g * HEAD_DIM:(g + 1) * HEAD_DIM]
            s = lax.dot_general(qs_ref[g], kg, nt, preferred_element_type=F32)
            s = (s.reshape(grp, tq, kc) + bias_t[None]).reshape(grp * tq, kc)
            m_old = m_ref[g]
            m_new = jnp.maximum(m_old, jnp.max(s, axis=-1, keepdims=True))
            alpha = jnp.exp2((m_old - m_new) * c_exp)
            p = jnp.exp2((s - m_new) * c_exp)
            l_ref[g] = alpha * l_ref[g] + jnp.sum(p, axis=-1, keepdims=True)
            acc_ref[g] = alpha * acc_ref[g] + jnp.dot(p.astype(BF16), vg, preferred_element_type=F32)
            m_ref[g] = m_new
        return carry

    lax.fori_loop(0, nk, att_chunk, 0)
    for g in range(N_KV_HEADS):
        o = acc_ref[g] / l_ref[g]
        for r in range(grp):
            hh = g * grp + r
            o_ref[:, hh * HEAD_DIM:(hh + 1) * HEAD_DIM] = o[r * tq:(r + 1) * tq, :].astype(BF16)


def _attn(proj, small, bsz, seq):
    tq = min(ATTN_TQ, seq)
    kc = min(ATTN_KC, seq)
    assert seq % tq == 0 and seq % kc == 0 and kc % tq == 0
    nq = seq // tq
    n_keep = min(TOPK_MAX, seq // 4)
    grp = N_HEADS // N_KV_HEADS
    kern = functools.partial(_attn_kernel, tq=tq, kc=kc, seq=seq, n_keep=n_keep)
    return pl.pallas_call(
        kern,
        grid=(bsz, nq),
        in_specs=[
            pl.BlockSpec((tq, ATTN_WIDTH), lambda b, i: (b * nq + i, Q_OFF // ATTN_WIDTH)),
            pl.BlockSpec((tq, IDX_HEADS * IDX_DIM), lambda b, i: (b * nq + i, QI_OFF // (IDX_HEADS * IDX_DIM))),
            pl.BlockSpec((seq, KV_WIDTH), lambda b, i: (b, K_OFF // KV_WIDTH)),
            pl.BlockSpec((seq, KV_WIDTH), lambda b, i: (b, V_OFF // KV_WIDTH)),
            pl.BlockSpec((seq, LANES), lambda b, i: (b, 0)),
            pl.BlockSpec((tq, LANES), lambda b, i: (b * nq + i, 0)),
        ],
        out_specs=pl.BlockSpec((tq, ATTN_WIDTH), lambda b, i: (b * nq + i, 0)),
        out_shape=jax.ShapeDtypeStruct((bsz * seq, ATTN_WIDTH), BF16),
        scratch_shapes=[
            pltpu.VMEM((seq, tq), I32),
            pltpu.VMEM((8, tq), I32),
            pltpu.VMEM((N_KV_HEADS, grp * tq, HEAD_DIM), BF16),
            pltpu.VMEM((N_KV_HEADS, grp * tq, 1), F32),
            pltpu.VMEM((N_KV_HEADS, grp * tq, 1), F32),
            pltpu.VMEM((N_KV_HEADS, grp * tq, HEAD_DIM), F32),
        ],
        compiler_params=_cparams(("arbitrary", "arbitrary")),
        name="attn",
    )(proj, proj, proj, proj, small, small)


def _merge_kernel(
    u_ref, halo_ref, ya_ref, gl_ref, x_ref, mod_ref, wgrp_ref, pscale_ref, wup_p_ref, wup_a_ref, wout_ref,
    gpost_ref, gffn_ref, wr_ref, br_ref,
    x1_ref, hp_ref, route_ref, cnt_ref,
    carry_ref, *, seq,
):
    tm = MERGE_TM
    i = pl.program_id(0)
    pos0 = (i * tm) % seq

    @pl.when(i == 0)
    def _():
        carry_ref[...] = jnp.zeros_like(carry_ref)

    halo_rows = halo_ref.shape[0]
    halo = jnp.where(pos0 == 0, 0.0, halo_ref[...].astype(F32))
    ext = jnp.concatenate([halo, u_ref[...].astype(F32)], axis=0)
    pos = pos0 + lax.broadcasted_iota(I32, (tm, 1), 0)
    ys = []
    for g, win in enumerate(POOL_WINDOWS):
        sl = slice(g * POOL_GROUP_DIM, (g + 1) * POOL_GROUP_DIM)
        e = ext[:, sl]
        acc = e
        span = 1
        while span < win:
            acc = acc + pltpu.roll(acc, span, 0)
            span *= 2
        cnt = jnp.minimum(pos + 1, win).astype(F32)
        mixed = (acc[halo_rows:] / cnt - e[halo_rows:]).astype(BF16)
        ys.append(jnp.dot(mixed, wgrp_ref[g], preferred_element_type=F32))
    y_pool = (jnp.concatenate(ys, axis=1) * pscale_ref[...]).astype(BF16)

    gates = jax.nn.sigmoid(gl_ref[...].astype(F32))
    up_p = jnp.dot(y_pool, wup_p_ref[...], preferred_element_type=F32)
    up_a = jnp.dot(ya_ref[...], wup_a_ref[...], preferred_element_type=F32)
    merged = (gates[:, :D_MODEL] * up_p + gates[:, D_MODEL:] * up_a).astype(BF16)
    y = jnp.dot(merged, wout_ref[...], preferred_element_type=F32)
    m = mod_ref[0]
    x1 = x_ref[...] + m[2:3] * _rms(y, gpost_ref[...])
    x1_ref[...] = x1

    h2 = (_rms(x1, gffn_ref[...]) * (1.0 + m[4:5]) + m[3:4]).astype(BF16)
    hbits = lax.bitcast_convert_type(h2.astype(F32), U32)
    half = D_MODEL // 2
    hp_ref[...] = hbits[:, half:] | lax.shift_right_logical(hbits[:, :half], np.uint32(16))

    lane = lax.broadcasted_iota(I32, (tm, LANES), 1)
    logits = jnp.dot(h2, wr_ref[...], preferred_element_type=F32) + br_ref[...]
    work = jnp.where(lane < N_EXPERTS, logits, -jnp.inf)
    vals, hots = [], []
    for _ in range(TOP_K):
        mx = jnp.max(work, axis=-1, keepdims=True)
        first = jnp.min(jnp.where(work == mx, lane, LANES), axis=-1, keepdims=True)
        hot = lane == first
        vals.append(mx)
        hots.append(hot)
        work = jnp.where(hot, -jnp.inf, work)
    ex = [jnp.exp(v - vals[0]) for v in vals]
    den = ex[0] + ex[1] + ex[2] + ex[3]

    onehot = jnp.zeros((tm, LANES), F32)
    for hot in hots:
        onehot = onehot + hot.astype(F32)
    r_i = lax.broadcasted_iota(I32, (tm, tm), 0)
    c_i = lax.broadcasted_iota(I32, (tm, tm), 1)
    tril = (c_i < r_i).astype(BF16)
    before = jnp.dot(tril, onehot.astype(BF16), preferred_element_type=F32) + carry_ref[0:1, :]
    out = jnp.zeros((tm, LANES), F32)
    lane_f = lane.astype(F32)
    for k in range(TOP_K):
        out = jnp.where(lane == k, ex[k] / den, out)
        idx_k = jnp.sum(jnp.where(hots[k], lane_f, 0.0), axis=-1, keepdims=True)
        out = jnp.where(lane == TOP_K + k, idx_k, out)
        rank_k = jnp.sum(jnp.where(hots[k], before, 0.0), axis=-1, keepdims=True)
        out = jnp.where(lane == 2 * TOP_K + k, rank_k, out)
    route_ref[...] = out
    total = carry_ref[0:1, :] + jnp.sum(onehot, axis=0, keepdims=True)
    carry_ref[...] = jnp.broadcast_to(total, carry_ref.shape)
    cnt_ref[...] = jnp.broadcast_to(total, cnt_ref.shape)


def _merge(proj, y_attn, x2, mod3, w_pool_grp, pool_scale, w_up_pool, w_up_attn, w_out, g_post_mix, g_pre_ffn,
           w_router, b_router, seq):
    m_rows = x2.shape[0]
    tm = MERGE_TM
    halo = 16
    assert seq % tm == 0 and max(POOL_WINDOWS) <= halo
    per_seq = seq // tm
    wr = jnp.concatenate([w_router, jnp.zeros((D_MODEL, LANES - N_EXPERTS), F32)], axis=1).astype(BF16)
    br = jnp.concatenate([b_router, jnp.zeros((LANES - N_EXPERTS,), F32)]).reshape(1, LANES)
    const = lambda shape: pl.BlockSpec(shape, lambda i: (0,) * len(shape), pipeline_mode=pl.Buffered(1))
    kern = functools.partial(_merge_kernel, seq=seq)
    return pl.pallas_call(
        kern,
        grid=(m_rows // tm,),
        in_specs=[
            pl.BlockSpec((tm, POOL_WIDTH), lambda i: (i, U_OFF // POOL_WIDTH)),
            pl.BlockSpec((halo, POOL_WIDTH), lambda i: (jnp.maximum(i * (tm // halo) - 1, 0), U_OFF // POOL_WIDTH)),
            pl.BlockSpec((tm, ATTN_WIDTH), lambda i: (i, 0)),
            pl.BlockSpec((tm, N_BRANCHES * D_MODEL), lambda i: (i, 0)),
            pl.BlockSpec((tm, D_MODEL), lambda i: (i, 0)),
            pl.BlockSpec((1, N_MOD, D_MODEL), lambda i: (i // per_seq, 0, 0)),
            const((POOL_GROUPS, POOL_GROUP_DIM, POOL_GROUP_DIM)),
            const((1, POOL_WIDTH)),
            const((POOL_WIDTH, D_MODEL)),
            const((ATTN_WIDTH, D_MODEL)),
            const((D_MODEL, D_MODEL)),
            const((1, D_MODEL)),
            const((1, D_MODEL)),
            const((D_MODEL, LANES)),
            const((1, LANES)),
        ],
        out_specs=[
            pl.BlockSpec((tm, D_MODEL), lambda i: (i, 0)),
            pl.BlockSpec((tm, D_MODEL // 2), lambda i: (i, 0)),
            pl.BlockSpec((tm, LANES), lambda i: (i, 0)),
            pl.BlockSpec((8, LANES), lambda i: (0, 0)),
        ],
        out_shape=[
            jax.ShapeDtypeStruct((m_rows, D_MODEL), F32),
            jax.ShapeDtypeStruct((m_rows, D_MODEL // 2), U32),
            jax.ShapeDtypeStruct((m_rows, LANES), F32),
            jax.ShapeDtypeStruct((8, LANES), F32),
        ],
        scratch_shapes=[pltpu.VMEM((8, LANES), F32)],
        compiler_params=_cparams(("arbitrary",)),
        name="merge",
    )(
        proj, proj, y_attn, proj, x2, mod3,
        w_pool_grp.astype(BF16), pool_scale.reshape(1, POOL_WIDTH), w_up_pool.astype(BF16),
        w_up_attn.astype(BF16), w_out.astype(BF16), g_post_mix.reshape(1, D_MODEL), g_pre_ffn.reshape(1, D_MODEL),
        wr, br,
    )


def _dispatch_kernel(zf_ref, pos_ref, h_ref, xs_ref, zbuf, sem, zsem):
    tm = h_ref.shape[0]
    bm = zbuf.shape[0]

    @pl.when(pl.program_id(0) == 0)
    def _():
        zbuf[...] = jnp.zeros_like(zbuf)

        def zero_copy(b):
            return pltpu.make_async_copy(zbuf, xs_ref.at[pl.ds(b * bm, bm), :], zsem)

        def z_issue(b, carry):
            pl.when(zf_ref[b] != 0)(lambda: zero_copy(b).start())
            return carry

        def z_drain(b, carry):
            pl.when(zf_ref[b] != 0)(lambda: zero_copy(b).wait())
            return carry

        lax.fori_loop(0, zf_ref.shape[0], z_issue, 0)
        lax.fori_loop(0, zf_ref.shape[0], z_drain, 0)

    def row_copy(t, k):
        return pltpu.make_async_copy(
            h_ref.at[pl.ds(t, 1), :], xs_ref.at[pl.ds(pos_ref[t * TOP_K + k], 1), :], sem
        )

    def issue(t, carry):
        for k in range(TOP_K):
            row_copy(t, k).start()
        return carry

    def drain(t, carry):
        for k in range(TOP_K):
            row_copy(t, k).wait()
        return carry

    lax.fori_loop(0, tm, issue, 0)
    lax.fori_loop(0, tm, drain, 0)


def _dispatch(zero_flag, pos_flat, hp, n_rows):
    m_rows, width = hp.shape
    tm = DISPATCH_TM
    grid_spec = pltpu.PrefetchScalarGridSpec(
        num_scalar_prefetch=1,
        grid=(m_rows // tm,),
        in_specs=[
            pl.BlockSpec((tm * TOP_K,), lambda i, zf: (i,), memory_space=pltpu.SMEM),
            pl.BlockSpec((tm, width), lambda i, zf: (i, 0)),
        ],
        out_specs=pl.BlockSpec(memory_space=pl.ANY),
        scratch_shapes=[pltpu.VMEM((MOE_BM, width), U32), pltpu.SemaphoreType.DMA(()), pltpu.SemaphoreType.DMA(())],
    )
    return pl.pallas_call(
        _dispatch_kernel,
        grid_spec=grid_spec,
        out_shape=jax.ShapeDtypeStruct((n_rows, width), U32),
        compiler_params=_cparams(("arbitrary",)),
        name="dispatch",
    )(zero_flag, pos_flat, hp)


def _expert_rows(e, j, bs_ref, nb_ref, nu_ref, x_hbm, o_hbm, xbuf, obuf, xsem, osem, compute, prepare, *, tn):
    bm = xbuf.shape[1]
    n_blocks = x_hbm.shape[0] // bm
    b0 = bs_ref[e]
    nb = nb_ref[e]
    n_used = nu_ref[0]

    def x_copy(blk, slot):
        return pltpu.make_async_copy(x_hbm.at[pl.ds(blk * bm, bm), :], xbuf.at[slot], xsem.at[slot])

    def o_copy(blk, slot):
        return pltpu.make_async_copy(obuf.at[slot], o_hbm.at[pl.ds(blk * bm, bm), pl.ds(j * tn, tn)], osem.at[slot])

    @pl.when(nb > 0)
    def _():
        prepare()

        def body(rb, carry):
            g = b0 + rb
            slot = g % 2

            @pl.when(g == 0)
            def _():
                x_copy(0, 0).start()

            x_copy(g, slot).wait()

            @pl.when(g + 1 < n_used)
            def _():
                x_copy(g + 1, 1 - slot).start()

            @pl.when(g >= 2)
            def _():
                o_copy(g - 2, slot).wait()

            obuf[slot] = compute(xbuf[slot])
            o_copy(g, slot).start()

            @pl.when(g == n_used - 1)
            def _():
                @pl.when(g >= 1)
                def _():
                    o_copy(g - 1, 1 - slot).wait()

                o_copy(g, slot).wait()

            return carry

        lax.fori_loop(0, nb, body, 0)

    @pl.when(e == N_EXPERTS - 1)
    def _():
        obuf[0] = jnp.zeros(obuf.shape[1:], obuf.dtype)

        def z_issue(blk, carry):
            o_copy(blk, 0).start()
            return carry

        def z_drain(blk, carry):
            o_copy(blk, 0).wait()
            return carry

        lax.fori_loop(nu_ref[0], n_blocks, z_issue, 0)
        lax.fori_loop(nu_ref[0], n_blocks, z_drain, 0)


def _ffn1_kernel(bs_ref, nb_ref, nu_ref, x_hbm, wg_ref, wl_ref, bg_ref, bl_ref, o_hbm,
                 wg_s, wl_s, xbuf, obuf, xsem, osem, *, tn):
    j, e = pl.program_id(0), pl.program_id(1)
    half = D_MODEL // 2

    def prepare():
        wg_s[...] = wg_ref[0].astype(BF16)
        wl_s[...] = wl_ref[0].astype(BF16)

    def compute(xp):
        lo = lax.bitcast_convert_type(lax.shift_left(xp, np.uint32(16)), F32).astype(BF16)
        hi = lax.bitcast_convert_type(xp & np.uint32(0xFFFF0000), F32).astype(BF16)

        def proj(w_s, b_ref):
            return (
                jnp.dot(lo, w_s[:half, :], preferred_element_type=F32)
                + jnp.dot(hi, w_s[half:, :], preferred_element_type=F32)
                + b_ref[0]
            )

        glu = jnp.minimum(proj(wg_s, bg_ref), SWIGLU_LIMIT)
        lin = jnp.clip(proj(wl_s, bl_ref), -SWIGLU_LIMIT, SWIGLU_LIMIT)
        return (glu * jax.nn.sigmoid(SWIGLU_ALPHA * glu) * (lin + 1.0)).astype(BF16)

    _expert_rows(e, j, bs_ref, nb_ref, nu_ref, x_hbm, o_hbm, xbuf, obuf, xsem, osem, compute, prepare, tn=tn)


def _ffn1(blk_start, blk_count, n_used, xs, w1, b1):
    n_rows = xs.shape[0]
    bm, tn = MOE_BM, FFN1_TN
    nj = D_FF // tn
    b1r = b1.reshape(N_EXPERTS, 1, 2 * D_FF)
    grid_spec = pltpu.PrefetchScalarGridSpec(
        num_scalar_prefetch=3,
        grid=(nj, N_EXPERTS),
        in_specs=[
            pl.BlockSpec(memory_space=pl.ANY),
            pl.BlockSpec((1, D_MODEL, tn), lambda j, e, *_: (e, 0, j)),
            pl.BlockSpec((1, D_MODEL, tn), lambda j, e, *_: (e, 0, nj + j)),
            pl.BlockSpec((1, 1, tn), lambda j, e, *_: (e, 0, j)),
            pl.BlockSpec((1, 1, tn), lambda j, e, *_: (e, 0, nj + j)),
        ],
        out_specs=pl.BlockSpec(memory_space=pl.ANY),
        scratch_shapes=[
            pltpu.VMEM((D_MODEL, tn), BF16),
            pltpu.VMEM((D_MODEL, tn), BF16),
            pltpu.VMEM((2, bm, D_MODEL // 2), U32),
            pltpu.VMEM((2, bm, tn), BF16),
            pltpu.SemaphoreType.DMA((2,)),
            pltpu.SemaphoreType.DMA((2,)),
        ],
    )
    return pl.pallas_call(
        functools.partial(_ffn1_kernel, tn=tn),
        grid_spec=grid_spec,
        out_shape=jax.ShapeDtypeStruct((n_rows, D_FF), BF16),
        compiler_params=_cparams(("arbitrary", "arbitrary")),
        name="ffn1",
    )(blk_start, blk_count, n_used, xs, w1, w1, b1r, b1r)


def _ffn2_kernel(bs_ref, nb_ref, nu_ref, a_hbm, w_ref, b_ref, o_hbm, w_s, xbuf, obuf, xsem, osem, *, tn):
    j, e = pl.program_id(0), pl.program_id(1)

    def prepare():
        w_s[...] = w_ref[0].astype(BF16)

    def compute(a):
        return jnp.dot(a, w_s[...], preferred_element_type=F32) + b_ref[0]

    _expert_rows(e, j, bs_ref, nb_ref, nu_ref, a_hbm, o_hbm, xbuf, obuf, xsem, osem, compute, prepare, tn=tn)


def _ffn2(blk_start, blk_count, n_used, act, w2, b2):
    n_rows = act.shape[0]
    bm, tn = MOE_BM, FFN2_TN
    nj = D_MODEL // tn
    b2r = b2.reshape(N_EXPERTS, 1, D_MODEL)
    grid_spec = pltpu.PrefetchScalarGridSpec(
        num_scalar_prefetch=3,
        grid=(nj, N_EXPERTS),
        in_specs=[
            pl.BlockSpec(memory_space=pl.ANY),
            pl.BlockSpec((1, D_FF, tn), lambda j, e, *_: (e, 0, j)),
            pl.BlockSpec((1, 1, tn), lambda j, e, *_: (e, 0, j)),
        ],
        out_specs=pl.BlockSpec(memory_space=pl.ANY),
        scratch_shapes=[
            pltpu.VMEM((D_FF, tn), BF16),
            pltpu.VMEM((2, bm, D_FF), BF16),
            pltpu.VMEM((2, bm, tn), F32),
            pltpu.SemaphoreType.DMA((2,)),
            pltpu.SemaphoreType.DMA((2,)),
        ],
    )
    return pl.pallas_call(
        functools.partial(_ffn2_kernel, tn=tn),
        grid_spec=grid_spec,
        out_shape=jax.ShapeDtypeStruct((n_rows, D_MODEL), F32),
        compiler_params=_cparams(("arbitrary", "arbitrary")),
        name="ffn2",
    )(blk_start, blk_count, n_used, act, w2, b2r)


def _combine_kernel(pos_ref, ys_ref, route_ref, x1_ref, mod_ref, g_ref, o_ref, buf, sem):
    tm = x1_ref.shape[0]

    def row_copy(t, k):
        return pltpu.make_async_copy(
            ys_ref.at[pl.ds(pos_ref[t * TOP_K + k], 1), :], buf.at[k, pl.ds(t, 1), :], sem
        )

    def issue(t, carry):
        for k in range(TOP_K):
            row_copy(t, k).start()
        return carry

    def drain(t, carry):
        for k in range(TOP_K):
            row_copy(t, k).wait()
        return carry

    lax.fori_loop(0, tm, issue, 0)
    lax.fori_loop(0, tm, drain, 0)
    route = route_ref[...]
    y = route[:, 0:1] * buf[0]
    for k in range(1, TOP_K):
        y = y + route[:, k:k + 1] * buf[k]
    m = mod_ref[0]
    o_ref[...] = x1_ref[...] + m[5:6] * _rms(y, g_ref[...])


def _combine(pos_flat, ys, route, x1, mod3, g_post_ffn, seq):
    m_rows = x1.shape[0]
    tm = COMBINE_TM
    per_seq = seq // tm
    return pl.pallas_call(
        _combine_kernel,
        grid=(m_rows // tm,),
        in_specs=[
            pl.BlockSpec((tm * TOP_K,), lambda i: (i,), memory_space=pltpu.SMEM),
            pl.BlockSpec(memory_space=pl.ANY),
            pl.BlockSpec((tm, LANES), lambda i: (i, 0)),
            pl.BlockSpec((tm, D_MODEL), lambda i: (i, 0)),
            pl.BlockSpec((1, N_MOD, D_MODEL), lambda i: (i // per_seq, 0, 0)),
            pl.BlockSpec((1, D_MODEL), lambda i: (0, 0)),
        ],
        out_specs=pl.BlockSpec((tm, D_MODEL), lambda i: (i, 0)),
        out_shape=jax.ShapeDtypeStruct((m_rows, D_MODEL), F32),
        scratch_shapes=[pltpu.VMEM((TOP_K, tm, D_MODEL), F32), pltpu.SemaphoreType.DMA(())],
        compiler_params=_cparams(("arbitrary",)),
        name="combine",
    )(pos_flat, ys, route, x1, mod3, g_post_ffn.reshape(1, D_MODEL))


def _layer(x2, mod3, bsz, seq, g_pre_mix, g_post_mix, w_in, w_pool_grp, pool_scale, w_up_pool, w_up_attn, w_out,
           g_pre_ffn, g_post_ffn, w_router, b_router, w1, b1, w2, b2):
    m_rows = x2.shape[0]
    proj, small = _in_proj(x2, mod3, g_pre_mix, w_in, seq)
    y_attn = _attn(proj, small, bsz, seq)
    x1, hp, route, cnt = _merge(proj, y_attn, x2, mod3, w_pool_grp, pool_scale, w_up_pool, w_up_attn, w_out,
                                g_post_mix, g_pre_ffn, w_router, b_router, seq)

    bm = MOE_BM
    n_slots = m_rows * TOP_K
    n_blocks = -(-n_slots // bm) + N_EXPERTS
    idx = route[:, TOP_K:2 * TOP_K].astype(I32)
    rank = route[:, 2 * TOP_K:3 * TOP_K].astype(I32)
    counts = cnt[0, :N_EXPERTS].astype(I32)
    padded = (counts + bm - 1) // bm * bm
    pad_end = jnp.cumsum(padded)
    pad_start = pad_end - padded
    pos_flat = (pad_start[idx] + rank).reshape(-1)
    n_used = (pad_end[-1] // bm).astype(I32).reshape(1)
    blk_ids = jnp.arange(n_blocks, dtype=I32)
    blk_start = (pad_start // bm).astype(I32)
    blk_count = (padded // bm).astype(I32)
    last_of_expert = jnp.any((blk_ids[:, None] + 1) * bm == pad_end[None, :], axis=1)
    zero_flag = (last_of_expert | (blk_ids >= n_used[0])).astype(I32)

    xs = _dispatch(zero_flag, pos_flat, hp, n_blocks * bm)
    act = _ffn1(blk_start, blk_count, n_used, xs, w1, b1)
    ys = _ffn2(blk_start, blk_count, n_used, act, w2, b2)
    return _combine(pos_flat, ys, route, x1, mod3, g_post_ffn, seq)


def kernel(x, c, w_ada, b_ada, g_pre_mix, g_post_mix, w_in, w_pool_grp, pool_scale, w_up_pool, w_up_attn, w_out,
           g_pre_ffn, g_post_ffn, w_router, b_router, w1, b1, w2, b2):
    bsz, seq, d = x.shape
    assert d == D_MODEL
    depth = w_ada.shape[0]
    x2 = x.reshape(bsz * seq, d)
    for layer in range(depth):
        mod3 = _ada(c, w_ada[layer], b_ada[layer]).reshape(bsz, N_MOD, D_MODEL)
        x2 = _layer(x2, mod3, bsz, seq, g_pre_mix[layer], g_post_mix[layer], w_in[layer], w_pool_grp[layer],
                    pool_scale[layer], w_up_pool[layer], w_up_attn[layer], w_out[layer], g_pre_ffn[layer],
                    g_post_ffn[layer], w_router[layer], b_router[layer], w1[layer], b1[layer], w2[layer], b2[layer])
    return x2.reshape(bsz, seq, d)
```

```python
import functools

import jax
import jax.numpy as jnp
import numpy as np
from jax import lax
from jax.experimental import pallas as pl
from jax.experimental.pallas import tpu as pltpu

F32 = jnp.float32
BF16 = jnp.bfloat16
I32 = jnp.int32
U32 = jnp.uint32

D_MODEL = 2048
POOL_WINDOWS = (2, 4, 8, 16)
POOL_GROUPS = 4
POOL_GROUP_DIM = D_MODEL // 8
POOL_WIDTH = POOL_GROUPS * POOL_GROUP_DIM
N_HEADS = 16
N_KV_HEADS = 4
HEAD_DIM = 128
ATTN_WIDTH = N_HEADS * HEAD_DIM
KV_WIDTH = N_KV_HEADS * HEAD_DIM
ROT_DIM = HEAD_DIM // 4
IDX_HEADS = 16
IDX_DIM = 64
IDX_ROT_DIM = IDX_DIM // 4
TOPK_MAX = 256
ROPE_THETA = 500000.0
N_BRANCHES = 2
N_EXPERTS = 32
TOP_K = 4
D_FF = D_MODEL
SWIGLU_ALPHA = 1.702
SWIGLU_LIMIT = 7.0
N_MOD = 6
EPS = 1e-6

LANES = 128
INT_MIN = np.int32(-(2 ** 31))

GL_OFF = 0
Q_OFF = GL_OFF + N_BRANCHES * D_MODEL
U_OFF = Q_OFF + ATTN_WIDTH
QI_OFF = U_OFF + POOL_WIDTH
K_OFF = QI_OFF + IDX_HEADS * IDX_DIM
V_OFF = K_OFF + KV_WIDTH
PROJ_WIDTH = V_OFF + KV_WIDTH

IN_TN = 1024
ATTN_TQ = 256
ATTN_KC = 512
MASK_NEG = -1e30
MERGE_TM = 256
MOE_BM = 512
FFN1_TN = 512
FFN2_TN = 1024
DISPATCH_TM = 256
COMBINE_TM = 256
FFN_X_SLOTS = 3
VMEM_LIMIT = 56 * 1024 * 1024


def _cparams(sem):
    return pltpu.CompilerParams(dimension_semantics=sem, vmem_limit_bytes=VMEM_LIMIT)


def _rms(x, g):
    return x * lax.rsqrt(jnp.mean(x * x, axis=-1, keepdims=True) + EPS) * g


def _pack_bf16_pairs(x):
    n = x.shape[1] // 2
    bits = lax.bitcast_convert_type(x.astype(BF16).astype(F32), U32)
    return bits[:, n:] | lax.shift_right_logical(bits[:, :n], np.uint32(16))


def _unpack_bf16_pairs(w):
    lo = lax.bitcast_convert_type(lax.shift_left(w, np.uint32(16)), F32)
    hi = lax.bitcast_convert_type(w & np.uint32(0xFFFF0000), F32)
    return lo, hi


def _ada_kernel(c_ref, w_ref, b_ref, o_ref):
    c = c_ref[...]
    ca = (c * jax.nn.sigmoid(c)).astype(BF16)
    o_ref[...] = jnp.dot(ca, w_ref[...].astype(BF16), preferred_element_type=F32) + b_ref[...]


def _ada(c, w_ada, b_ada):
    bsz = c.shape[0]
    n = w_ada.shape[1]
    tn = 1024
    return pl.pallas_call(
        _ada_kernel,
        grid=(n // tn,),
        in_specs=[
            pl.BlockSpec((bsz, D_MODEL), lambda j: (0, 0)),
            pl.BlockSpec((D_MODEL, tn), lambda j: (0, j)),
            pl.BlockSpec((1, tn), lambda j: (0, j)),
        ],
        out_specs=pl.BlockSpec((bsz, tn), lambda j: (0, j)),
        out_shape=jax.ShapeDtypeStruct((bsz, n), F32),
        compiler_params=_cparams(("arbitrary",)),
        name="ada",
    )(c, w_ada, b_ada.reshape(1, n))


def _rope_slice(xs, c, s1, s2, shift):
    return xs * c + pltpu.roll(xs, LANES - shift, 1) * s1 + pltpu.roll(xs, shift, 1) * s2


def _in_kernel(x_ref, mod_ref, g_ref, w_ref, ws_ref, tab_ref, o_ref, os_ref, h_ref, acc_ref):
    j = pl.program_id(1)
    half_a = ROT_DIM // 2
    half_i = IDX_ROT_DIM // 2

    @pl.when(j == 0)
    def _():
        m = mod_ref[0]
        h = _rms(x_ref[...], g_ref[...]) * (1.0 + m[1:2]) + m[0:1]
        hb = h.astype(BF16)
        h_ref[...] = hb
        small = jnp.dot(hb, ws_ref[...], preferred_element_type=F32)
        os_ref[...] = _rope_slice(small, tab_ref[6], tab_ref[7], tab_ref[8], half_i)

    acc_ref[...] = jnp.dot(h_ref[...], w_ref[...], preferred_element_type=F32)
    n_sl = IN_TN // LANES
    j_q0, j_q1 = Q_OFF // IN_TN, U_OFF // IN_TN
    j_qi = QI_OFF // IN_TN
    j_kv = K_OFF // IN_TN
    k_sl = KV_WIDTH // LANES

    @pl.when((j < j_q0) | (j == U_OFF // IN_TN))
    def _():
        o_ref[...] = acc_ref[...].astype(BF16)

    @pl.when((j >= j_q0) & (j < j_q1))
    def _():
        for s in range(n_sl):
            sl = slice(s * LANES, (s + 1) * LANES)
            o_ref[:, sl] = _rope_slice(acc_ref[:, sl], tab_ref[0], tab_ref[1], tab_ref[2], half_a).astype(BF16)

    @pl.when(j == j_qi)
    def _():
        for s in range(n_sl):
            sl = slice(s * LANES, (s + 1) * LANES)
            o_ref[:, sl] = _rope_slice(acc_ref[:, sl], tab_ref[3], tab_ref[4], tab_ref[5], half_i).astype(BF16)

    @pl.when(j == j_kv)
    def _():
        for s in range(k_sl):
            sl = slice(s * LANES, (s + 1) * LANES)
            o_ref[:, sl] = _rope_slice(acc_ref[:, sl], tab_ref[0], tab_ref[1], tab_ref[2], half_a).astype(BF16)
        o_ref[:, KV_WIDTH:] = acc_ref[:, KV_WIDTH:].astype(BF16)


def _rope_tables(seq):
    def tabs(rot_dim, period, n_rep):
        half = rot_dim // 2
        inv = np.float32(ROPE_THETA) ** (-np.arange(0, rot_dim, 2, dtype=np.float32) / np.float32(rot_dim))
        ang = np.arange(seq, dtype=np.float32)[:, None] * inv.astype(np.float32)[None, :]
        cos, sin = np.cos(ang).astype(np.float32), np.sin(ang).astype(np.float32)
        ones = np.ones((seq, period - 2 * half), np.float32)
        z_h = np.zeros((seq, half), np.float32)
        c = np.concatenate([cos, cos, ones], axis=1)
        s1 = np.concatenate([-sin, z_h, 0 * ones], axis=1)
        s2 = np.concatenate([z_h, sin, 0 * ones], axis=1)
        return [np.tile(t, (1, n_rep)) for t in (c, s1, s2)]

    ta = tabs(ROT_DIM, HEAD_DIM, LANES // HEAD_DIM)
    ti = tabs(IDX_ROT_DIM, IDX_DIM, LANES // IDX_DIM)
    ts = tabs(IDX_ROT_DIM, IDX_DIM, 1)
    wi_scale = np.float32(IDX_HEADS ** -0.5 * IDX_DIM ** -0.5)
    pad = LANES - IDX_DIM
    c_tail = np.where(np.arange(pad) < IDX_HEADS, wi_scale, np.float32(0.0)).astype(np.float32)
    ts = [
        np.concatenate([ts[0], np.broadcast_to(c_tail, (seq, pad))], axis=1),
        np.concatenate([ts[1], np.zeros((seq, pad), np.float32)], axis=1),
        np.concatenate([ts[2], np.zeros((seq, pad), np.float32)], axis=1),
    ]
    return jnp.asarray(np.stack(ta + ti + ts, axis=0))


def _in_proj(x2, mod3, g_pre, w_in, seq):
    m_rows = x2.shape[0]
    tm = min(1024, seq)
    assert seq % tm == 0 and m_rows % tm == 0
    per_seq = seq // tm
    offs = np.cumsum((POOL_WIDTH, ATTN_WIDTH, KV_WIDTH, KV_WIDTH, IDX_HEADS * IDX_DIM, IDX_DIM, IDX_HEADS))
    w_u, w_q, w_k, w_v, w_qi, w_ki, w_wi, w_gl = jnp.split(w_in, [int(o) for o in offs], axis=1)
    w_main = jnp.concatenate([w_gl, w_q, w_u, w_qi, w_k, w_v], axis=1).astype(BF16)
    w_small = jnp.concatenate(
        [w_ki, w_wi, jnp.zeros((D_MODEL, LANES - IDX_DIM - IDX_HEADS), F32)], axis=1
    ).astype(BF16)
    tabs = _rope_tables(seq)
    n_j = PROJ_WIDTH // IN_TN
    return pl.pallas_call(
        _in_kernel,
        grid=(m_rows // tm, n_j),
        in_specs=[
            pl.BlockSpec((tm, D_MODEL), lambda i, j: (i, 0)),
            pl.BlockSpec((1, N_MOD, D_MODEL), lambda i, j: (i // per_seq, 0, 0)),
            pl.BlockSpec((1, D_MODEL), lambda i, j: (0, 0)),
            pl.BlockSpec((D_MODEL, IN_TN), lambda i, j: (0, j)),
            pl.BlockSpec((D_MODEL, LANES), lambda i, j: (0, 0)),
            pl.BlockSpec((9, tm, LANES), lambda i, j: (0, i % per_seq, 0)),
        ],
        out_specs=[
            pl.BlockSpec((tm, IN_TN), lambda i, j: (i, j)),
            pl.BlockSpec((tm, LANES), lambda i, j: (i, 0)),
        ],
        out_shape=[
            jax.ShapeDtypeStruct((m_rows, PROJ_WIDTH), BF16),
            jax.ShapeDtypeStruct((m_rows, LANES), F32),
        ],
        scratch_shapes=[pltpu.VMEM((tm, D_MODEL), BF16), pltpu.VMEM((tm, IN_TN), F32)],
        compiler_params=_cparams(("arbitrary", "arbitrary")),
        name="in_proj",
    )(x2, mod3, g_pre.reshape(1, D_MODEL), w_main, w_small, tabs)


def _attn_kernel(q_ref, qi_ref, k_ref, v_ref, kis_ref, wis_ref, o_ref,
                 key_ref, plim_ref, qs_ref, m_ref, l_ref, acc_ref, *, tq, kc, seq, n_keep):
    i = pl.program_id(1)
    nk = ((i + 1) * tq + kc - 1) // kc
    nt = (((1,), (1,)), ((), ()))
    grp = N_HEADS // N_KV_HEADS
    row_io = lax.broadcasted_iota(I32, (kc, tq), 0)
    tpos = i * tq + lax.broadcasted_iota(I32, (kc, tq), 1)
    w_t = wis_ref[...].T

    def idx_chunk(c, carry):
        r0 = pl.multiple_of(c * kc, kc)
        ki = kis_ref[pl.ds(r0, kc), :IDX_DIM].astype(BF16)
        sc = jnp.zeros((kc, tq), F32)
        for h in range(IDX_HEADS):
            qh = qi_ref[:, h * IDX_DIM:(h + 1) * IDX_DIM]
            d = lax.dot_general(ki, qh, nt, preferred_element_type=F32)
            sc = sc + jnp.maximum(d, 0.0) * w_t[IDX_DIM + h:IDX_DIM + h + 1, :]
        bits = lax.bitcast_convert_type(sc, I32)
        key = jnp.where(bits < 0, bits ^ np.int32(0x7FFFFFFF), bits)
        key_ref[pl.ds(r0, kc), :] = jnp.where(r0 + row_io <= tpos, key, INT_MIN)
        return carry

    lax.fori_loop(0, nk, idx_chunk, 0)

    def count(pred):
        def body(c, acc):
            r0 = pl.multiple_of(c * kc, kc)
            hit = pred(key_ref[pl.ds(r0, kc), :], r0)
            return acc + jnp.sum(hit.astype(I32), axis=0, keepdims=True)

        return lax.fori_loop(0, nk, body, jnp.zeros((1, tq), I32))

    def bit_step(it, tu):
        cand = tu | lax.shift_left(np.int32(1), 31 - it)
        cnt = count(lambda kch, r0: kch >= (cand ^ INT_MIN))
        return jnp.where(cnt >= n_keep, cand, tu)

    tu = lax.fori_loop(0, 32, bit_step, jnp.zeros((1, tq), I32))
    thr = tu ^ INT_MIN
    cnt_gt = count(lambda kch, r0: kch > thr)
    cnt_ge = count(lambda kch, r0: kch >= thr)
    tie_i = ((cnt_ge > n_keep) & (tu != 0)).astype(I32)
    need = n_keep - cnt_gt

    plim_ref[...] = jnp.full(plim_ref.shape, seq, I32)

    @pl.when(jnp.max(tie_i) > 0)
    def _():
        n_bits = (seq - 1).bit_length()

        def idx_step(it, p):
            cand = p | lax.shift_left(np.int32(1), (n_bits - 1) - it)
            below = count(lambda kch, r0: (kch == thr) & (r0 + row_io < cand))
            return jnp.where(below < need, cand, p)

        p = lax.fori_loop(0, n_bits, idx_step, jnp.zeros((1, tq), I32))
        plim_ref[...] = jnp.broadcast_to(jnp.where(tie_i > 0, p, seq), plim_ref.shape)

    for g in range(N_KV_HEADS):
        for r in range(grp):
            hh = g * grp + r
            qs_ref[g, r * tq:(r + 1) * tq, :] = q_ref[:, hh * HEAD_DIM:(hh + 1) * HEAD_DIM]
    m_ref[...] = jnp.full(m_ref.shape, MASK_NEG, F32)
    l_ref[...] = jnp.zeros_like(l_ref)
    acc_ref[...] = jnp.zeros_like(acc_ref)
    c_exp = HEAD_DIM ** -0.5 * np.log2(np.e)

    def att_chunk(c, carry):
        r0 = pl.multiple_of(c * kc, kc)
        kch = key_ref[pl.ds(r0, kc), :]
        row = r0 + row_io
        sel = ((kch > thr) | ((kch == thr) & (row <= plim_ref[0:1, :]))) & (row <= tpos)
        bias_t = jnp.where(sel, 0.0, MASK_NEG).astype(F32).T
        for g in range(N_KV_HEADS):
            kg = k_ref[pl.ds(r0, kc), g * HEAD_DIM:(g + 1) * HEAD_DIM]
            vg = v_ref[pl.ds(r0, kc), g * HEAD_DIM:(g + 1) * HEAD_DIM]
            s = lax.dot_general(qs_ref[g], kg, nt, preferred_element_type=F32)
            s = (s.reshape(grp, tq, kc) + bias_t[None]).reshape(grp * tq, kc)
            m_old = m_ref[g]
            m_new = jnp.maximum(m_old, jnp.max(s, axis=-1, keepdims=True))
            alpha = jnp.exp2((m_old - m_new) * c_exp)
            p = jnp.exp2((s - m_new) * c_exp)
            l_ref[g] = alpha * l_ref[g] + jnp.sum(p, axis=-1, keepdims=True)
            acc_ref[g] = alpha * acc_ref[g] + jnp.dot(p.astype(BF16), vg, preferred_element_type=F32)
            m_ref[g] = m_new
        return carry

    lax.fori_loop(0, nk, att_chunk, 0)
    for g in range(N_KV_HEADS):
        o = acc_ref[g] / l_ref[g]
        for r in range(grp):
            hh = g * grp + r
            o_ref[:, hh * HEAD_DIM:(hh + 1) * HEAD_DIM] = o[r * tq:(r + 1) * tq, :].astype(BF16)


def _attn(proj, small, bsz, seq):
    tq = min(ATTN_TQ, seq)
    kc = min(ATTN_KC, seq)
    assert seq % tq == 0 and seq % kc == 0 and kc % tq == 0
    nq = seq // tq
    n_keep = min(TOPK_MAX, seq // 4)
    grp = N_HEADS // N_KV_HEADS
    kern = functools.partial(_attn_kernel, tq=tq, kc=kc, seq=seq, n_keep=n_keep)
    return pl.pallas_call(
        kern,
        grid=(bsz, nq),
        in_specs=[
            pl.BlockSpec((tq, ATTN_WIDTH), lambda b, i: (b * nq + i, Q_OFF // ATTN_WIDTH)),
            pl.BlockSpec((tq, IDX_HEADS * IDX_DIM), lambda b, i: (b * nq + i, QI_OFF // (IDX_HEADS * IDX_DIM))),
            pl.BlockSpec((seq, KV_WIDTH), lambda b, i: (b, K_OFF // KV_WIDTH)),
            pl.BlockSpec((seq, KV_WIDTH), lambda b, i: (b, V_OFF // KV_WIDTH)),
            pl.BlockSpec((seq, LANES), lambda b, i: (b, 0)),
            pl.BlockSpec((tq, LANES), lambda b, i: (b * nq + i, 0)),
        ],
        out_specs=pl.BlockSpec((tq, ATTN_WIDTH), lambda b, i: (b * nq + i, 0)),
        out_shape=jax.ShapeDtypeStruct((bsz * seq, ATTN_WIDTH), BF16),
        scratch_shapes=[
            pltpu.VMEM((seq, tq), I32),
            pltpu.VMEM((8, tq), I32),
            pltpu.VMEM((N_KV_HEADS, grp * tq, HEAD_DIM), BF16),
            pltpu.VMEM((N_KV_HEADS, grp * tq, 1), F32),
            pltpu.VMEM((N_KV_HEADS, grp * tq, 1), F32),
            pltpu.VMEM((N_KV_HEADS, grp * tq, HEAD_DIM), F32),
        ],
        compiler_params=_cparams(("arbitrary", "arbitrary")),
        name="attn",
    )(proj, proj, proj, proj, small, small)


def _merge_kernel(
    u_ref, halo_ref, ya_ref, gl_ref, x_ref, mod_ref, wgrp_ref, pscale_ref, wup_p_ref, wup_a_ref, wout_ref,
    gpost_ref, gffn_ref, wr_ref, br_ref,
    x1_ref, hp_ref, route_ref, cnt_ref,
    carry_ref, *, seq,
):
    tm = MERGE_TM
    i = pl.program_id(0)
    pos0 = (i * tm) % seq

    @pl.when(i == 0)
    def _():
        carry_ref[...] = jnp.zeros_like(carry_ref)

    halo_rows = halo_ref.shape[0]
    halo = jnp.where(pos0 == 0, 0.0, halo_ref[...].astype(F32))
    ext = jnp.concatenate([halo, u_ref[...].astype(F32)], axis=0)
    pos = pos0 + lax.broadcasted_iota(I32, (tm, 1), 0)
    ys = []
    for g, win in enumerate(POOL_WINDOWS):
        sl = slice(g * POOL_GROUP_DIM, (g + 1) * POOL_GROUP_DIM)
        e = ext[:, sl]
        acc = e
        span = 1
        while span < win:
            acc = acc + pltpu.roll(acc, span, 0)
            span *= 2
        cnt = jnp.minimum(pos + 1, win).astype(F32)
        mixed = (acc[halo_rows:] / cnt - e[halo_rows:]).astype(BF16)
        ys.append(jnp.dot(mixed, wgrp_ref[g], preferred_element_type=F32))
    y_pool = (jnp.concatenate(ys, axis=1) * pscale_ref[...]).astype(BF16)

    gates = jax.nn.sigmoid(gl_ref[...].astype(F32))
    up_p = jnp.dot(y_pool, wup_p_ref[...], preferred_element_type=F32)
    up_a = jnp.dot(ya_ref[...], wup_a_ref[...], preferred_element_type=F32)
    merged = (gates[:, :D_MODEL] * up_p + gates[:, D_MODEL:] * up_a).astype(BF16)
    y = jnp.dot(merged, wout_ref[...], preferred_element_type=F32)
    m = mod_ref[0]
    x1 = x_ref[...] + m[2:3] * _rms(y, gpost_ref[...])
    x1_ref[...] = x1

    h2f = _rms(x1, gffn_ref[...]) * (1.0 + m[4:5]) + m[3:4]
    h2 = h2f.astype(BF16)
    hp_ref[...] = _pack_bf16_pairs(h2f)

    lane = lax.broadcasted_iota(I32, (tm, LANES), 1)
    logits = jnp.dot(h2, wr_ref[...], preferred_element_type=F32) + br_ref[...]
    work = jnp.where(lane < N_EXPERTS, logits, -jnp.inf)
    vals, hots = [], []
    for _ in range(TOP_K):
        mx = jnp.max(work, axis=-1, keepdims=True)
        first = jnp.min(jnp.where(work == mx, lane, LANES), axis=-1, keepdims=True)
        hot = lane == first
        vals.append(mx)
        hots.append(hot)
        work = jnp.where(hot, -jnp.inf, work)
    ex = [jnp.exp(v - vals[0]) for v in vals]
    den = ex[0] + ex[1] + ex[2] + ex[3]

    onehot = jnp.zeros((tm, LANES), F32)
    for hot in hots:
        onehot = onehot + hot.astype(F32)
    r_i = lax.broadcasted_iota(I32, (tm, tm), 0)
    c_i = lax.broadcasted_iota(I32, (tm, tm), 1)
    tril = (c_i < r_i).astype(BF16)
    before = jnp.dot(tril, onehot.astype(BF16), preferred_element_type=F32) + carry_ref[0:1, :]
    out = jnp.zeros((tm, LANES), F32)
    lane_f = lane.astype(F32)
    for k in range(TOP_K):
        out = jnp.where(lane == k, ex[k] / den, out)
        idx_k = jnp.sum(jnp.where(hots[k], lane_f, 0.0), axis=-1, keepdims=True)
        out = jnp.where(lane == TOP_K + k, idx_k, out)
        rank_k = jnp.sum(jnp.where(hots[k], before, 0.0), axis=-1, keepdims=True)
        out = jnp.where(lane == 2 * TOP_K + k, rank_k, out)
    route_ref[...] = out
    total = carry_ref[0:1, :] + jnp.sum(onehot, axis=0, keepdims=True)
    carry_ref[...] = jnp.broadcast_to(total, carry_ref.shape)
    cnt_ref[...] = jnp.broadcast_to(total, cnt_ref.shape)


def _merge(proj, y_attn, x2, mod3, w_pool_grp, pool_scale, w_up_pool, w_up_attn, w_out, g_post_mix, g_pre_ffn,
           w_router, b_router, seq):
    m_rows = x2.shape[0]
    tm = MERGE_TM
    halo = 16
    assert seq % tm == 0 and max(POOL_WINDOWS) <= halo
    per_seq = seq // tm
    wr = jnp.concatenate([w_router, jnp.zeros((D_MODEL, LANES - N_EXPERTS), F32)], axis=1).astype(BF16)
    br = jnp.concatenate([b_router, jnp.zeros((LANES - N_EXPERTS,), F32)]).reshape(1, LANES)
    const = lambda shape: pl.BlockSpec(shape, lambda i: (0,) * len(shape), pipeline_mode=pl.Buffered(1))
    kern = functools.partial(_merge_kernel, seq=seq)
    return pl.pallas_call(
        kern,
        grid=(m_rows // tm,),
        in_specs=[
            pl.BlockSpec((tm, POOL_WIDTH), lambda i: (i, U_OFF // POOL_WIDTH)),
            pl.BlockSpec((halo, POOL_WIDTH), lambda i: (jnp.maximum(i * (tm // halo) - 1, 0), U_OFF // POOL_WIDTH)),
            pl.BlockSpec((tm, ATTN_WIDTH), lambda i: (i, 0)),
            pl.BlockSpec((tm, N_BRANCHES * D_MODEL), lambda i: (i, 0)),
            pl.BlockSpec((tm, D_MODEL), lambda i: (i, 0)),
            pl.BlockSpec((1, N_MOD, D_MODEL), lambda i: (i // per_seq, 0, 0)),
            const((POOL_GROUPS, POOL_GROUP_DIM, POOL_GROUP_DIM)),
            const((1, POOL_WIDTH)),
            const((POOL_WIDTH, D_MODEL)),
            const((ATTN_WIDTH, D_MODEL)),
            const((D_MODEL, D_MODEL)),
            const((1, D_MODEL)),
            const((1, D_MODEL)),
            const((D_MODEL, LANES)),
            const((1, LANES)),
        ],
        out_specs=[
            pl.BlockSpec((tm, D_MODEL), lambda i: (i, 0)),
            pl.BlockSpec((tm, D_MODEL // 2), lambda i: (i, 0)),
            pl.BlockSpec((tm, LANES), lambda i: (i, 0)),
            pl.BlockSpec((8, LANES), lambda i: (0, 0)),
        ],
        out_shape=[
            jax.ShapeDtypeStruct((m_rows, D_MODEL), F32),
            jax.ShapeDtypeStruct((m_rows, D_MODEL // 2), U32),
            jax.ShapeDtypeStruct((m_rows, LANES), F32),
            jax.ShapeDtypeStruct((8, LANES), F32),
        ],
        scratch_shapes=[pltpu.VMEM((8, LANES), F32)],
        compiler_params=_cparams(("arbitrary",)),
        name="merge",
    )(
        proj, proj, y_attn, proj, x2, mod3,
        w_pool_grp.astype(BF16), pool_scale.reshape(1, POOL_WIDTH), w_up_pool.astype(BF16),
        w_up_attn.astype(BF16), w_out.astype(BF16), g_post_mix.reshape(1, D_MODEL), g_pre_ffn.reshape(1, D_MODEL),
        wr, br,
    )


def _dispatch_kernel(zf_ref, pos_ref, h_ref, xs_ref, zbuf, sem, zsem):
    tm = h_ref.shape[0]
    bm = zbuf.shape[0]

    @pl.when(pl.program_id(0) == 0)
    def _():
        zbuf[...] = jnp.zeros_like(zbuf)

        def zero_copy(b):
            return pltpu.make_async_copy(zbuf, xs_ref.at[pl.ds(b * bm, bm), :], zsem)

        def z_issue(b, carry):
            pl.when(zf_ref[b] != 0)(lambda: zero_copy(b).start())
            return carry

        def z_drain(b, carry):
            pl.when(zf_ref[b] != 0)(lambda: zero_copy(b).wait())
            return carry

        lax.fori_loop(0, zf_ref.shape[0], z_issue, 0)
        lax.fori_loop(0, zf_ref.shape[0], z_drain, 0)

    def row_copy(t, k):
        return pltpu.make_async_copy(
            h_ref.at[pl.ds(t, 1), :], xs_ref.at[pl.ds(pos_ref[t * TOP_K + k], 1), :], sem
        )

    def issue(t, carry):
        for k in range(TOP_K):
            row_copy(t, k).start()
        return carry

    def drain(t, carry):
        for k in range(TOP_K):
            row_copy(t, k).wait()
        return carry

    lax.fori_loop(0, tm, issue, 0)
    lax.fori_loop(0, tm, drain, 0)


def _dispatch(zero_flag, pos_flat, hp, n_rows):
    m_rows, width = hp.shape
    tm = DISPATCH_TM
    grid_spec = pltpu.PrefetchScalarGridSpec(
        num_scalar_prefetch=1,
        grid=(m_rows // tm,),
        in_specs=[
            pl.BlockSpec((tm * TOP_K,), lambda i, zf: (i,), memory_space=pltpu.SMEM),
            pl.BlockSpec((tm, width), lambda i, zf: (i, 0)),
        ],
        out_specs=pl.BlockSpec(memory_space=pl.ANY),
        scratch_shapes=[pltpu.VMEM((MOE_BM, width), U32), pltpu.SemaphoreType.DMA(()), pltpu.SemaphoreType.DMA(())],
    )
    return pl.pallas_call(
        _dispatch_kernel,
        grid_spec=grid_spec,
        out_shape=jax.ShapeDtypeStruct((n_rows, width), U32),
        compiler_params=_cparams(("arbitrary",)),
        name="dispatch",
    )(zero_flag, pos_flat, hp)


def _expert_rows(e, j, bs_ref, nb_ref, nu_ref, x_hbm, o_hbm, xbuf, obuf, xsem, osem, compute, prepare):
    n_x, bm = xbuf.shape[0], xbuf.shape[1]
    tw = obuf.shape[2]
    n_blocks = x_hbm.shape[0] // bm
    b0 = bs_ref[e]
    nb = nb_ref[e]
    n_used = nu_ref[0]

    def x_copy(blk, slot):
        return pltpu.make_async_copy(x_hbm.at[pl.ds(blk * bm, bm), :], xbuf.at[slot], xsem.at[slot])

    def o_copy(blk, slot):
        return pltpu.make_async_copy(obuf.at[slot], o_hbm.at[pl.ds(blk * bm, bm), pl.ds(j * tw, tw)], osem.at[slot])

    @pl.when(nb > 0)
    def _():
        prepare()

        def body(rb, carry):
            g = b0 + rb
            slot = g % 2
            ahead = n_x - 1

            @pl.when(g == 0)
            def _():
                for a in range(ahead):
                    pl.when(a < n_used)(lambda a=a: x_copy(a, a).start())

            x_copy(g, g % n_x).wait()

            @pl.when(g + ahead < n_used)
            def _():
                x_copy(g + ahead, (g + ahead) % n_x).start()

            @pl.when(g >= 2)
            def _():
                o_copy(g - 2, slot).wait()

            obuf[slot] = compute(xbuf[g % n_x])
            o_copy(g, slot).start()

            @pl.when(g == n_used - 1)
            def _():
                @pl.when(g >= 1)
                def _():
                    o_copy(g - 1, 1 - slot).wait()

                o_copy(g, slot).wait()

            return carry

        lax.fori_loop(0, nb, body, 0)

    @pl.when(e == N_EXPERTS - 1)
    def _():
        obuf[0] = jnp.zeros(obuf.shape[1:], obuf.dtype)

        def z_issue(blk, carry):
            o_copy(blk, 0).start()
            return carry

        def z_drain(blk, carry):
            o_copy(blk, 0).wait()
            return carry

        lax.fori_loop(nu_ref[0], n_blocks, z_issue, 0)
        lax.fori_loop(nu_ref[0], n_blocks, z_drain, 0)


def _ffn1_kernel(bs_ref, nb_ref, nu_ref, x_hbm, wg_ref, wl_ref, bg_ref, bl_ref, o_hbm,
                 wg_s, wl_s, xbuf, obuf, xsem, osem):
    j, e = pl.program_id(0), pl.program_id(1)
    half = D_MODEL // 2

    def prepare():
        wg_s[...] = wg_ref[0].astype(BF16)
        wl_s[...] = wl_ref[0].astype(BF16)

    def compute(xp):
        lo, hi = (v.astype(BF16) for v in _unpack_bf16_pairs(xp))

        def proj(w_s, b_ref):
            return (
                jnp.dot(lo, w_s[:half, :], preferred_element_type=F32)
                + jnp.dot(hi, w_s[half:, :], preferred_element_type=F32)
                + b_ref[0]
            )

        glu = jnp.minimum(proj(wg_s, bg_ref), SWIGLU_LIMIT)
        lin = jnp.clip(proj(wl_s, bl_ref), -SWIGLU_LIMIT, SWIGLU_LIMIT)
        return (glu * jax.nn.sigmoid(SWIGLU_ALPHA * glu) * (lin + 1.0)).astype(BF16)

    _expert_rows(e, j, bs_ref, nb_ref, nu_ref, x_hbm, o_hbm, xbuf, obuf, xsem, osem, compute, prepare)


def _ffn1(blk_start, blk_count, n_used, xs, w1, b1):
    n_rows = xs.shape[0]
    bm, tn = MOE_BM, FFN1_TN
    nj = D_FF // tn
    b1r = b1.reshape(N_EXPERTS, 1, 2 * D_FF)
    grid_spec = pltpu.PrefetchScalarGridSpec(
        num_scalar_prefetch=3,
        grid=(nj, N_EXPERTS),
        in_specs=[
            pl.BlockSpec(memory_space=pl.ANY),
            pl.BlockSpec((1, D_MODEL, tn), lambda j, e, *_: (e, 0, j)),
            pl.BlockSpec((1, D_MODEL, tn), lambda j, e, *_: (e, 0, nj + j)),
            pl.BlockSpec((1, 1, tn), lambda j, e, *_: (e, 0, j)),
            pl.BlockSpec((1, 1, tn), lambda j, e, *_: (e, 0, nj + j)),
        ],
        out_specs=pl.BlockSpec(memory_space=pl.ANY),
        scratch_shapes=[
            pltpu.VMEM((D_MODEL, tn), BF16),
            pltpu.VMEM((D_MODEL, tn), BF16),
            pltpu.VMEM((FFN_X_SLOTS, bm, D_MODEL // 2), U32),
            pltpu.VMEM((2, bm, tn), BF16),
            pltpu.SemaphoreType.DMA((FFN_X_SLOTS,)),
            pltpu.SemaphoreType.DMA((2,)),
        ],
    )
    return pl.pallas_call(
        _ffn1_kernel,
        grid_spec=grid_spec,
        out_shape=jax.ShapeDtypeStruct((n_rows, D_FF), BF16),
        compiler_params=_cparams(("arbitrary", "arbitrary")),
        name="ffn1",
    )(blk_start, blk_count, n_used, xs, w1, w1, b1r, b1r)


def _ffn2_kernel(bs_ref, nb_ref, nu_ref, a_hbm, w_ref, b_ref, o_hbm, w_s, xbuf, obuf, xsem, osem):
    j, e = pl.program_id(0), pl.program_id(1)

    def prepare():
        w_s[...] = w_ref[0].astype(BF16)

    def compute(a):
        return _pack_bf16_pairs(jnp.dot(a, w_s[...], preferred_element_type=F32) + b_ref[0])

    _expert_rows(e, j, bs_ref, nb_ref, nu_ref, a_hbm, o_hbm, xbuf, obuf, xsem, osem, compute, prepare)


def _ffn2(blk_start, blk_count, n_used, act, w2, b2):
    n_rows = act.shape[0]
    bm, tn = MOE_BM, FFN2_TN
    nj = D_MODEL // tn
    b2r = b2.reshape(N_EXPERTS, 1, D_MODEL)
    grid_spec = pltpu.PrefetchScalarGridSpec(
        num_scalar_prefetch=3,
        grid=(nj, N_EXPERTS),
        in_specs=[
            pl.BlockSpec(memory_space=pl.ANY),
            pl.BlockSpec((1, D_FF, tn), lambda j, e, *_: (e, 0, j)),
            pl.BlockSpec((1, 1, tn), lambda j, e, *_: (e, 0, j)),
        ],
        out_specs=pl.BlockSpec(memory_space=pl.ANY),
        scratch_shapes=[
            pltpu.VMEM((D_FF, tn), BF16),
            pltpu.VMEM((FFN_X_SLOTS, bm, D_FF), BF16),
            pltpu.VMEM((2, bm, tn // 2), U32),
            pltpu.SemaphoreType.DMA((FFN_X_SLOTS,)),
            pltpu.SemaphoreType.DMA((2,)),
        ],
    )
    return pl.pallas_call(
        _ffn2_kernel,
        grid_spec=grid_spec,
        out_shape=jax.ShapeDtypeStruct((n_rows, D_MODEL // 2), U32),
        compiler_params=_cparams(("arbitrary", "arbitrary")),
        name="ffn2",
    )(blk_start, blk_count, n_used, act, w2, b2r)


def _combine_kernel(pos_ref, ys_ref, route_ref, x1_ref, mod_ref, g_ref, o_ref, buf, sem):
    tm = x1_ref.shape[0]

    def row_copy(t, k):
        return pltpu.make_async_copy(
            ys_ref.at[pl.ds(pos_ref[t * TOP_K + k], 1), :], buf.at[k, pl.ds(t, 1), :], sem
        )

    def issue(t, carry):
        for k in range(TOP_K):
            row_copy(t, k).start()
        return carry

    def drain(t, carry):
        for k in range(TOP_K):
            row_copy(t, k).wait()
        return carry

    lax.fori_loop(0, tm, issue, 0)
    lax.fori_loop(0, tm, drain, 0)
    route = route_ref[...]
    y_lo, y_hi = (route[:, 0:1] * v for v in _unpack_bf16_pairs(buf[0]))
    for k in range(1, TOP_K):
        lo, hi = _unpack_bf16_pairs(buf[k])
        y_lo = y_lo + route[:, k:k + 1] * lo
        y_hi = y_hi + route[:, k:k + 1] * hi
    hw = FFN2_TN // 2
    parts = []
    for t in range(D_MODEL // FFN2_TN):
        parts += [y_lo[:, t * hw:(t + 1) * hw], y_hi[:, t * hw:(t + 1) * hw]]
    y = jnp.concatenate(parts, axis=1)
    m = mod_ref[0]
    o_ref[...] = x1_ref[...] + m[5:6] * _rms(y, g_ref[...])


def _combine(pos_flat, ys, route, x1, mod3, g_post_ffn, seq):
    m_rows = x1.shape[0]
    tm = COMBINE_TM
    per_seq = seq // tm
    return pl.pallas_call(
        _combine_kernel,
        grid=(m_rows // tm,),
        in_specs=[
            pl.BlockSpec((tm * TOP_K,), lambda i: (i,), memory_space=pltpu.SMEM),
            pl.BlockSpec(memory_space=pl.ANY),
            pl.BlockSpec((tm, LANES), lambda i: (i, 0)),
            pl.BlockSpec((tm, D_MODEL), lambda i: (i, 0)),
            pl.BlockSpec((1, N_MOD, D_MODEL), lambda i: (i // per_seq, 0, 0)),
            pl.BlockSpec((1, D_MODEL), lambda i: (0, 0)),
        ],
        out_specs=pl.BlockSpec((tm, D_MODEL), lambda i: (i, 0)),
        out_shape=jax.ShapeDtypeStruct((m_rows, D_MODEL), F32),
        scratch_shapes=[pltpu.VMEM((TOP_K, tm, D_MODEL // 2), U32), pltpu.SemaphoreType.DMA(())],
        compiler_params=_cparams(("arbitrary",)),
        name="combine",
    )(pos_flat, ys, route, x1, mod3, g_post_ffn.reshape(1, D_MODEL))


def _layer(x2, mod3, bsz, seq, g_pre_mix, g_post_mix, w_in, w_pool_grp, pool_scale, w_up_pool, w_up_attn, w_out,
           g_pre_ffn, g_post_ffn, w_router, b_router, w1, b1, w2, b2):
    m_rows = x2.shape[0]
    proj, small = _in_proj(x2, mod3, g_pre_mix, w_in, seq)
    y_attn = _attn(proj, small, bsz, seq)
    x1, hp, route, cnt = _merge(proj, y_attn, x2, mod3, w_pool_grp, pool_scale, w_up_pool, w_up_attn, w_out,
                                g_post_mix, g_pre_ffn, w_router, b_router, seq)

    bm = MOE_BM
    n_slots = m_rows * TOP_K
    n_blocks = -(-n_slots // bm) + N_EXPERTS
    idx = route[:, TOP_K:2 * TOP_K].astype(I32)
    rank = route[:, 2 * TOP_K:3 * TOP_K].astype(I32)
    counts = cnt[0, :N_EXPERTS].astype(I32)
    padded = (counts + bm - 1) // bm * bm
    pad_end = jnp.cumsum(padded)
    pad_start = pad_end - padded
    pos_flat = (pad_start[idx] + rank).reshape(-1)
    n_used = (pad_end[-1] // bm).astype(I32).reshape(1)
    blk_ids = jnp.arange(n_blocks, dtype=I32)
    blk_start = (pad_start // bm).astype(I32)
    blk_count = (padded // bm).astype(I32)
    last_of_expert = jnp.any((blk_ids[:, None] + 1) * bm == pad_end[None, :], axis=1)
    zero_flag = (last_of_expert | (blk_ids >= n_used[0])).astype(I32)

    xs = _dispatch(zero_flag, pos_flat, hp, n_blocks * bm)
    act = _ffn1(blk_start, blk_count, n_used, xs, w1, b1)
    ys = _ffn2(blk_start, blk_count, n_used, act, w2, b2)
    return _combine(pos_flat, ys, route, x1, mod3, g_post_ffn, seq)


def kernel(x, c, w_ada, b_ada, g_pre_mix, g_post_mix, w_in, w_pool_grp, pool_scale, w_up_pool, w_up_attn, w_out,
           g_pre_ffn, g_post_ffn, w_router, b_router, w1, b1, w2, b2):
    bsz, seq, d = x.shape
    assert d == D_MODEL
    depth = w_ada.shape[0]
    x2 = x.reshape(bsz * seq, d)
    for layer in range(depth):
        mod3 = _ada(c, w_ada[layer], b_ada[layer]).reshape(bsz, N_MOD, D_MODEL)
        x2 = _layer(x2, mod3, bsz, seq, g_pre_mix[layer], g_post_mix[layer], w_in[layer], w_pool_grp[layer],
                    pool_scale[layer], w_up_pool[layer], w_up_attn[layer], w_out[layer], g_pre_ffn[layer],
                    g_post_ffn[layer], w_router[layer], b_router[layer], w1[layer], b1[layer], w2[layer], b2[layer])
    return x2.reshape(bsz, seq, d)
```

```python
import functools

import jax
import jax.numpy as jnp
import numpy as np
from jax import lax
from jax.experimental import pallas as pl
from jax.experimental.pallas import tpu as pltpu

F32 = jnp.float32
BF16 = jnp.bfloat16
I32 = jnp.int32
U32 = jnp.uint32

D_MODEL = 2048
POOL_WINDOWS = (2, 4, 8, 16)
POOL_GROUPS = 4
POOL_GROUP_DIM = D_MODEL // 8
POOL_WIDTH = POOL_GROUPS * POOL_GROUP_DIM
N_HEADS = 16
N_KV_HEADS = 4
HEAD_DIM = 128
ATTN_WIDTH = N_HEADS * HEAD_DIM
KV_WIDTH = N_KV_HEADS * HEAD_DIM
ROT_DIM = HEAD_DIM // 4
IDX_HEADS = 16
IDX_DIM = 64
IDX_ROT_DIM = IDX_DIM // 4
TOPK_MAX = 256
ROPE_THETA = 500000.0
N_BRANCHES = 2
N_EXPERTS = 32
TOP_K = 4
D_FF = D_MODEL
SWIGLU_ALPHA = 1.702
SWIGLU_LIMIT = 7.0
N_MOD = 6
EPS = 1e-6

LANES = 128
INT_MIN = np.int32(-(2 ** 31))

GL_OFF = 0
Q_OFF = GL_OFF + N_BRANCHES * D_MODEL
U_OFF = Q_OFF + ATTN_WIDTH
QI_OFF = U_OFF + POOL_WIDTH
K_OFF = QI_OFF + IDX_HEADS * IDX_DIM
V_OFF = K_OFF + KV_WIDTH
PROJ_WIDTH = V_OFF + KV_WIDTH

IN_TN = 1024
ATTN_TQ = 256
ATTN_KC = 512
MASK_NEG = -1e30
MERGE_TM = 256
MOE_BM = 512
FFN1_TN = 512
FFN2_TN = 1024
DISPATCH_TM = 256
COMBINE_TM = 256
FFN_X_SLOTS = 3
VMEM_LIMIT = 56 * 1024 * 1024


def _cparams(sem):
    return pltpu.CompilerParams(dimension_semantics=sem, vmem_limit_bytes=VMEM_LIMIT)


def _rms(x, g):
    return x * lax.rsqrt(jnp.mean(x * x, axis=-1, keepdims=True) + EPS) * g


def _pack_bf16_pairs(x):
    n = x.shape[1] // 2
    bits = lax.bitcast_convert_type(x.astype(BF16).astype(F32), U32)
    return bits[:, n:] | lax.shift_right_logical(bits[:, :n], np.uint32(16))


def _unpack_bf16_pairs(w):
    lo = lax.bitcast_convert_type(lax.shift_left(w, np.uint32(16)), F32)
    hi = lax.bitcast_convert_type(w & np.uint32(0xFFFF0000), F32)
    return lo, hi


def _ada_kernel(c_ref, w_ref, b_ref, o_ref):
    c = c_ref[...]
    ca = (c * jax.nn.sigmoid(c)).astype(BF16)
    o_ref[...] = jnp.dot(ca, w_ref[...].astype(BF16), preferred_element_type=F32) + b_ref[...]


def _ada(c, w_ada, b_ada):
    bsz = c.shape[0]
    n = w_ada.shape[1]
    tn = 1024
    return pl.pallas_call(
        _ada_kernel,
        grid=(n // tn,),
        in_specs=[
            pl.BlockSpec((bsz, D_MODEL), lambda j: (0, 0)),
            pl.BlockSpec((D_MODEL, tn), lambda j: (0, j)),
            pl.BlockSpec((1, tn), lambda j: (0, j)),
        ],
        out_specs=pl.BlockSpec((bsz, tn), lambda j: (0, j)),
        out_shape=jax.ShapeDtypeStruct((bsz, n), F32),
        compiler_params=_cparams(("arbitrary",)),
        name="ada",
    )(c, w_ada, b_ada.reshape(1, n))


def _rope_slice(xs, c, s1, s2, shift):
    return xs * c + pltpu.roll(xs, LANES - shift, 1) * s1 + pltpu.roll(xs, shift, 1) * s2


def _in_kernel(x_ref, mod_ref, g_ref, w_ref, ws_ref, tab_ref, o_ref, os_ref, h_ref, acc_ref):
    j = pl.program_id(1)
    half_a = ROT_DIM // 2
    half_i = IDX_ROT_DIM // 2

    @pl.when(j == 0)
    def _():
        m = mod_ref[0]
        h = _rms(x_ref[...], g_ref[...]) * (1.0 + m[1:2]) + m[0:1]
        hb = h.astype(BF16)
        h_ref[...] = hb
        small = jnp.dot(hb, ws_ref[...], preferred_element_type=F32)
        os_ref[...] = _rope_slice(small, tab_ref[6], tab_ref[7], tab_ref[8], half_i)

    acc_ref[...] = jnp.dot(h_ref[...], w_ref[...], preferred_element_type=F32)
    n_sl = IN_TN // LANES
    j_q0, j_q1 = Q_OFF // IN_TN, U_OFF // IN_TN
    j_qi = QI_OFF // IN_TN
    j_kv = K_OFF // IN_TN
    k_sl = KV_WIDTH // LANES

    @pl.when((j < j_q0) | (j == U_OFF // IN_TN))
    def _():
        o_ref[...] = acc_ref[...].astype(BF16)

    @pl.when((j >= j_q0) & (j < j_q1))
    def _():
        for s in range(n_sl):
            sl = slice(s * LANES, (s + 1) * LANES)
            o_ref[:, sl] = _rope_slice(acc_ref[:, sl], tab_ref[0], tab_ref[1], tab_ref[2], half_a).astype(BF16)

    @pl.when(j == j_qi)
    def _():
        for s in range(n_sl):
            sl = slice(s * LANES, (s + 1) * LANES)
            o_ref[:, sl] = _rope_slice(acc_ref[:, sl], tab_ref[3], tab_ref[4], tab_ref[5], half_i).astype(BF16)

    @pl.when(j == j_kv)
    def _():
        for s in range(k_sl):
            sl = slice(s * LANES, (s + 1) * LANES)
            o_ref[:, sl] = _rope_slice(acc_ref[:, sl], tab_ref[0], tab_ref[1], tab_ref[2], half_a).astype(BF16)
        o_ref[:, KV_WIDTH:] = acc_ref[:, KV_WIDTH:].astype(BF16)


def _rope_tables(seq):
    def tabs(rot_dim, period, n_rep):
        half = rot_dim // 2
        inv = np.float32(ROPE_THETA) ** (-np.arange(0, rot_dim, 2, dtype=np.float32) / np.float32(rot_dim))
        ang = np.arange(seq, dtype=np.float32)[:, None] * inv.astype(np.float32)[None, :]
        cos, sin = np.cos(ang).astype(np.float32), np.sin(ang).astype(np.float32)
        ones = np.ones((seq, period - 2 * half), np.float32)
        z_h = np.zeros((seq, half), np.float32)
        c = np.concatenate([cos, cos, ones], axis=1)
        s1 = np.concatenate([-sin, z_h, 0 * ones], axis=1)
        s2 = np.concatenate([z_h, sin, 0 * ones], axis=1)
        return [np.tile(t, (1, n_rep)) for t in (c, s1, s2)]

    ta = tabs(ROT_DIM, HEAD_DIM, LANES // HEAD_DIM)
    ti = tabs(IDX_ROT_DIM, IDX_DIM, LANES // IDX_DIM)
    ts = tabs(IDX_ROT_DIM, IDX_DIM, 1)
    wi_scale = np.float32(IDX_HEADS ** -0.5 * IDX_DIM ** -0.5)
    pad = LANES - IDX_DIM
    c_tail = np.where(np.arange(pad) < IDX_HEADS, wi_scale, np.float32(0.0)).astype(np.float32)
    ts = [
        np.concatenate([ts[0], np.broadcast_to(c_tail, (seq, pad))], axis=1),
        np.concatenate([ts[1], np.zeros((seq, pad), np.float32)], axis=1),
        np.concatenate([ts[2], np.zeros((seq, pad), np.float32)], axis=1),
    ]
    return jnp.asarray(np.stack(ta + ti + ts, axis=0))


def _in_proj(x2, mod3, g_pre, w_in, seq):
    m_rows = x2.shape[0]
    tm = min(1024, seq)
    assert seq % tm == 0 and m_rows % tm == 0
    per_seq = seq // tm
    offs = np.cumsum((POOL_WIDTH, ATTN_WIDTH, KV_WIDTH, KV_WIDTH, IDX_HEADS * IDX_DIM, IDX_DIM, IDX_HEADS))
    w_u, w_q, w_k, w_v, w_qi, w_ki, w_wi, w_gl = jnp.split(w_in, [int(o) for o in offs], axis=1)
    w_main = jnp.concatenate([w_gl, w_q, w_u, w_qi, w_k, w_v], axis=1).astype(BF16)
    w_small = jnp.concatenate(
        [w_ki, w_wi, jnp.zeros((D_MODEL, LANES - IDX_DIM - IDX_HEADS), F32)], axis=1
    ).astype(BF16)
    tabs = _rope_tables(seq)
    n_j = PROJ_WIDTH // IN_TN
    return pl.pallas_call(
        _in_kernel,
        grid=(m_rows // tm, n_j),
        in_specs=[
            pl.BlockSpec((tm, D_MODEL), lambda i, j: (i, 0)),
            pl.BlockSpec((1, N_MOD, D_MODEL), lambda i, j: (i // per_seq, 0, 0)),
            pl.BlockSpec((1, D_MODEL), lambda i, j: (0, 0)),
            pl.BlockSpec((D_MODEL, IN_TN), lambda i, j: (0, j)),
            pl.BlockSpec((D_MODEL, LANES), lambda i, j: (0, 0)),
            pl.BlockSpec((9, tm, LANES), lambda i, j: (0, i % per_seq, 0)),
        ],
        out_specs=[
            pl.BlockSpec((tm, IN_TN), lambda i, j: (i, j)),
            pl.BlockSpec((tm, LANES), lambda i, j: (i, 0)),
        ],
        out_shape=[
            jax.ShapeDtypeStruct((m_rows, PROJ_WIDTH), BF16),
            jax.ShapeDtypeStruct((m_rows, LANES), F32),
        ],
        scratch_shapes=[pltpu.VMEM((tm, D_MODEL), BF16), pltpu.VMEM((tm, IN_TN), F32)],
        compiler_params=_cparams(("arbitrary", "arbitrary")),
        name="in_proj",
    )(x2, mod3, g_pre.reshape(1, D_MODEL), w_main, w_small, tabs)


def _attn_kernel(q_ref, qi_ref, k_ref, v_ref, kis_ref, wis_ref, o_ref,
                 key_ref, plim_ref, qs_ref, m_ref, l_ref, acc_ref, *, tq, kc, seq, n_keep):
    i = pl.program_id(1)
    nk = ((i + 1) * tq + kc - 1) // kc
    nt = (((1,), (1,)), ((), ()))
    grp = N_HEADS // N_KV_HEADS
    row_io = lax.broadcasted_iota(I32, (kc, tq), 0)
    tpos = i * tq + lax.broadcasted_iota(I32, (kc, tq), 1)
    w_t = wis_ref[...].T

    def idx_chunk(c, carry):
        r0 = pl.multiple_of(c * kc, kc)
        ki = kis_ref[pl.ds(r0, kc), :IDX_DIM].astype(BF16)
        sc = jnp.zeros((kc, tq), F32)
        for h in range(IDX_HEADS):
            qh = qi_ref[:, h * IDX_DIM:(h + 1) * IDX_DIM]
            d = lax.dot_general(ki, qh, nt, preferred_element_type=F32)
            sc = sc + jnp.maximum(d, 0.0) * w_t[IDX_DIM + h:IDX_DIM + h + 1, :]
        bits = lax.bitcast_convert_type(sc, I32)
        key = jnp.where(bits < 0, bits ^ np.int32(0x7FFFFFFF), bits)
        key_ref[pl.ds(r0, kc), :] = jnp.where(r0 + row_io <= tpos, key, INT_MIN)
        return carry

    lax.fori_loop(0, nk, idx_chunk, 0)

    def count(pred):
        def body(c, acc):
            r0 = pl.multiple_of(c * kc, kc)
            hit = pred(key_ref[pl.ds(r0, kc), :], r0)
            return acc + jnp.sum(hit.astype(I32), axis=0, keepdims=True)

        return lax.fori_loop(0, nk, body, jnp.zeros((1, tq), I32))

    def bit_step(it, tu):
        cand = tu | lax.shift_left(np.int32(1), 31 - it)
        cnt = count(lambda kch, r0: kch >= (cand ^ INT_MIN))
        return jnp.where(cnt >= n_keep, cand, tu)

    tu = lax.fori_loop(0, 32, bit_step, jnp.zeros((1, tq), I32))
    thr = tu ^ INT_MIN
    cnt_gt = count(lambda kch, r0: kch > thr)
    cnt_ge = count(lambda kch, r0: kch >= thr)
    tie_i = ((cnt_ge > n_keep) & (tu != 0)).astype(I32)
    need = n_keep - cnt_gt

    plim_ref[...] = jnp.full(plim_ref.shape, seq, I32)

    @pl.when(jnp.max(tie_i) > 0)
    def _():
        n_bits = (seq - 1).bit_length()

        def idx_step(it, p):
            cand = p | lax.shift_left(np.int32(1), (n_bits - 1) - it)
            below = count(lambda kch, r0: (kch == thr) & (r0 + row_io < cand))
            return jnp.where(below < need, cand, p)

        p = lax.fori_loop(0, n_bits, idx_step, jnp.zeros((1, tq), I32))
        plim_ref[...] = jnp.broadcast_to(jnp.where(tie_i > 0, p, seq), plim_ref.shape)

    for g in range(N_KV_HEADS):
        for r in range(grp):
            hh = g * grp + r
            qs_ref[g, r * tq:(r + 1) * tq, :] = q_ref[:, hh * HEAD_DIM:(hh + 1) * HEAD_DIM]
    m_ref[...] = jnp.full(m_ref.shape, MASK_NEG, F32)
    l_ref[...] = jnp.zeros_like(l_ref)
    acc_ref[...] = jnp.zeros_like(acc_ref)
    c_exp = HEAD_DIM ** -0.5 * np.log2(np.e)

    def att_chunk(c, carry):
        r0 = pl.multiple_of(c * kc, kc)
        kch = key_ref[pl.ds(r0, kc), :]
        row = r0 + row_io
        sel = ((kch > thr) | ((kch == thr) & (row <= plim_ref[0:1, :]))) & (row <= tpos)
        bias_t = jnp.where(sel, 0.0, MASK_NEG).astype(F32).T
        for g in range(N_KV_HEADS):
            kg = k_ref[pl.ds(r0, kc), g * HEAD_DIM:(g + 1) * HEAD_DIM]
            vg = v_ref[pl.ds(r0, kc), g * HEAD_DIM:(g + 1) * HEAD_DIM]
            s = lax.dot_general(qs_ref[g], kg, nt, preferred_element_type=F32)
            s = (s.reshape(grp, tq, kc) + bias_t[None]).reshape(grp * tq, kc)
            m_old = m_ref[g]
            m_new = jnp.maximum(m_old, jnp.max(s, axis=-1, keepdims=True))
            alpha = jnp.exp2((m_old - m_new) * c_exp)
            p = jnp.exp2((s - m_new) * c_exp)
            l_ref[g] = alpha * l_ref[g] + jnp.sum(p, axis=-1, keepdims=True)
            acc_ref[g] = alpha * acc_ref[g] + jnp.dot(p.astype(BF16), vg, preferred_element_type=F32)
            m_ref[g] = m_new
        return carry

    lax.fori_loop(0, nk, att_chunk, 0)
    for g in range(N_KV_HEADS):
        o = acc_ref[g] / l_ref[g]
        for r in range(grp):
            hh = g * grp + r
            o_ref[:, hh * HEAD_DIM:(hh + 1) * HEAD_DIM] = o[r * tq:(r + 1) * tq, :].astype(BF16)


def _attn(proj, small, bsz, seq):
    tq = min(ATTN_TQ, seq)
    kc = min(ATTN_KC, seq)
    assert seq % tq == 0 and seq % kc == 0 and kc % tq == 0
    nq = seq // tq
    n_keep = min(TOPK_MAX, seq // 4)
    grp = N_HEADS // N_KV_HEADS
    kern = functools.partial(_attn_kernel, tq=tq, kc=kc, seq=seq, n_keep=n_keep)
    return pl.pallas_call(
        kern,
        grid=(bsz, nq),
        in_specs=[
            pl.BlockSpec((tq, ATTN_WIDTH), lambda b, i: (b * nq + i, Q_OFF // ATTN_WIDTH)),
            pl.BlockSpec((tq, IDX_HEADS * IDX_DIM), lambda b, i: (b * nq + i, QI_OFF // (IDX_HEADS * IDX_DIM))),
            pl.BlockSpec((seq, KV_WIDTH), lambda b, i: (b, K_OFF // KV_WIDTH)),
            pl.BlockSpec((seq, KV_WIDTH), lambda b, i: (b, V_OFF // KV_WIDTH)),
            pl.BlockSpec((seq, LANES), lambda b, i: (b, 0)),
            pl.BlockSpec((tq, LANES), lambda b, i: (b * nq + i, 0)),
        ],
        out_specs=pl.BlockSpec((tq, ATTN_WIDTH), lambda b, i: (b * nq + i, 0)),
        out_shape=jax.ShapeDtypeStruct((bsz * seq, ATTN_WIDTH), BF16),
        scratch_shapes=[
            pltpu.VMEM((seq, tq), I32),
            pltpu.VMEM((8, tq), I32),
            pltpu.VMEM((N_KV_HEADS, grp * tq, HEAD_DIM), BF16),
            pltpu.VMEM((N_KV_HEADS, grp * tq, 1), F32),
            pltpu.VMEM((N_KV_HEADS, grp * tq, 1), F32),
            pltpu.VMEM((N_KV_HEADS, grp * tq, HEAD_DIM), F32),
        ],
        compiler_params=_cparams(("arbitrary", "arbitrary")),
        name="attn",
    )(proj, proj, proj, proj, small, small)


def _merge_kernel(
    u_ref, halo_ref, ya_ref, gl_ref, x_ref, mod_ref, wgrp_ref, pscale_ref, wup_p_ref, wup_a_ref, wout_ref,
    gpost_ref, gffn_ref, wr_ref, br_ref,
    x1_ref, hp_ref, route_ref, cnt_ref,
    carry_ref, *, seq,
):
    tm = MERGE_TM
    i = pl.program_id(0)
    pos0 = (i * tm) % seq

    @pl.when(i == 0)
    def _():
        carry_ref[...] = jnp.zeros_like(carry_ref)

    halo_rows = halo_ref.shape[0]
    halo = jnp.where(pos0 == 0, 0.0, halo_ref[...].astype(F32))
    ext = jnp.concatenate([halo, u_ref[...].astype(F32)], axis=0)
    pos = pos0 + lax.broadcasted_iota(I32, (tm, 1), 0)
    ys = []
    for g, win in enumerate(POOL_WINDOWS):
        sl = slice(g * POOL_GROUP_DIM, (g + 1) * POOL_GROUP_DIM)
        e = ext[:, sl]
        acc = e
        span = 1
        while span < win:
            acc = acc + pltpu.roll(acc, span, 0)
            span *= 2
        cnt = jnp.minimum(pos + 1, win).astype(F32)
        mixed = (acc[halo_rows:] / cnt - e[halo_rows:]).astype(BF16)
        ys.append(jnp.dot(mixed, wgrp_ref[g], preferred_element_type=F32))
    y_pool = (jnp.concatenate(ys, axis=1) * pscale_ref[...]).astype(BF16)

    gates = jax.nn.sigmoid(gl_ref[...].astype(F32))
    up_p = jnp.dot(y_pool, wup_p_ref[...], preferred_element_type=F32)
    up_a = jnp.dot(ya_ref[...], wup_a_ref[...], preferred_element_type=F32)
    merged = (gates[:, :D_MODEL] * up_p + gates[:, D_MODEL:] * up_a).astype(BF16)
    y = jnp.dot(merged, wout_ref[...], preferred_element_type=F32)
    m = mod_ref[0]
    x1 = x_ref[...] + m[2:3] * _rms(y, gpost_ref[...])
    x1_ref[...] = x1

    h2f = _rms(x1, gffn_ref[...]) * (1.0 + m[4:5]) + m[3:4]
    h2 = h2f.astype(BF16)
    hp_ref[...] = _pack_bf16_pairs(h2f)

    lane = lax.broadcasted_iota(I32, (tm, LANES), 1)
    logits = jnp.dot(h2, wr_ref[...], preferred_element_type=F32) + br_ref[...]
    work = jnp.where(lane < N_EXPERTS, logits, -jnp.inf)
    vals, hots = [], []
    for _ in range(TOP_K):
        mx = jnp.max(work, axis=-1, keepdims=True)
        first = jnp.min(jnp.where(work == mx, lane, LANES), axis=-1, keepdims=True)
        hot = lane == first
        vals.append(mx)
        hots.append(hot)
        work = jnp.where(hot, -jnp.inf, work)
    ex = [jnp.exp(v - vals[0]) for v in vals]
    den = ex[0] + ex[1] + ex[2] + ex[3]

    onehot = jnp.zeros((tm, LANES), F32)
    for hot in hots:
        onehot = onehot + hot.astype(F32)
    r_i = lax.broadcasted_iota(I32, (tm, tm), 0)
    c_i = lax.broadcasted_iota(I32, (tm, tm), 1)
    tril = (c_i < r_i).astype(BF16)
    before = jnp.dot(tril, onehot.astype(BF16), preferred_element_type=F32) + carry_ref[0:1, :]
    out = jnp.zeros((tm, LANES), F32)
    lane_f = lane.astype(F32)
    for k in range(TOP_K):
        out = jnp.where(lane == k, ex[k] / den, out)
        idx_k = jnp.sum(jnp.where(hots[k], lane_f, 0.0), axis=-1, keepdims=True)
        out = jnp.where(lane == TOP_K + k, idx_k, out)
        rank_k = jnp.sum(jnp.where(hots[k], before, 0.0), axis=-1, keepdims=True)
        out = jnp.where(lane == 2 * TOP_K + k, rank_k, out)
    route_ref[...] = out
    total = carry_ref[0:1, :] + jnp.sum(onehot, axis=0, keepdims=True)
    carry_ref[...] = jnp.broadcast_to(total, carry_ref.shape)
    cnt_ref[...] = jnp.broadcast_to(total, cnt_ref.shape)


def _merge(proj, y_attn, x2, mod3, w_pool_grp, pool_scale, w_up_pool, w_up_attn, w_out, g_post_mix, g_pre_ffn,
           w_router, b_router, seq):
    m_rows = x2.shape[0]
    tm = MERGE_TM
    halo = 16
    assert seq % tm == 0 and max(POOL_WINDOWS) <= halo
    per_seq = seq // tm
    wr = jnp.concatenate([w_router, jnp.zeros((D_MODEL, LANES - N_EXPERTS), F32)], axis=1).astype(BF16)
    br = jnp.concatenate([b_router, jnp.zeros((LANES - N_EXPERTS,), F32)]).reshape(1, LANES)
    const = lambda shape: pl.BlockSpec(shape, lambda i: (0,) * len(shape), pipeline_mode=pl.Buffered(1))
    kern = functools.partial(_merge_kernel, seq=seq)
    return pl.pallas_call(
        kern,
        grid=(m_rows // tm,),
        in_specs=[
            pl.BlockSpec((tm, POOL_WIDTH), lambda i: (i, U_OFF // POOL_WIDTH)),
            pl.BlockSpec((halo, POOL_WIDTH), lambda i: (jnp.maximum(i * (tm // halo) - 1, 0), U_OFF // POOL_WIDTH)),
            pl.BlockSpec((tm, ATTN_WIDTH), lambda i: (i, 0)),
            pl.BlockSpec((tm, N_BRANCHES * D_MODEL), lambda i: (i, 0)),
            pl.BlockSpec((tm, D_MODEL), lambda i: (i, 0)),
            pl.BlockSpec((1, N_MOD, D_MODEL), lambda i: (i // per_seq, 0, 0)),
            const((POOL_GROUPS, POOL_GROUP_DIM, POOL_GROUP_DIM)),
            const((1, POOL_WIDTH)),
            const((POOL_WIDTH, D_MODEL)),
            const((ATTN_WIDTH, D_MODEL)),
            const((D_MODEL, D_MODEL)),
            const((1, D_MODEL)),
            const((1, D_MODEL)),
            const((D_MODEL, LANES)),
            const((1, LANES)),
        ],
        out_specs=[
            pl.BlockSpec((tm, D_MODEL), lambda i: (i, 0)),
            pl.BlockSpec((tm, D_MODEL // 2), lambda i: (i, 0)),
            pl.BlockSpec((tm, LANES), lambda i: (i, 0)),
            pl.BlockSpec((8, LANES), lambda i: (0, 0)),
        ],
        out_shape=[
            jax.ShapeDtypeStruct((m_rows, D_MODEL), F32),
            jax.ShapeDtypeStruct((m_rows, D_MODEL // 2), U32),
            jax.ShapeDtypeStruct((m_rows, LANES), F32),
            jax.ShapeDtypeStruct((8, LANES), F32),
        ],
        scratch_shapes=[pltpu.VMEM((8, LANES), F32)],
        compiler_params=_cparams(("arbitrary",)),
        name="merge",
    )(
        proj, proj, y_attn, proj, x2, mod3,
        w_pool_grp.astype(BF16), pool_scale.reshape(1, POOL_WIDTH), w_up_pool.astype(BF16),
        w_up_attn.astype(BF16), w_out.astype(BF16), g_post_mix.reshape(1, D_MODEL), g_pre_ffn.reshape(1, D_MODEL),
        wr, br,
    )


def _dispatch_kernel(zf_ref, pos_ref, h_ref, xs_ref, zbuf, sem, zsem):
    tm = h_ref.shape[0]
    bm = zbuf.shape[0]

    @pl.when(pl.program_id(0) == 0)
    def _():
        zbuf[...] = jnp.zeros_like(zbuf)

        def zero_copy(b):
            return pltpu.make_async_copy(zbuf, xs_ref.at[pl.ds(b * bm, bm), :], zsem)

        def z_issue(b, carry):
            pl.when(zf_ref[b] != 0)(lambda: zero_copy(b).start())
            return carry

        def z_drain(b, carry):
            pl.when(zf_ref[b] != 0)(lambda: zero_copy(b).wait())
            return carry

        lax.fori_loop(0, zf_ref.shape[0], z_issue, 0)
        lax.fori_loop(0, zf_ref.shape[0], z_drain, 0)

    def row_copy(t, k):
        return pltpu.make_async_copy(
            h_ref.at[pl.ds(t, 1), :], xs_ref.at[pl.ds(pos_ref[t * TOP_K + k], 1), :], sem
        )

    def issue(t, carry):
        for k in range(TOP_K):
            row_copy(t, k).start(priority=k % 2)
        return carry

    def drain(t, carry):
        for k in range(TOP_K):
            row_copy(t, k).wait()
        return carry

    lax.fori_loop(0, tm, issue, 0)
    lax.fori_loop(0, tm, drain, 0)


def _dispatch(zero_flag, pos_flat, hp, n_rows):
    m_rows, width = hp.shape
    tm = DISPATCH_TM
    grid_spec = pltpu.PrefetchScalarGridSpec(
        num_scalar_prefetch=1,
        grid=(m_rows // tm,),
        in_specs=[
            pl.BlockSpec((tm * TOP_K,), lambda i, zf: (i,), memory_space=pltpu.SMEM),
            pl.BlockSpec((tm, width), lambda i, zf: (i, 0)),
        ],
        out_specs=pl.BlockSpec(memory_space=pl.ANY),
        scratch_shapes=[pltpu.VMEM((MOE_BM, width), U32), pltpu.SemaphoreType.DMA(()), pltpu.SemaphoreType.DMA(())],
    )
    return pl.pallas_call(
        _dispatch_kernel,
        grid_spec=grid_spec,
        out_shape=jax.ShapeDtypeStruct((n_rows, width), U32),
        compiler_params=_cparams(("arbitrary",)),
        name="dispatch",
    )(zero_flag, pos_flat, hp)


def _expert_rows(e, j, bs_ref, nb_ref, nu_ref, x_hbm, o_hbm, xbuf, obuf, xsem, osem, compute, prepare):
    n_x, bm = xbuf.shape[0], xbuf.shape[1]
    tw = obuf.shape[2]
    n_blocks = x_hbm.shape[0] // bm
    b0 = bs_ref[e]
    nb = nb_ref[e]
    n_used = nu_ref[0]

    def x_copy(blk, slot):
        return pltpu.make_async_copy(x_hbm.at[pl.ds(blk * bm, bm), :], xbuf.at[slot], xsem.at[slot])

    def o_copy(blk, slot):
        return pltpu.make_async_copy(obuf.at[slot], o_hbm.at[pl.ds(blk * bm, bm), pl.ds(j * tw, tw)], osem.at[slot])

    @pl.when(nb > 0)
    def _():
        prepare()

        def body(rb, carry):
            g = b0 + rb
            slot = g % 2
            ahead = n_x - 1

            @pl.when(g == 0)
            def _():
                for a in range(ahead):
                    pl.when(a < n_used)(lambda a=a: x_copy(a, a).start())

            x_copy(g, g % n_x).wait()

            @pl.when(g + ahead < n_used)
            def _():
                x_copy(g + ahead, (g + ahead) % n_x).start()

            @pl.when(g >= 2)
            def _():
                o_copy(g - 2, slot).wait()

            obuf[slot] = compute(xbuf[g % n_x])
            o_copy(g, slot).start()

            @pl.when(g == n_used - 1)
            def _():
                @pl.when(g >= 1)
                def _():
                    o_copy(g - 1, 1 - slot).wait()

                o_copy(g, slot).wait()

            return carry

        lax.fori_loop(0, nb, body, 0)

    @pl.when(e == N_EXPERTS - 1)
    def _():
        obuf[0] = jnp.zeros(obuf.shape[1:], obuf.dtype)

        def z_issue(blk, carry):
            o_copy(blk, 0).start()
            return carry

        def z_drain(blk, carry):
            o_copy(blk, 0).wait()
            return carry

        lax.fori_loop(nu_ref[0], n_blocks, z_issue, 0)
        lax.fori_loop(nu_ref[0], n_blocks, z_drain, 0)


def _ffn1_kernel(bs_ref, nb_ref, nu_ref, x_hbm, wg_ref, wl_ref, bg_ref, bl_ref, o_hbm,
                 wg_s, wl_s, xbuf, obuf, xsem, osem):
    j, e = pl.program_id(0), pl.program_id(1)
    half = D_MODEL // 2

    def prepare():
        wg_s[...] = wg_ref[0].astype(BF16)
        wl_s[...] = wl_ref[0].astype(BF16)

    def compute(xp):
        lo, hi = (v.astype(BF16) for v in _unpack_bf16_pairs(xp))

        def proj(w_s, b_ref):
            return (
                jnp.dot(lo, w_s[:half, :], preferred_element_type=F32)
                + jnp.dot(hi, w_s[half:, :], preferred_element_type=F32)
                + b_ref[0]
            )

        glu = jnp.minimum(proj(wg_s, bg_ref), SWIGLU_LIMIT)
        lin = jnp.clip(proj(wl_s, bl_ref), -SWIGLU_LIMIT, SWIGLU_LIMIT)
        return (glu * jax.nn.sigmoid(SWIGLU_ALPHA * glu) * (lin + 1.0)).astype(BF16)

    _expert_rows(e, j, bs_ref, nb_ref, nu_ref, x_hbm, o_hbm, xbuf, obuf, xsem, osem, compute, prepare)


def _ffn1(blk_start, blk_count, n_used, xs, w1, b1):
    n_rows = xs.shape[0]
    bm, tn = MOE_BM, FFN1_TN
    nj = D_FF // tn
    b1r = b1.reshape(N_EXPERTS, 1, 2 * D_FF)
    grid_spec = pltpu.PrefetchScalarGridSpec(
        num_scalar_prefetch=3,
        grid=(nj, N_EXPERTS),
        in_specs=[
            pl.BlockSpec(memory_space=pl.ANY),
            pl.BlockSpec((1, D_MODEL, tn), lambda j, e, *_: (e, 0, j)),
            pl.BlockSpec((1, D_MODEL, tn), lambda j, e, *_: (e, 0, nj + j)),
            pl.BlockSpec((1, 1, tn), lambda j, e, *_: (e, 0, j)),
            pl.BlockSpec((1, 1, tn), lambda j, e, *_: (e, 0, nj + j)),
        ],
        out_specs=pl.BlockSpec(memory_space=pl.ANY),
        scratch_shapes=[
            pltpu.VMEM((D_MODEL, tn), BF16),
            pltpu.VMEM((D_MODEL, tn), BF16),
            pltpu.VMEM((FFN_X_SLOTS, bm, D_MODEL // 2), U32),
            pltpu.VMEM((2, bm, tn), BF16),
            pltpu.SemaphoreType.DMA((FFN_X_SLOTS,)),
            pltpu.SemaphoreType.DMA((2,)),
        ],
    )
    return pl.pallas_call(
        _ffn1_kernel,
        grid_spec=grid_spec,
        out_shape=jax.ShapeDtypeStruct((n_rows, D_FF), BF16),
        compiler_params=_cparams(("arbitrary", "arbitrary")),
        name="ffn1",
    )(blk_start, blk_count, n_used, xs, w1, w1, b1r, b1r)


def _ffn2_kernel(bs_ref, nb_ref, nu_ref, a_hbm, w_ref, b_ref, o_hbm, w_s, xbuf, obuf, xsem, osem):
    j, e = pl.program_id(0), pl.program_id(1)

    def prepare():
        w_s[...] = w_ref[0].astype(BF16)

    def compute(a):
        return _pack_bf16_pairs(jnp.dot(a, w_s[...], preferred_element_type=F32) + b_ref[0])

    _expert_rows(e, j, bs_ref, nb_ref, nu_ref, a_hbm, o_hbm, xbuf, obuf, xsem, osem, compute, prepare)


def _ffn2(blk_start, blk_count, n_used, act, w2, b2):
    n_rows = act.shape[0]
    bm, tn = MOE_BM, FFN2_TN
    nj = D_MODEL // tn
    b2r = b2.reshape(N_EXPERTS, 1, D_MODEL)
    grid_spec = pltpu.PrefetchScalarGridSpec(
        num_scalar_prefetch=3,
        grid=(nj, N_EXPERTS),
        in_specs=[
            pl.BlockSpec(memory_space=pl.ANY),
            pl.BlockSpec((1, D_FF, tn), lambda j, e, *_: (e, 0, j)),
            pl.BlockSpec((1, 1, tn), lambda j, e, *_: (e, 0, j)),
        ],
        out_specs=pl.BlockSpec(memory_space=pl.ANY),
        scratch_shapes=[
            pltpu.VMEM((D_FF, tn), BF16),
            pltpu.VMEM((FFN_X_SLOTS, bm, D_FF), BF16),
            pltpu.VMEM((2, bm, tn // 2), U32),
            pltpu.SemaphoreType.DMA((FFN_X_SLOTS,)),
            pltpu.SemaphoreType.DMA((2,)),
        ],
    )
    return pl.pallas_call(
        _ffn2_kernel,
        grid_spec=grid_spec,
        out_shape=jax.ShapeDtypeStruct((n_rows, D_MODEL // 2), U32),
        compiler_params=_cparams(("arbitrary", "arbitrary")),
        name="ffn2",
    )(blk_start, blk_count, n_used, act, w2, b2r)


def _combine_kernel(pos_ref, ys_ref, route_ref, x1_ref, mod_ref, g_ref, o_ref, buf, sem):
    tm = x1_ref.shape[0]

    def row_copy(t, k):
        return pltpu.make_async_copy(
            ys_ref.at[pl.ds(pos_ref[t * TOP_K + k], 1), :], buf.at[k, pl.ds(t, 1), :], sem
        )

    def issue(t, carry):
        for k in range(TOP_K):
            row_copy(t, k).start(priority=k % 2)
        return carry

    def drain(t, carry):
        for k in range(TOP_K):
            row_copy(t, k).wait()
        return carry

    lax.fori_loop(0, tm, issue, 0)
    lax.fori_loop(0, tm, drain, 0)
    route = route_ref[...]
    y_lo, y_hi = (route[:, 0:1] * v for v in _unpack_bf16_pairs(buf[0]))
    for k in range(1, TOP_K):
        lo, hi = _unpack_bf16_pairs(buf[k])
        y_lo = y_lo + route[:, k:k + 1] * lo
        y_hi = y_hi + route[:, k:k + 1] * hi
    hw = FFN2_TN // 2
    parts = []
    for t in range(D_MODEL // FFN2_TN):
        parts += [y_lo[:, t * hw:(t + 1) * hw], y_hi[:, t * hw:(t + 1) * hw]]
    y = jnp.concatenate(parts, axis=1)
    m = mod_ref[0]
    o_ref[...] = x1_ref[...] + m[5:6] * _rms(y, g_ref[...])


def _combine(pos_flat, ys, route, x1, mod3, g_post_ffn, seq):
    m_rows = x1.shape[0]
    tm = COMBINE_TM
    per_seq = seq // tm
    return pl.pallas_call(
        _combine_kernel,
        grid=(m_rows // tm,),
        in_specs=[
            pl.BlockSpec((tm * TOP_K,), lambda i: (i,), memory_space=pltpu.SMEM),
            pl.BlockSpec(memory_space=pl.ANY),
            pl.BlockSpec((tm, LANES), lambda i: (i, 0)),
            pl.BlockSpec((tm, D_MODEL), lambda i: (i, 0)),
            pl.BlockSpec((1, N_MOD, D_MODEL), lambda i: (i // per_seq, 0, 0)),
            pl.BlockSpec((1, D_MODEL), lambda i: (0, 0)),
        ],
        out_specs=pl.BlockSpec((tm, D_MODEL), lambda i: (i, 0)),
        out_shape=jax.ShapeDtypeStruct((m_rows, D_MODEL), F32),
        scratch_shapes=[pltpu.VMEM((TOP_K, tm, D_MODEL // 2), U32), pltpu.SemaphoreType.DMA(())],
        compiler_params=_cparams(("arbitrary",)),
        name="combine",
    )(pos_flat, ys, route, x1, mod3, g_post_ffn.reshape(1, D_MODEL))


def _layer(x2, mod3, bsz, seq, g_pre_mix, g_post_mix, w_in, w_pool_grp, pool_scale, w_up_pool, w_up_attn, w_out,
           g_pre_ffn, g_post_ffn, w_router, b_router, w1, b1, w2, b2):
    m_rows = x2.shape[0]
    proj, small = _in_proj(x2, mod3, g_pre_mix, w_in, seq)
    y_attn = _attn(proj, small, bsz, seq)
    x1, hp, route, cnt = _merge(proj, y_attn, x2, mod3, w_pool_grp, pool_scale, w_up_pool, w_up_attn, w_out,
                                g_post_mix, g_pre_ffn, w_router, b_router, seq)

    bm = MOE_BM
    n_slots = m_rows * TOP_K
    n_blocks = -(-n_slots // bm) + N_EXPERTS
    idx = route[:, TOP_K:2 * TOP_K].astype(I32)
    rank = route[:, 2 * TOP_K:3 * TOP_K].astype(I32)
    counts = cnt[0, :N_EXPERTS].astype(I32)
    padded = (counts + bm - 1) // bm * bm
    pad_end = jnp.cumsum(padded)
    pad_start = pad_end - padded
    pos_flat = (pad_start[idx] + rank).reshape(-1)
    n_used = (pad_end[-1] // bm).astype(I32).reshape(1)
    blk_ids = jnp.arange(n_blocks, dtype=I32)
    blk_start = (pad_start // bm).astype(I32)
    blk_count = (padded // bm).astype(I32)
    last_of_expert = jnp.any((blk_ids[:, None] + 1) * bm == pad_end[None, :], axis=1)
    zero_flag = (last_of_expert | (blk_ids >= n_used[0])).astype(I32)

    xs = _dispatch(zero_flag, pos_flat, hp, n_blocks * bm)
    act = _ffn1(blk_start, blk_count, n_used, xs, w1, b1)
    ys = _ffn2(blk_start, blk_count, n_used, act, w2, b2)
    return _combine(pos_flat, ys, route, x1, mod3, g_post_ffn, seq)


def kernel(x, c, w_ada, b_ada, g_pre_mix, g_post_mix, w_in, w_pool_grp, pool_scale, w_up_pool, w_up_attn, w_out,
           g_pre_ffn, g_post_ffn, w_router, b_router, w1, b1, w2, b2):
    bsz, seq, d = x.shape
    assert d == D_MODEL
    depth = w_ada.shape[0]
    x2 = x.reshape(bsz * seq, d)
    for layer in range(depth):
        mod3 = _ada(c, w_ada[layer], b_ada[layer]).reshape(bsz, N_MOD, D_MODEL)
        x2 = _layer(x2, mod3, bsz, seq, g_pre_mix[layer], g_post_mix[layer], w_in[layer], w_pool_grp[layer],
                    pool_scale[layer], w_up_pool[layer], w_up_attn[layer], w_out[layer], g_pre_ffn[layer],
                    g_post_ffn[layer], w_router[layer], b_router[layer], w1[layer], b1[layer], w2[layer], b2[layer])
    return x2.reshape(bsz, seq, d)
```

```python
import functools

import jax
import jax.numpy as jnp
import numpy as np
from jax import lax
from jax.experimental import pallas as pl
from jax.experimental.pallas import tpu as pltpu

F32 = jnp.float32
BF16 = jnp.bfloat16
I32 = jnp.int32
U32 = jnp.uint32

D_MODEL = 2048
POOL_WINDOWS = (2, 4, 8, 16)
POOL_GROUPS = 4
POOL_GROUP_DIM = D_MODEL // 8
POOL_WIDTH = POOL_GROUPS * POOL_GROUP_DIM
N_HEADS = 16
N_KV_HEADS = 4
HEAD_DIM = 128
ATTN_WIDTH = N_HEADS * HEAD_DIM
KV_WIDTH = N_KV_HEADS * HEAD_DIM
ROT_DIM = HEAD_DIM // 4
IDX_HEADS = 16
IDX_DIM = 64
IDX_ROT_DIM = IDX_DIM // 4
TOPK_MAX = 256
ROPE_THETA = 500000.0
N_BRANCHES = 2
N_EXPERTS = 32
TOP_K = 4
D_FF = D_MODEL
SWIGLU_ALPHA = 1.702
SWIGLU_LIMIT = 7.0
N_MOD = 6
EPS = 1e-6

LANES = 128
INT_MIN = np.int32(-(2 ** 31))

GL_OFF = 0
Q_OFF = GL_OFF + N_BRANCHES * D_MODEL
U_OFF = Q_OFF + ATTN_WIDTH
QI_OFF = U_OFF + POOL_WIDTH
K_OFF = QI_OFF + IDX_HEADS * IDX_DIM
V_OFF = K_OFF + KV_WIDTH
PROJ_WIDTH = V_OFF + KV_WIDTH

IN_TN = 1024
ATTN_TQ = 256
ATTN_KC = 512
MASK_NEG = -1e30
MERGE_TM = 256
MOE_BM = 512
FFN1_TN = 512
FFN2_TN = 1024
DISPATCH_TM = 256
COMBINE_TM = 256
ROW_DMA_UNROLL = 8
FFN_X_SLOTS = 3
VMEM_LIMIT = 56 * 1024 * 1024


def _cparams(sem):
    return pltpu.CompilerParams(dimension_semantics=sem, vmem_limit_bytes=VMEM_LIMIT)


def _rms(x, g):
    return x * lax.rsqrt(jnp.mean(x * x, axis=-1, keepdims=True) + EPS) * g


def _pack_bf16_pairs(x):
    n = x.shape[1] // 2
    bits = lax.bitcast_convert_type(x.astype(BF16).astype(F32), U32)
    return bits[:, n:] | lax.shift_right_logical(bits[:, :n], np.uint32(16))


def _unpack_bf16_pairs(w):
    lo = lax.bitcast_convert_type(lax.shift_left(w, np.uint32(16)), F32)
    hi = lax.bitcast_convert_type(w & np.uint32(0xFFFF0000), F32)
    return lo, hi


def _ada_kernel(c_ref, w_ref, b_ref, o_ref):
    c = c_ref[...]
    ca = (c * jax.nn.sigmoid(c)).astype(BF16)
    o_ref[...] = jnp.dot(ca, w_ref[...].astype(BF16), preferred_element_type=F32) + b_ref[...]


def _ada(c, w_ada, b_ada):
    bsz = c.shape[0]
    n = w_ada.shape[1]
    tn = 1024
    return pl.pallas_call(
        _ada_kernel,
        grid=(n // tn,),
        in_specs=[
            pl.BlockSpec((bsz, D_MODEL), lambda j: (0, 0)),
            pl.BlockSpec((D_MODEL, tn), lambda j: (0, j)),
            pl.BlockSpec((1, tn), lambda j: (0, j)),
        ],
        out_specs=pl.BlockSpec((bsz, tn), lambda j: (0, j)),
        out_shape=jax.ShapeDtypeStruct((bsz, n), F32),
        compiler_params=_cparams(("arbitrary",)),
        name="ada",
    )(c, w_ada, b_ada.reshape(1, n))


def _rope_slice(xs, c, s1, s2, shift):
    return xs * c + pltpu.roll(xs, LANES - shift, 1) * s1 + pltpu.roll(xs, shift, 1) * s2


def _in_kernel(x_ref, mod_ref, g_ref, w_ref, ws_ref, tab_ref, o_ref, os_ref, h_ref, acc_ref):
    j = pl.program_id(1)
    half_a = ROT_DIM // 2
    half_i = IDX_ROT_DIM // 2

    @pl.when(j == 0)
    def _():
        m = mod_ref[0]
        h = _rms(x_ref[...], g_ref[...]) * (1.0 + m[1:2]) + m[0:1]
        hb = h.astype(BF16)
        h_ref[...] = hb
        small = jnp.dot(hb, ws_ref[...], preferred_element_type=F32)
        os_ref[...] = _rope_slice(small, tab_ref[6], tab_ref[7], tab_ref[8], half_i)

    acc_ref[...] = jnp.dot(h_ref[...], w_ref[...], preferred_element_type=F32)
    n_sl = IN_TN // LANES
    j_q0, j_q1 = Q_OFF // IN_TN, U_OFF // IN_TN
    j_qi = QI_OFF // IN_TN
    j_kv = K_OFF // IN_TN
    k_sl = KV_WIDTH // LANES

    @pl.when((j < j_q0) | (j == U_OFF // IN_TN))
    def _():
        o_ref[...] = acc_ref[...].astype(BF16)

    @pl.when((j >= j_q0) & (j < j_q1))
    def _():
        for s in range(n_sl):
            sl = slice(s * LANES, (s + 1) * LANES)
            o_ref[:, sl] = _rope_slice(acc_ref[:, sl], tab_ref[0], tab_ref[1], tab_ref[2], half_a).astype(BF16)

    @pl.when(j == j_qi)
    def _():
        for s in range(n_sl):
            sl = slice(s * LANES, (s + 1) * LANES)
            o_ref[:, sl] = _rope_slice(acc_ref[:, sl], tab_ref[3], tab_ref[4], tab_ref[5], half_i).astype(BF16)

    @pl.when(j == j_kv)
    def _():
        for s in range(k_sl):
            sl = slice(s * LANES, (s + 1) * LANES)
            o_ref[:, sl] = _rope_slice(acc_ref[:, sl], tab_ref[0], tab_ref[1], tab_ref[2], half_a).astype(BF16)
        o_ref[:, KV_WIDTH:] = acc_ref[:, KV_WIDTH:].astype(BF16)


def _rope_tables(seq):
    def tabs(rot_dim, period, n_rep):
        half = rot_dim // 2
        inv = np.float32(ROPE_THETA) ** (-np.arange(0, rot_dim, 2, dtype=np.float32) / np.float32(rot_dim))
        ang = np.arange(seq, dtype=np.float32)[:, None] * inv.astype(np.float32)[None, :]
        cos, sin = np.cos(ang).astype(np.float32), np.sin(ang).astype(np.float32)
        ones = np.ones((seq, period - 2 * half), np.float32)
        z_h = np.zeros((seq, half), np.float32)
        c = np.concatenate([cos, cos, ones], axis=1)
        s1 = np.concatenate([-sin, z_h, 0 * ones], axis=1)
        s2 = np.concatenate([z_h, sin, 0 * ones], axis=1)
        return [np.tile(t, (1, n_rep)) for t in (c, s1, s2)]

    ta = tabs(ROT_DIM, HEAD_DIM, LANES // HEAD_DIM)
    ti = tabs(IDX_ROT_DIM, IDX_DIM, LANES // IDX_DIM)
    ts = tabs(IDX_ROT_DIM, IDX_DIM, 1)
    wi_scale = np.float32(IDX_HEADS ** -0.5 * IDX_DIM ** -0.5)
    pad = LANES - IDX_DIM
    c_tail = np.where(np.arange(pad) < IDX_HEADS, wi_scale, np.float32(0.0)).astype(np.float32)
    ts = [
        np.concatenate([ts[0], np.broadcast_to(c_tail, (seq, pad))], axis=1),
        np.concatenate([ts[1], np.zeros((seq, pad), np.float32)], axis=1),
        np.concatenate([ts[2], np.zeros((seq, pad), np.float32)], axis=1),
    ]
    return jnp.asarray(np.stack(ta + ti + ts, axis=0))


def _in_proj(x2, mod3, g_pre, w_in, seq):
    m_rows = x2.shape[0]
    tm = min(1024, seq)
    assert seq % tm == 0 and m_rows % tm == 0
    per_seq = seq // tm
    offs = np.cumsum((POOL_WIDTH, ATTN_WIDTH, KV_WIDTH, KV_WIDTH, IDX_HEADS * IDX_DIM, IDX_DIM, IDX_HEADS))
    w_u, w_q, w_k, w_v, w_qi, w_ki, w_wi, w_gl = jnp.split(w_in, [int(o) for o in offs], axis=1)
    w_main = jnp.concatenate([w_gl, w_q, w_u, w_qi, w_k, w_v], axis=1).astype(BF16)
    w_small = jnp.concatenate(
        [w_ki, w_wi, jnp.zeros((D_MODEL, LANES - IDX_DIM - IDX_HEADS), F32)], axis=1
    ).astype(BF16)
    tabs = _rope_tables(seq)
    n_j = PROJ_WIDTH // IN_TN
    return pl.pallas_call(
        _in_kernel,
        grid=(m_rows // tm, n_j),
        in_specs=[
            pl.BlockSpec((tm, D_MODEL), lambda i, j: (i, 0)),
            pl.BlockSpec((1, N_MOD, D_MODEL), lambda i, j: (i // per_seq, 0, 0)),
            pl.BlockSpec((1, D_MODEL), lambda i, j: (0, 0)),
            pl.BlockSpec((D_MODEL, IN_TN), lambda i, j: (0, j)),
            pl.BlockSpec((D_MODEL, LANES), lambda i, j: (0, 0)),
            pl.BlockSpec((9, tm, LANES), lambda i, j: (0, i % per_seq, 0)),
        ],
        out_specs=[
            pl.BlockSpec((tm, IN_TN), lambda i, j: (i, j)),
            pl.BlockSpec((tm, LANES), lambda i, j: (i, 0)),
        ],
        out_shape=[
            jax.ShapeDtypeStruct((m_rows, PROJ_WIDTH), BF16),
            jax.ShapeDtypeStruct((m_rows, LANES), F32),
        ],
        scratch_shapes=[pltpu.VMEM((tm, D_MODEL), BF16), pltpu.VMEM((tm, IN_TN), F32)],
        compiler_params=_cparams(("arbitrary", "arbitrary")),
        name="in_proj",
    )(x2, mod3, g_pre.reshape(1, D_MODEL), w_main, w_small, tabs)


def _attn_kernel(q_ref, qi_ref, k_ref, v_ref, kis_ref, wis_ref, o_ref,
                 key_ref, plim_ref, qs_ref, m_ref, l_ref, acc_ref, *, tq, kc, seq, n_keep):
    i = pl.program_id(1)
    nk = ((i + 1) * tq + kc - 1) // kc
    nt = (((1,), (1,)), ((), ()))
    grp = N_HEADS // N_KV_HEADS
    row_io = lax.broadcasted_iota(I32, (kc, tq), 0)
    tpos = i * tq + lax.broadcasted_iota(I32, (kc, tq), 1)
    w_t = wis_ref[...].T

    def idx_chunk(c, carry):
        r0 = pl.multiple_of(c * kc, kc)
        ki = kis_ref[pl.ds(r0, kc), :IDX_DIM].astype(BF16)
        sc = jnp.zeros((kc, tq), F32)
        for h in range(IDX_HEADS):
            qh = qi_ref[:, h * IDX_DIM:(h + 1) * IDX_DIM]
            d = lax.dot_general(ki, qh, nt, preferred_element_type=F32)
            sc = sc + jnp.maximum(d, 0.0) * w_t[IDX_DIM + h:IDX_DIM + h + 1, :]
        bits = lax.bitcast_convert_type(sc, I32)
        key = jnp.where(bits < 0, bits ^ np.int32(0x7FFFFFFF), bits)
        key_ref[pl.ds(r0, kc), :] = jnp.where(r0 + row_io <= tpos, key, INT_MIN)
        return carry

    lax.fori_loop(0, nk, idx_chunk, 0)

    def count(pred):
        def body(c, acc):
            r0 = pl.multiple_of(c * kc, kc)
            hit = pred(key_ref[pl.ds(r0, kc), :], r0)
            return acc + jnp.sum(hit.astype(I32), axis=0, keepdims=True)

        return lax.fori_loop(0, nk, body, jnp.zeros((1, tq), I32))

    def bit_step(it, tu):
        cand = tu | lax.shift_left(np.int32(1), 31 - it)
        cnt = count(lambda kch, r0: kch >= (cand ^ INT_MIN))
        return jnp.where(cnt >= n_keep, cand, tu)

    tu = lax.fori_loop(0, 32, bit_step, jnp.zeros((1, tq), I32))
    thr = tu ^ INT_MIN
    cnt_gt = count(lambda kch, r0: kch > thr)
    cnt_ge = count(lambda kch, r0: kch >= thr)
    tie_i = ((cnt_ge > n_keep) & (tu != 0)).astype(I32)
    need = n_keep - cnt_gt

    plim_ref[...] = jnp.full(plim_ref.shape, seq, I32)

    @pl.when(jnp.max(tie_i) > 0)
    def _():
        n_bits = (seq - 1).bit_length()

        def idx_step(it, p):
            cand = p | lax.shift_left(np.int32(1), (n_bits - 1) - it)
            below = count(lambda kch, r0: (kch == thr) & (r0 + row_io < cand))
            return jnp.where(below < need, cand, p)

        p = lax.fori_loop(0, n_bits, idx_step, jnp.zeros((1, tq), I32))
        plim_ref[...] = jnp.broadcast_to(jnp.where(tie_i > 0, p, seq), plim_ref.shape)

    for g in range(N_KV_HEADS):
        for r in range(grp):
            hh = g * grp + r
            qs_ref[g, r * tq:(r + 1) * tq, :] = q_ref[:, hh * HEAD_DIM:(hh + 1) * HEAD_DIM]
    m_ref[...] = jnp.full(m_ref.shape, MASK_NEG, F32)
    l_ref[...] = jnp.zeros_like(l_ref)
    acc_ref[...] = jnp.zeros_like(acc_ref)
    c_exp = HEAD_DIM ** -0.5 * np.log2(np.e)

    def att_chunk(c, carry):
        r0 = pl.multiple_of(c * kc, kc)
        kch = key_ref[pl.ds(r0, kc), :]
        row = r0 + row_io
        sel = ((kch > thr) | ((kch == thr) & (row <= plim_ref[0:1, :]))) & (row <= tpos)
        bias_t = jnp.where(sel, 0.0, MASK_NEG).astype(F32).T
        for g in range(N_KV_HEADS):
            kg = k_ref[pl.ds(r0, kc), g * HEAD_DIM:(g + 1) * HEAD_DIM]
            vg = v_ref[pl.ds(r0, kc), g * HEAD_DIM:(g + 1) * HEAD_DIM]
            s = lax.dot_general(qs_ref[g], kg, nt, preferred_element_type=F32)
            s = (s.reshape(grp, tq, kc) + bias_t[None]).reshape(grp * tq, kc)
            m_old = m_ref[g]
            m_new = jnp.maximum(m_old, jnp.max(s, axis=-1, keepdims=True))
            alpha = jnp.exp2((m_old - m_new) * c_exp)
            p = jnp.exp2((s - m_new) * c_exp)
            l_ref[g] = alpha * l_ref[g] + jnp.sum(p, axis=-1, keepdims=True)
            acc_ref[g] = alpha * acc_ref[g] + jnp.dot(p.astype(BF16), vg, preferred_element_type=F32)
            m_ref[g] = m_new
        return carry

    lax.fori_loop(0, nk, att_chunk, 0)
    for g in range(N_KV_HEADS):
        o = acc_ref[g] / l_ref[g]
        for r in range(grp):
            hh = g * grp + r
            o_ref[:, hh * HEAD_DIM:(hh + 1) * HEAD_DIM] = o[r * tq:(r + 1) * tq, :].astype(BF16)


def _attn(proj, small, bsz, seq):
    tq = min(ATTN_TQ, seq)
    kc = min(ATTN_KC, seq)
    assert seq % tq == 0 and seq % kc == 0 and kc % tq == 0
    nq = seq // tq
    n_keep = min(TOPK_MAX, seq // 4)
    grp = N_HEADS // N_KV_HEADS
    kern = functools.partial(_attn_kernel, tq=tq, kc=kc, seq=seq, n_keep=n_keep)
    return pl.pallas_call(
        kern,
        grid=(bsz, nq),
        in_specs=[
            pl.BlockSpec((tq, ATTN_WIDTH), lambda b, i: (b * nq + i, Q_OFF // ATTN_WIDTH)),
            pl.BlockSpec((tq, IDX_HEADS * IDX_DIM), lambda b, i: (b * nq + i, QI_OFF // (IDX_HEADS * IDX_DIM))),
            pl.BlockSpec((seq, KV_WIDTH), lambda b, i: (b, K_OFF // KV_WIDTH)),
            pl.BlockSpec((seq, KV_WIDTH), lambda b, i: (b, V_OFF // KV_WIDTH)),
            pl.BlockSpec((seq, LANES), lambda b, i: (b, 0)),
            pl.BlockSpec((tq, LANES), lambda b, i: (b * nq + i, 0)),
        ],
        out_specs=pl.BlockSpec((tq, ATTN_WIDTH), lambda b, i: (b * nq + i, 0)),
        out_shape=jax.ShapeDtypeStruct((bsz * seq, ATTN_WIDTH), BF16),
        scratch_shapes=[
            pltpu.VMEM((seq, tq), I32),
            pltpu.VMEM((8, tq), I32),
            pltpu.VMEM((N_KV_HEADS, grp * tq, HEAD_DIM), BF16),
            pltpu.VMEM((N_KV_HEADS, grp * tq, 1), F32),
            pltpu.VMEM((N_KV_HEADS, grp * tq, 1), F32),
            pltpu.VMEM((N_KV_HEADS, grp * tq, HEAD_DIM), F32),
        ],
        compiler_params=_cparams(("arbitrary", "arbitrary")),
        name="attn",
    )(proj, proj, proj, proj, small, small)


def _merge_kernel(
    u_ref, halo_ref, ya_ref, gl_ref, x_ref, mod_ref, wgrp_ref, pscale_ref, wup_p_ref, wup_a_ref, wout_ref,
    gpost_ref, gffn_ref, wr_ref, br_ref,
    x1_ref, hp_ref, route_ref, cnt_ref,
    carry_ref, *, seq,
):
    tm = MERGE_TM
    i = pl.program_id(0)
    pos0 = (i * tm) % seq

    @pl.when(i == 0)
    def _():
        carry_ref[...] = jnp.zeros_like(carry_ref)

    halo_rows = halo_ref.shape[0]
    halo = jnp.where(pos0 == 0, 0.0, halo_ref[...].astype(F32))
    ext = jnp.concatenate([halo, u_ref[...].astype(F32)], axis=0)
    pos = pos0 + lax.broadcasted_iota(I32, (tm, 1), 0)
    ys = []
    for g, win in enumerate(POOL_WINDOWS):
        sl = slice(g * POOL_GROUP_DIM, (g + 1) * POOL_GROUP_DIM)
        e = ext[:, sl]
        acc = e
        span = 1
        while span < win:
            acc = acc + pltpu.roll(acc, span, 0)
            span *= 2
        cnt = jnp.minimum(pos + 1, win).astype(F32)
        mixed = (acc[halo_rows:] / cnt - e[halo_rows:]).astype(BF16)
        ys.append(jnp.dot(mixed, wgrp_ref[g], preferred_element_type=F32))
    y_pool = (jnp.concatenate(ys, axis=1) * pscale_ref[...]).astype(BF16)

    gates = jax.nn.sigmoid(gl_ref[...].astype(F32))
    up_p = jnp.dot(y_pool, wup_p_ref[...], preferred_element_type=F32)
    up_a = jnp.dot(ya_ref[...], wup_a_ref[...], preferred_element_type=F32)
    merged = (gates[:, :D_MODEL] * up_p + gates[:, D_MODEL:] * up_a).astype(BF16)
    y = jnp.dot(merged, wout_ref[...], preferred_element_type=F32)
    m = mod_ref[0]
    x1 = x_ref[...] + m[2:3] * _rms(y, gpost_ref[...])
    x1_ref[...] = x1

    h2f = _rms(x1, gffn_ref[...]) * (1.0 + m[4:5]) + m[3:4]
    h2 = h2f.astype(BF16)
    hp_ref[...] = _pack_bf16_pairs(h2f)

    lane = lax.broadcasted_iota(I32, (tm, LANES), 1)
    logits = jnp.dot(h2, wr_ref[...], preferred_element_type=F32) + br_ref[...]
    work = jnp.where(lane < N_EXPERTS, logits, -jnp.inf)
    vals, hots = [], []
    for _ in range(TOP_K):
        mx = jnp.max(work, axis=-1, keepdims=True)
        first = jnp.min(jnp.where(work == mx, lane, LANES), axis=-1, keepdims=True)
        hot = lane == first
        vals.append(mx)
        hots.append(hot)
        work = jnp.where(hot, -jnp.inf, work)
    ex = [jnp.exp(v - vals[0]) for v in vals]
    den = ex[0] + ex[1] + ex[2] + ex[3]

    onehot = jnp.zeros((tm, LANES), F32)
    for hot in hots:
        onehot = onehot + hot.astype(F32)
    r_i = lax.broadcasted_iota(I32, (tm, tm), 0)
    c_i = lax.broadcasted_iota(I32, (tm, tm), 1)
    tril = (c_i < r_i).astype(BF16)
    before = jnp.dot(tril, onehot.astype(BF16), preferred_element_type=F32) + carry_ref[0:1, :]
    out = jnp.zeros((tm, LANES), F32)
    lane_f = lane.astype(F32)
    for k in range(TOP_K):
        out = jnp.where(lane == k, ex[k] / den, out)
        idx_k = jnp.sum(jnp.where(hots[k], lane_f, 0.0), axis=-1, keepdims=True)
        out = jnp.where(lane == TOP_K + k, idx_k, out)
        rank_k = jnp.sum(jnp.where(hots[k], before, 0.0), axis=-1, keepdims=True)
        out = jnp.where(lane == 2 * TOP_K + k, rank_k, out)
    route_ref[...] = out
    total = carry_ref[0:1, :] + jnp.sum(onehot, axis=0, keepdims=True)
    carry_ref[...] = jnp.broadcast_to(total, carry_ref.shape)
    cnt_ref[...] = jnp.broadcast_to(total, cnt_ref.shape)


def _merge(proj, y_attn, x2, mod3, w_pool_grp, pool_scale, w_up_pool, w_up_attn, w_out, g_post_mix, g_pre_ffn,
           w_router, b_router, seq):
    m_rows = x2.shape[0]
    tm = MERGE_TM
    halo = 16
    assert seq % tm == 0 and max(POOL_WINDOWS) <= halo
    per_seq = seq // tm
    wr = jnp.concatenate([w_router, jnp.zeros((D_MODEL, LANES - N_EXPERTS), F32)], axis=1).astype(BF16)
    br = jnp.concatenate([b_router, jnp.zeros((LANES - N_EXPERTS,), F32)]).reshape(1, LANES)
    const = lambda shape: pl.BlockSpec(shape, lambda i: (0,) * len(shape), pipeline_mode=pl.Buffered(1))
    kern = functools.partial(_merge_kernel, seq=seq)
    return pl.pallas_call(
        kern,
        grid=(m_rows // tm,),
        in_specs=[
            pl.BlockSpec((tm, POOL_WIDTH), lambda i: (i, U_OFF // POOL_WIDTH)),
            pl.BlockSpec((halo, POOL_WIDTH), lambda i: (jnp.maximum(i * (tm // halo) - 1, 0), U_OFF // POOL_WIDTH)),
            pl.BlockSpec((tm, ATTN_WIDTH), lambda i: (i, 0)),
            pl.BlockSpec((tm, N_BRANCHES * D_MODEL), lambda i: (i, 0)),
            pl.BlockSpec((tm, D_MODEL), lambda i: (i, 0)),
            pl.BlockSpec((1, N_MOD, D_MODEL), lambda i: (i // per_seq, 0, 0)),
            const((POOL_GROUPS, POOL_GROUP_DIM, POOL_GROUP_DIM)),
            const((1, POOL_WIDTH)),
            const((POOL_WIDTH, D_MODEL)),
            const((ATTN_WIDTH, D_MODEL)),
            const((D_MODEL, D_MODEL)),
            const((1, D_MODEL)),
            const((1, D_MODEL)),
            const((D_MODEL, LANES)),
            const((1, LANES)),
        ],
        out_specs=[
            pl.BlockSpec((tm, D_MODEL), lambda i: (i, 0)),
            pl.BlockSpec((tm, D_MODEL // 2), lambda i: (i, 0)),
            pl.BlockSpec((tm, LANES), lambda i: (i, 0)),
            pl.BlockSpec((8, LANES), lambda i: (0, 0)),
        ],
        out_shape=[
            jax.ShapeDtypeStruct((m_rows, D_MODEL), F32),
            jax.ShapeDtypeStruct((m_rows, D_MODEL // 2), U32),
            jax.ShapeDtypeStruct((m_rows, LANES), F32),
            jax.ShapeDtypeStruct((8, LANES), F32),
        ],
        scratch_shapes=[pltpu.VMEM((8, LANES), F32)],
        compiler_params=_cparams(("arbitrary",)),
        name="merge",
    )(
        proj, proj, y_attn, proj, x2, mod3,
        w_pool_grp.astype(BF16), pool_scale.reshape(1, POOL_WIDTH), w_up_pool.astype(BF16),
        w_up_attn.astype(BF16), w_out.astype(BF16), g_post_mix.reshape(1, D_MODEL), g_pre_ffn.reshape(1, D_MODEL),
        wr, br,
    )


def _dispatch_kernel(zf_ref, pos_ref, h_ref, xs_ref, zbuf, sem, zsem):
    tm = h_ref.shape[0]
    bm = zbuf.shape[0]

    @pl.when(pl.program_id(0) == 0)
    def _():
        zbuf[...] = jnp.zeros_like(zbuf)

        def zero_copy(b):
            return pltpu.make_async_copy(zbuf, xs_ref.at[pl.ds(b * bm, bm), :], zsem)

        def z_issue(b, carry):
            pl.when(zf_ref[b] != 0)(lambda: zero_copy(b).start())
            return carry

        def z_drain(b, carry):
            pl.when(zf_ref[b] != 0)(lambda: zero_copy(b).wait())
            return carry

        lax.fori_loop(0, zf_ref.shape[0], z_issue, 0)
        lax.fori_loop(0, zf_ref.shape[0], z_drain, 0)

    def row_copy(t, k):
        return pltpu.make_async_copy(
            h_ref.at[pl.ds(t, 1), :], xs_ref.at[pl.ds(pos_ref[t * TOP_K + k], 1), :], sem
        )

    def issue(t, carry):
        for k in range(TOP_K):
            row_copy(t, k).start(priority=k % 2)
        return carry

    lax.fori_loop(0, tm, issue, 0, unroll=ROW_DMA_UNROLL)
    for _ in range(TOP_K):
        pltpu.make_async_copy(h_ref, xs_ref.at[pl.ds(0, tm), :], sem).wait()


def _dispatch(zero_flag, pos_flat, hp, n_rows):
    m_rows, width = hp.shape
    tm = DISPATCH_TM
    grid_spec = pltpu.PrefetchScalarGridSpec(
        num_scalar_prefetch=1,
        grid=(m_rows // tm,),
        in_specs=[
            pl.BlockSpec((tm * TOP_K,), lambda i, zf: (i,), memory_space=pltpu.SMEM),
            pl.BlockSpec((tm, width), lambda i, zf: (i, 0)),
        ],
        out_specs=pl.BlockSpec(memory_space=pl.ANY),
        scratch_shapes=[pltpu.VMEM((MOE_BM, width), U32), pltpu.SemaphoreType.DMA(()), pltpu.SemaphoreType.DMA(())],
    )
    return pl.pallas_call(
        _dispatch_kernel,
        grid_spec=grid_spec,
        out_shape=jax.ShapeDtypeStruct((n_rows, width), U32),
        compiler_params=_cparams(("arbitrary",)),
        name="dispatch",
    )(zero_flag, pos_flat, hp)


def _expert_rows(e, j, bs_ref, nb_ref, nu_ref, x_hbm, o_hbm, xbuf, obuf, xsem, osem, compute, prepare):
    n_x, bm = xbuf.shape[0], xbuf.shape[1]
    tw = obuf.shape[2]
    n_blocks = x_hbm.shape[0] // bm
    b0 = bs_ref[e]
    nb = nb_ref[e]
    n_used = nu_ref[0]

    def x_copy(blk, slot):
        return pltpu.make_async_copy(x_hbm.at[pl.ds(blk * bm, bm), :], xbuf.at[slot], xsem.at[slot])

    def o_copy(blk, slot):
        return pltpu.make_async_copy(obuf.at[slot], o_hbm.at[pl.ds(blk * bm, bm), pl.ds(j * tw, tw)], osem.at[slot])

    @pl.when(nb > 0)
    def _():
        prepare()

        def body(rb, carry):
            g = b0 + rb
            slot = g % 2
            ahead = n_x - 1

            @pl.when(g == 0)
            def _():
                for a in range(ahead):
                    pl.when(a < n_used)(lambda a=a: x_copy(a, a).start())

            x_copy(g, g % n_x).wait()

            @pl.when(g + ahead < n_used)
            def _():
                x_copy(g + ahead, (g + ahead) % n_x).start()

            @pl.when(g >= 2)
            def _():
                o_copy(g - 2, slot).wait()

            obuf[slot] = compute(xbuf[g % n_x])
            o_copy(g, slot).start()

            @pl.when(g == n_used - 1)
            def _():
                @pl.when(g >= 1)
                def _():
                    o_copy(g - 1, 1 - slot).wait()

                o_copy(g, slot).wait()

            return carry

        lax.fori_loop(0, nb, body, 0)

    @pl.when(e == N_EXPERTS - 1)
    def _():
        obuf[0] = jnp.zeros(obuf.shape[1:], obuf.dtype)

        def z_issue(blk, carry):
            o_copy(blk, 0).start()
            return carry

        def z_drain(blk, carry):
            o_copy(blk, 0).wait()
            return carry

        lax.fori_loop(nu_ref[0], n_blocks, z_issue, 0)
        lax.fori_loop(nu_ref[0], n_blocks, z_drain, 0)


def _ffn1_kernel(bs_ref, nb_ref, nu_ref, x_hbm, wg_ref, wl_ref, bg_ref, bl_ref, o_hbm,
                 wg_s, wl_s, xbuf, obuf, xsem, osem):
    j, e = pl.program_id(0), pl.program_id(1)
    half = D_MODEL // 2

    def prepare():
        wg_s[...] = wg_ref[0].astype(BF16)
        wl_s[...] = wl_ref[0].astype(BF16)

    def compute(xp):
        lo, hi = (v.astype(BF16) for v in _unpack_bf16_pairs(xp))

        def proj(w_s, b_ref):
            return (
                jnp.dot(lo, w_s[:half, :], preferred_element_type=F32)
                + jnp.dot(hi, w_s[half:, :], preferred_element_type=F32)
                + b_ref[0]
            )

        glu = jnp.minimum(proj(wg_s, bg_ref), SWIGLU_LIMIT)
        lin = jnp.clip(proj(wl_s, bl_ref), -SWIGLU_LIMIT, SWIGLU_LIMIT)
        return (glu * jax.nn.sigmoid(SWIGLU_ALPHA * glu) * (lin + 1.0)).astype(BF16)

    _expert_rows(e, j, bs_ref, nb_ref, nu_ref, x_hbm, o_hbm, xbuf, obuf, xsem, osem, compute, prepare)


def _ffn1(blk_start, blk_count, n_used, xs, w1, b1):
    n_rows = xs.shape[0]
    bm, tn = MOE_BM, FFN1_TN
    nj = D_FF // tn
    b1r = b1.reshape(N_EXPERTS, 1, 2 * D_FF)
    grid_spec = pltpu.PrefetchScalarGridSpec(
        num_scalar_prefetch=3,
        grid=(nj, N_EXPERTS),
        in_specs=[
            pl.BlockSpec(memory_space=pl.ANY),
            pl.BlockSpec((1, D_MODEL, tn), lambda j, e, *_: (e, 0, j)),
            pl.BlockSpec((1, D_MODEL, tn), lambda j, e, *_: (e, 0, nj + j)),
            pl.BlockSpec((1, 1, tn), lambda j, e, *_: (e, 0, j)),
            pl.BlockSpec((1, 1, tn), lambda j, e, *_: (e, 0, nj + j)),
        ],
        out_specs=pl.BlockSpec(memory_space=pl.ANY),
        scratch_shapes=[
            pltpu.VMEM((D_MODEL, tn), BF16),
            pltpu.VMEM((D_MODEL, tn), BF16),
            pltpu.VMEM((FFN_X_SLOTS, bm, D_MODEL // 2), U32),
            pltpu.VMEM((2, bm, tn), BF16),
            pltpu.SemaphoreType.DMA((FFN_X_SLOTS,)),
            pltpu.SemaphoreType.DMA((2,)),
        ],
    )
    return pl.pallas_call(
        _ffn1_kernel,
        grid_spec=grid_spec,
        out_shape=jax.ShapeDtypeStruct((n_rows, D_FF), BF16),
        compiler_params=_cparams(("arbitrary", "arbitrary")),
        name="ffn1",
    )(blk_start, blk_count, n_used, xs, w1, w1, b1r, b1r)


def _ffn2_kernel(bs_ref, nb_ref, nu_ref, a_hbm, w_ref, b_ref, o_hbm, w_s, xbuf, obuf, xsem, osem):
    j, e = pl.program_id(0), pl.program_id(1)

    def prepare():
        w_s[...] = w_ref[0].astype(BF16)

    def compute(a):
        return _pack_bf16_pairs(jnp.dot(a, w_s[...], preferred_element_type=F32) + b_ref[0])

    _expert_rows(e, j, bs_ref, nb_ref, nu_ref, a_hbm, o_hbm, xbuf, obuf, xsem, osem, compute, prepare)


def _ffn2(blk_start, blk_count, n_used, act, w2, b2):
    n_rows = act.shape[0]
    bm, tn = MOE_BM, FFN2_TN
    nj = D_MODEL // tn
    b2r = b2.reshape(N_EXPERTS, 1, D_MODEL)
    grid_spec = pltpu.PrefetchScalarGridSpec(
        num_scalar_prefetch=3,
        grid=(nj, N_EXPERTS),
        in_specs=[
            pl.BlockSpec(memory_space=pl.ANY),
            pl.BlockSpec((1, D_FF, tn), lambda j, e, *_: (e, 0, j)),
            pl.BlockSpec((1, 1, tn), lambda j, e, *_: (e, 0, j)),
        ],
        out_specs=pl.BlockSpec(memory_space=pl.ANY),
        scratch_shapes=[
            pltpu.VMEM((D_FF, tn), BF16),
            pltpu.VMEM((FFN_X_SLOTS, bm, D_FF), BF16),
            pltpu.VMEM((2, bm, tn // 2), U32),
            pltpu.SemaphoreType.DMA((FFN_X_SLOTS,)),
            pltpu.SemaphoreType.DMA((2,)),
        ],
    )
    return pl.pallas_call(
        _ffn2_kernel,
        grid_spec=grid_spec,
        out_shape=jax.ShapeDtypeStruct((n_rows, D_MODEL // 2), U32),
        compiler_params=_cparams(("arbitrary", "arbitrary")),
        name="ffn2",
    )(blk_start, blk_count, n_used, act, w2, b2r)


def _combine_kernel(pos_ref, ys_ref, route_ref, x1_ref, mod_ref, g_ref, o_ref, buf, sem):
    tm = x1_ref.shape[0]

    def row_copy(t, k):
        return pltpu.make_async_copy(
            ys_ref.at[pl.ds(pos_ref[t * TOP_K + k], 1), :], buf.at[k, pl.ds(t, 1), :], sem
        )

    def issue(t, carry):
        for k in range(TOP_K):
            row_copy(t, k).start(priority=k % 2)
        return carry

    lax.fori_loop(0, tm, issue, 0, unroll=ROW_DMA_UNROLL)
    for k in range(TOP_K):
        pltpu.make_async_copy(ys_ref.at[pl.ds(0, tm), :], buf.at[k], sem).wait()
    route = route_ref[...]
    y_lo, y_hi = (route[:, 0:1] * v for v in _unpack_bf16_pairs(buf[0]))
    for k in range(1, TOP_K):
        lo, hi = _unpack_bf16_pairs(buf[k])
        y_lo = y_lo + route[:, k:k + 1] * lo
        y_hi = y_hi + route[:, k:k + 1] * hi
    hw = FFN2_TN // 2
    parts = []
    for t in range(D_MODEL // FFN2_TN):
        parts += [y_lo[:, t * hw:(t + 1) * hw], y_hi[:, t * hw:(t + 1) * hw]]
    y = jnp.concatenate(parts, axis=1)
    m = mod_ref[0]
    o_ref[...] = x1_ref[...] + m[5:6] * _rms(y, g_ref[...])


def _combine(pos_flat, ys, route, x1, mod3, g_post_ffn, seq):
    m_rows = x1.shape[0]
    tm = COMBINE_TM
    per_seq = seq // tm
    return pl.pallas_call(
        _combine_kernel,
        grid=(m_rows // tm,),
        in_specs=[
            pl.BlockSpec((tm * TOP_K,), lambda i: (i,), memory_space=pltpu.SMEM),
            pl.BlockSpec(memory_space=pl.ANY),
            pl.BlockSpec((tm, LANES), lambda i: (i, 0)),
            pl.BlockSpec((tm, D_MODEL), lambda i: (i, 0)),
            pl.BlockSpec((1, N_MOD, D_MODEL), lambda i: (i // per_seq, 0, 0)),
            pl.BlockSpec((1, D_MODEL), lambda i: (0, 0)),
        ],
        out_specs=pl.BlockSpec((tm, D_MODEL), lambda i: (i, 0)),
        out_shape=jax.ShapeDtypeStruct((m_rows, D_MODEL), F32),
        scratch_shapes=[pltpu.VMEM((TOP_K, tm, D_MODEL // 2), U32), pltpu.SemaphoreType.DMA(())],
        compiler_params=_cparams(("arbitrary",)),
        name="combine",
    )(pos_flat, ys, route, x1, mod3, g_post_ffn.reshape(1, D_MODEL))


def _layer(x2, mod3, bsz, seq, g_pre_mix, g_post_mix, w_in, w_pool_grp, pool_scale, w_up_pool, w_up_attn, w_out,
           g_pre_ffn, g_post_ffn, w_router, b_router, w1, b1, w2, b2):
    m_rows = x2.shape[0]
    proj, small = _in_proj(x2, mod3, g_pre_mix, w_in, seq)
    y_attn = _attn(proj, small, bsz, seq)
    x1, hp, route, cnt = _merge(proj, y_attn, x2, mod3, w_pool_grp, pool_scale, w_up_pool, w_up_attn, w_out,
                                g_post_mix, g_pre_ffn, w_router, b_router, seq)

    bm = MOE_BM
    n_slots = m_rows * TOP_K
    n_blocks = -(-n_slots // bm) + N_EXPERTS
    idx = route[:, TOP_K:2 * TOP_K].astype(I32)
    rank = route[:, 2 * TOP_K:3 * TOP_K].astype(I32)
    counts = cnt[0, :N_EXPERTS].astype(I32)
    padded = (counts + bm - 1) // bm * bm
    pad_end = jnp.cumsum(padded)
    pad_start = pad_end - padded
    pos_flat = (pad_start[idx] + rank).reshape(-1)
    n_used = (pad_end[-1] // bm).astype(I32).reshape(1)
    blk_ids = jnp.arange(n_blocks, dtype=I32)
    blk_start = (pad_start // bm).astype(I32)
    blk_count = (padded // bm).astype(I32)
    last_of_expert = jnp.any((blk_ids[:, None] + 1) * bm == pad_end[None, :], axis=1)
    zero_flag = (last_of_expert | (blk_ids >= n_used[0])).astype(I32)

    xs = _dispatch(zero_flag, pos_flat, hp, n_blocks * bm)
    act = _ffn1(blk_start, blk_count, n_used, xs, w1, b1)
    ys = _ffn2(blk_start, blk_count, n_used, act, w2, b2)
    return _combine(pos_flat, ys, route, x1, mod3, g_post_ffn, seq)


def kernel(x, c, w_ada, b_ada, g_pre_mix, g_post_mix, w_in, w_pool_grp, pool_scale, w_up_pool, w_up_attn, w_out,
           g_pre_ffn, g_post_ffn, w_router, b_router, w1, b1, w2, b2):
    bsz, seq, d = x.shape
    assert d == D_MODEL
    depth = w_ada.shape[0]
    x2 = x.reshape(bsz * seq, d)
    for layer in range(depth):
        mod3 = _ada(c, w_ada[layer], b_ada[layer]).reshape(bsz, N_MOD, D_MODEL)
        x2 = _layer(x2, mod3, bsz, seq, g_pre_mix[layer], g_post_mix[layer], w_in[layer], w_pool_grp[layer],
                    pool_scale[layer], w_up_pool[layer], w_up_attn[layer], w_out[layer], g_pre_ffn[layer],
                    g_post_ffn[layer], w_router[layer], b_router[layer], w1[layer], b1[layer], w2[layer], b2[layer])
    return x2.reshape(bsz, seq, d)
```

```python
import functools

import jax
import jax.numpy as jnp
import numpy as np
from jax import lax
from jax.experimental import pallas as pl
from jax.experimental.pallas import tpu as pltpu

F32 = jnp.float32
BF16 = jnp.bfloat16
I32 = jnp.int32
U32 = jnp.uint32

D_MODEL = 2048
POOL_WINDOWS = (2, 4, 8, 16)
POOL_GROUPS = 4
POOL_GROUP_DIM = D_MODEL // 8
POOL_WIDTH = POOL_GROUPS * POOL_GROUP_DIM
N_HEADS = 16
N_KV_HEADS = 4
HEAD_DIM = 128
ATTN_WIDTH = N_HEADS * HEAD_DIM
KV_WIDTH = N_KV_HEADS * HEAD_DIM
ROT_DIM = HEAD_DIM // 4
IDX_HEADS = 16
IDX_DIM = 64
IDX_ROT_DIM = IDX_DIM // 4
TOPK_MAX = 256
ROPE_THETA = 500000.0
N_BRANCHES = 2
N_EXPERTS = 32
TOP_K = 4
D_FF = D_MODEL
SWIGLU_ALPHA = 1.702
SWIGLU_LIMIT = 7.0
N_MOD = 6
EPS = 1e-6

LANES = 128
INT_MIN = np.int32(-(2 ** 31))

GL_OFF = 0
Q_OFF = GL_OFF + N_BRANCHES * D_MODEL
U_OFF = Q_OFF + ATTN_WIDTH
QI_OFF = U_OFF + POOL_WIDTH
K_OFF = QI_OFF + IDX_HEADS * IDX_DIM
V_OFF = K_OFF + KV_WIDTH
PROJ_WIDTH = V_OFF + KV_WIDTH

IN_TN = 1024
ATTN_TQ = 256
ATTN_KC = 512
MASK_NEG = -1e30
MERGE_TM = 256
MOE_BM = 512
FFN1_TN = 512
FFN2_TN = 1024
DISPATCH_TM = 1024
COMBINE_TM = 512
ROW_DMA_UNROLL = 8
FFN_X_SLOTS = 3
VMEM_LIMIT = 56 * 1024 * 1024


def _cparams(sem):
    return pltpu.CompilerParams(dimension_semantics=sem, vmem_limit_bytes=VMEM_LIMIT)


def _rms(x, g):
    return x * lax.rsqrt(jnp.mean(x * x, axis=-1, keepdims=True) + EPS) * g


def _pack_bf16_pairs(x):
    n = x.shape[1] // 2
    bits = lax.bitcast_convert_type(x.astype(BF16).astype(F32), U32)
    return bits[:, n:] | lax.shift_right_logical(bits[:, :n], np.uint32(16))


def _unpack_bf16_pairs(w):
    lo = lax.bitcast_convert_type(lax.shift_left(w, np.uint32(16)), F32)
    hi = lax.bitcast_convert_type(w & np.uint32(0xFFFF0000), F32)
    return lo, hi


def _ada_kernel(c_ref, w_ref, b_ref, o_ref):
    c = c_ref[...]
    ca = (c * jax.nn.sigmoid(c)).astype(BF16)
    o_ref[...] = jnp.dot(ca, w_ref[...].astype(BF16), preferred_element_type=F32) + b_ref[...]


def _ada(c, w_ada, b_ada):
    bsz = c.shape[0]
    n = w_ada.shape[1]
    tn = 1024
    return pl.pallas_call(
        _ada_kernel,
        grid=(n // tn,),
        in_specs=[
            pl.BlockSpec((bsz, D_MODEL), lambda j: (0, 0)),
            pl.BlockSpec((D_MODEL, tn), lambda j: (0, j)),
            pl.BlockSpec((1, tn), lambda j: (0, j)),
        ],
        out_specs=pl.BlockSpec((bsz, tn), lambda j: (0, j)),
        out_shape=jax.ShapeDtypeStruct((bsz, n), F32),
        compiler_params=_cparams(("arbitrary",)),
        name="ada",
    )(c, w_ada, b_ada.reshape(1, n))


def _rope_slice(xs, c, s1, s2, shift):
    return xs * c + pltpu.roll(xs, LANES - shift, 1) * s1 + pltpu.roll(xs, shift, 1) * s2


def _in_kernel(x_ref, mod_ref, g_ref, w_ref, ws_ref, tab_ref, o_ref, os_ref, h_ref, acc_ref):
    j = pl.program_id(1)
    half_a = ROT_DIM // 2
    half_i = IDX_ROT_DIM // 2

    @pl.when(j == 0)
    def _():
        m = mod_ref[0]
        h = _rms(x_ref[...], g_ref[...]) * (1.0 + m[1:2]) + m[0:1]
        hb = h.astype(BF16)
        h_ref[...] = hb
        small = jnp.dot(hb, ws_ref[...], preferred_element_type=F32)
        os_ref[...] = _rope_slice(small, tab_ref[6], tab_ref[7], tab_ref[8], half_i)

    acc_ref[...] = jnp.dot(h_ref[...], w_ref[...], preferred_element_type=F32)
    n_sl = IN_TN // LANES
    j_q0, j_q1 = Q_OFF // IN_TN, U_OFF // IN_TN
    j_qi = QI_OFF // IN_TN
    j_kv = K_OFF // IN_TN
    k_sl = KV_WIDTH // LANES

    @pl.when((j < j_q0) | (j == U_OFF // IN_TN))
    def _():
        o_ref[...] = acc_ref[...].astype(BF16)

    @pl.when((j >= j_q0) & (j < j_q1))
    def _():
        for s in range(n_sl):
            sl = slice(s * LANES, (s + 1) * LANES)
            o_ref[:, sl] = _rope_slice(acc_ref[:, sl], tab_ref[0], tab_ref[1], tab_ref[2], half_a).astype(BF16)

    @pl.when(j == j_qi)
    def _():
        for s in range(n_sl):
            sl = slice(s * LANES, (s + 1) * LANES)
            o_ref[:, sl] = _rope_slice(acc_ref[:, sl], tab_ref[3], tab_ref[4], tab_ref[5], half_i).astype(BF16)

    @pl.when(j == j_kv)
    def _():
        for s in range(k_sl):
            sl = slice(s * LANES, (s + 1) * LANES)
            o_ref[:, sl] = _rope_slice(acc_ref[:, sl], tab_ref[0], tab_ref[1], tab_ref[2], half_a).astype(BF16)
        o_ref[:, KV_WIDTH:] = acc_ref[:, KV_WIDTH:].astype(BF16)


def _rope_tables(seq):
    def tabs(rot_dim, period, n_rep):
        half = rot_dim // 2
        inv = np.float32(ROPE_THETA) ** (-np.arange(0, rot_dim, 2, dtype=np.float32) / np.float32(rot_dim))
        ang = np.arange(seq, dtype=np.float32)[:, None] * inv.astype(np.float32)[None, :]
        cos, sin = np.cos(ang).astype(np.float32), np.sin(ang).astype(np.float32)
        ones = np.ones((seq, period - 2 * half), np.float32)
        z_h = np.zeros((seq, half), np.float32)
        c = np.concatenate([cos, cos, ones], axis=1)
        s1 = np.concatenate([-sin, z_h, 0 * ones], axis=1)
        s2 = np.concatenate([z_h, sin, 0 * ones], axis=1)
        return [np.tile(t, (1, n_rep)) for t in (c, s1, s2)]

    ta = tabs(ROT_DIM, HEAD_DIM, LANES // HEAD_DIM)
    ti = tabs(IDX_ROT_DIM, IDX_DIM, LANES // IDX_DIM)
    ts = tabs(IDX_ROT_DIM, IDX_DIM, 1)
    wi_scale = np.float32(IDX_HEADS ** -0.5 * IDX_DIM ** -0.5)
    pad = LANES - IDX_DIM
    c_tail = np.where(np.arange(pad) < IDX_HEADS, wi_scale, np.float32(0.0)).astype(np.float32)
    ts = [
        np.concatenate([ts[0], np.broadcast_to(c_tail, (seq, pad))], axis=1),
        np.concatenate([ts[1], np.zeros((seq, pad), np.float32)], axis=1),
        np.concatenate([ts[2], np.zeros((seq, pad), np.float32)], axis=1),
    ]
    return jnp.asarray(np.stack(ta + ti + ts, axis=0))


def _in_proj(x2, mod3, g_pre, w_in, seq):
    m_rows = x2.shape[0]
    tm = min(1024, seq)
    assert seq % tm == 0 and m_rows % tm == 0
    per_seq = seq // tm
    offs = np.cumsum((POOL_WIDTH, ATTN_WIDTH, KV_WIDTH, KV_WIDTH, IDX_HEADS * IDX_DIM, IDX_DIM, IDX_HEADS))
    w_u, w_q, w_k, w_v, w_qi, w_ki, w_wi, w_gl = jnp.split(w_in, [int(o) for o in offs], axis=1)
    w_main = jnp.concatenate([w_gl, w_q, w_u, w_qi, w_k, w_v], axis=1).astype(BF16)
    w_small = jnp.concatenate(
        [w_ki, w_wi, jnp.zeros((D_MODEL, LANES - IDX_DIM - IDX_HEADS), F32)], axis=1
    ).astype(BF16)
    tabs = _rope_tables(seq)
    n_j = PROJ_WIDTH // IN_TN
    return pl.pallas_call(
        _in_kernel,
        grid=(m_rows // tm, n_j),
        in_specs=[
            pl.BlockSpec((tm, D_MODEL), lambda i, j: (i, 0)),
            pl.BlockSpec((1, N_MOD, D_MODEL), lambda i, j: (i // per_seq, 0, 0)),
            pl.BlockSpec((1, D_MODEL), lambda i, j: (0, 0)),
            pl.BlockSpec((D_MODEL, IN_TN), lambda i, j: (0, j)),
            pl.BlockSpec((D_MODEL, LANES), lambda i, j: (0, 0)),
            pl.BlockSpec((9, tm, LANES), lambda i, j: (0, i % per_seq, 0)),
        ],
        out_specs=[
            pl.BlockSpec((tm, IN_TN), lambda i, j: (i, j)),
            pl.BlockSpec((tm, LANES), lambda i, j: (i, 0)),
        ],
        out_shape=[
            jax.ShapeDtypeStruct((m_rows, PROJ_WIDTH), BF16),
            jax.ShapeDtypeStruct((m_rows, LANES), F32),
        ],
        scratch_shapes=[pltpu.VMEM((tm, D_MODEL), BF16), pltpu.VMEM((tm, IN_TN), F32)],
        compiler_params=_cparams(("arbitrary", "arbitrary")),
        name="in_proj",
    )(x2, mod3, g_pre.reshape(1, D_MODEL), w_main, w_small, tabs)


def _attn_kernel(q_ref, qi_ref, k_ref, v_ref, kis_ref, wis_ref, o_ref,
                 key_ref, plim_ref, qs_ref, m_ref, l_ref, acc_ref, *, tq, kc, seq, n_keep):
    i = pl.program_id(1)
    nk = ((i + 1) * tq + kc - 1) // kc
    nt = (((1,), (1,)), ((), ()))
    grp = N_HEADS // N_KV_HEADS
    row_io = lax.broadcasted_iota(I32, (kc, tq), 0)
    tpos = i * tq + lax.broadcasted_iota(I32, (kc, tq), 1)
    w_t = wis_ref[...].T

    def idx_chunk(c, carry):
        r0 = pl.multiple_of(c * kc, kc)
        ki = kis_ref[pl.ds(r0, kc), :IDX_DIM].astype(BF16)
        sc = jnp.zeros((kc, tq), F32)
        for h in range(IDX_HEADS):
            qh = qi_ref[:, h * IDX_DIM:(h + 1) * IDX_DIM]
            d = lax.dot_general(ki, qh, nt, preferred_element_type=F32)
            sc = sc + jnp.maximum(d, 0.0) * w_t[IDX_DIM + h:IDX_DIM + h + 1, :]
        bits = lax.bitcast_convert_type(sc, I32)
        key = jnp.where(bits < 0, bits ^ np.int32(0x7FFFFFFF), bits)
        key_ref[pl.ds(r0, kc), :] = jnp.where(r0 + row_io <= tpos, key, INT_MIN)
        return carry

    lax.fori_loop(0, nk, idx_chunk, 0)

    def count(pred):
        def body(c, acc):
            r0 = pl.multiple_of(c * kc, kc)
            hit = pred(key_ref[pl.ds(r0, kc), :], r0)
            return acc + jnp.sum(hit.astype(I32), axis=0, keepdims=True)

        return lax.fori_loop(0, nk, body, jnp.zeros((1, tq), I32))

    def bit_step(it, tu):
        cand = tu | lax.shift_left(np.int32(1), 31 - it)
        cnt = count(lambda kch, r0: kch >= (cand ^ INT_MIN))
        return jnp.where(cnt >= n_keep, cand, tu)

    tu = lax.fori_loop(0, 32, bit_step, jnp.zeros((1, tq), I32))
    thr = tu ^ INT_MIN
    cnt_gt = count(lambda kch, r0: kch > thr)
    cnt_ge = count(lambda kch, r0: kch >= thr)
    tie_i = ((cnt_ge > n_keep) & (tu != 0)).astype(I32)
    need = n_keep - cnt_gt

    plim_ref[...] = jnp.full(plim_ref.shape, seq, I32)

    @pl.when(jnp.max(tie_i) > 0)
    def _():
        n_bits = (seq - 1).bit_length()

        def idx_step(it, p):
            cand = p | lax.shift_left(np.int32(1), (n_bits - 1) - it)
            below = count(lambda kch, r0: (kch == thr) & (r0 + row_io < cand))
            return jnp.where(below < need, cand, p)

        p = lax.fori_loop(0, n_bits, idx_step, jnp.zeros((1, tq), I32))
        plim_ref[...] = jnp.broadcast_to(jnp.where(tie_i > 0, p, seq), plim_ref.shape)

    for g in range(N_KV_HEADS):
        for r in range(grp):
            hh = g * grp + r
            qs_ref[g, r * tq:(r + 1) * tq, :] = q_ref[:, hh * HEAD_DIM:(hh + 1) * HEAD_DIM]
    m_ref[...] = jnp.full(m_ref.shape, MASK_NEG, F32)
    l_ref[...] = jnp.zeros_like(l_ref)
    acc_ref[...] = jnp.zeros_like(acc_ref)
    c_exp = HEAD_DIM ** -0.5 * np.log2(np.e)

    def att_chunk(c, carry):
        r0 = pl.multiple_of(c * kc, kc)
        kch = key_ref[pl.ds(r0, kc), :]
        row = r0 + row_io
        sel = ((kch > thr) | ((kch == thr) & (row <= plim_ref[0:1, :]))) & (row <= tpos)
        bias_t = jnp.where(sel, 0.0, MASK_NEG).astype(F32).T
        for g in range(N_KV_HEADS):
            kg = k_ref[pl.ds(r0, kc), g * HEAD_DIM:(g + 1) * HEAD_DIM]
            vg = v_ref[pl.ds(r0, kc), g * HEAD_DIM:(g + 1) * HEAD_DIM]
            s = lax.dot_general(qs_ref[g], kg, nt, preferred_element_type=F32)
            s = (s.reshape(grp, tq, kc) + bias_t[None]).reshape(grp * tq, kc)
            m_old = m_ref[g]
            m_new = jnp.maximum(m_old, jnp.max(s, axis=-1, keepdims=True))
            alpha = jnp.exp2((m_old - m_new) * c_exp)
            p = jnp.exp2((s - m_new) * c_exp)
            l_ref[g] = alpha * l_ref[g] + jnp.sum(p, axis=-1, keepdims=True)
            acc_ref[g] = alpha * acc_ref[g] + jnp.dot(p.astype(BF16), vg, preferred_element_type=F32)
            m_ref[g] = m_new
        return carry

    lax.fori_loop(0, nk, att_chunk, 0)
    for g in range(N_KV_HEADS):
        o = acc_ref[g] / l_ref[g]
        for r in range(grp):
            hh = g * grp + r
            o_ref[:, hh * HEAD_DIM:(hh + 1) * HEAD_DIM] = o[r * tq:(r + 1) * tq, :].astype(BF16)


def _attn(proj, small, bsz, seq):
    tq = min(ATTN_TQ, seq)
    kc = min(ATTN_KC, seq)
    assert seq % tq == 0 and seq % kc == 0 and kc % tq == 0
    nq = seq // tq
    n_keep = min(TOPK_MAX, seq // 4)
    grp = N_HEADS // N_KV_HEADS
    kern = functools.partial(_attn_kernel, tq=tq, kc=kc, seq=seq, n_keep=n_keep)
    return pl.pallas_call(
        kern,
        grid=(bsz, nq),
        in_specs=[
            pl.BlockSpec((tq, ATTN_WIDTH), lambda b, i: (b * nq + i, Q_OFF // ATTN_WIDTH)),
            pl.BlockSpec((tq, IDX_HEADS * IDX_DIM), lambda b, i: (b * nq + i, QI_OFF // (IDX_HEADS * IDX_DIM))),
            pl.BlockSpec((seq, KV_WIDTH), lambda b, i: (b, K_OFF // KV_WIDTH)),
            pl.BlockSpec((seq, KV_WIDTH), lambda b, i: (b, V_OFF // KV_WIDTH)),
            pl.BlockSpec((seq, LANES), lambda b, i: (b, 0)),
            pl.BlockSpec((tq, LANES), lambda b, i: (b * nq + i, 0)),
        ],
        out_specs=pl.BlockSpec((tq, ATTN_WIDTH), lambda b, i: (b * nq + i, 0)),
        out_shape=jax.ShapeDtypeStruct((bsz * seq, ATTN_WIDTH), BF16),
        scratch_shapes=[
            pltpu.VMEM((seq, tq), I32),
            pltpu.VMEM((8, tq), I32),
            pltpu.VMEM((N_KV_HEADS, grp * tq, HEAD_DIM), BF16),
            pltpu.VMEM((N_KV_HEADS, grp * tq, 1), F32),
            pltpu.VMEM((N_KV_HEADS, grp * tq, 1), F32),
            pltpu.VMEM((N_KV_HEADS, grp * tq, HEAD_DIM), F32),
        ],
        compiler_params=_cparams(("arbitrary", "arbitrary")),
        name="attn",
    )(proj, proj, proj, proj, small, small)


def _merge_kernel(
    u_ref, halo_ref, ya_ref, gl_ref, x_ref, mod_ref, wgrp_ref, pscale_ref, wup_p_ref, wup_a_ref, wout_ref,
    gpost_ref, gffn_ref, wr_ref, br_ref,
    x1_ref, hp_ref, route_ref, cnt_ref,
    carry_ref, *, seq,
):
    tm = MERGE_TM
    i = pl.program_id(0)
    pos0 = (i * tm) % seq

    @pl.when(i == 0)
    def _():
        carry_ref[...] = jnp.zeros_like(carry_ref)

    halo_rows = halo_ref.shape[0]
    halo = jnp.where(pos0 == 0, 0.0, halo_ref[...].astype(F32))
    ext = jnp.concatenate([halo, u_ref[...].astype(F32)], axis=0)
    pos = pos0 + lax.broadcasted_iota(I32, (tm, 1), 0)
    ys = []
    for g, win in enumerate(POOL_WINDOWS):
        sl = slice(g * POOL_GROUP_DIM, (g + 1) * POOL_GROUP_DIM)
        e = ext[:, sl]
        acc = e
        span = 1
        while span < win:
            acc = acc + pltpu.roll(acc, span, 0)
            span *= 2
        cnt = jnp.minimum(pos + 1, win).astype(F32)
        mixed = (acc[halo_rows:] / cnt - e[halo_rows:]).astype(BF16)
        ys.append(jnp.dot(mixed, wgrp_ref[g], preferred_element_type=F32))
    y_pool = (jnp.concatenate(ys, axis=1) * pscale_ref[...]).astype(BF16)

    gates = jax.nn.sigmoid(gl_ref[...].astype(F32))
    up_p = jnp.dot(y_pool, wup_p_ref[...], preferred_element_type=F32)
    up_a = jnp.dot(ya_ref[...], wup_a_ref[...], preferred_element_type=F32)
    merged = (gates[:, :D_MODEL] * up_p + gates[:, D_MODEL:] * up_a).astype(BF16)
    y = jnp.dot(merged, wout_ref[...], preferred_element_type=F32)
    m = mod_ref[0]
    x1 = x_ref[...] + m[2:3] * _rms(y, gpost_ref[...])
    x1_ref[...] = x1

    h2f = _rms(x1, gffn_ref[...]) * (1.0 + m[4:5]) + m[3:4]
    h2 = h2f.astype(BF16)
    hp_ref[...] = _pack_bf16_pairs(h2f)

    lane = lax.broadcasted_iota(I32, (tm, LANES), 1)
    logits = jnp.dot(h2, wr_ref[...], preferred_element_type=F32) + br_ref[...]
    work = jnp.where(lane < N_EXPERTS, logits, -jnp.inf)
    vals, hots = [], []
    for _ in range(TOP_K):
        mx = jnp.max(work, axis=-1, keepdims=True)
        first = jnp.min(jnp.where(work == mx, lane, LANES), axis=-1, keepdims=True)
        hot = lane == first
        vals.append(mx)
        hots.append(hot)
        work = jnp.where(hot, -jnp.inf, work)
    ex = [jnp.exp(v - vals[0]) for v in vals]
    den = ex[0] + ex[1] + ex[2] + ex[3]

    onehot = jnp.zeros((tm, LANES), F32)
    for hot in hots:
        onehot = onehot + hot.astype(F32)
    r_i = lax.broadcasted_iota(I32, (tm, tm), 0)
    c_i = lax.broadcasted_iota(I32, (tm, tm), 1)
    tril = (c_i < r_i).astype(BF16)
    before = jnp.dot(tril, onehot.astype(BF16), preferred_element_type=F32) + carry_ref[0:1, :]
    out = jnp.zeros((tm, LANES), F32)
    lane_f = lane.astype(F32)
    for k in range(TOP_K):
        out = jnp.where(lane == k, ex[k] / den, out)
        idx_k = jnp.sum(jnp.where(hots[k], lane_f, 0.0), axis=-1, keepdims=True)
        out = jnp.where(lane == TOP_K + k, idx_k, out)
        rank_k = jnp.sum(jnp.where(hots[k], before, 0.0), axis=-1, keepdims=True)
        out = jnp.where(lane == 2 * TOP_K + k, rank_k, out)
    route_ref[...] = out
    total = carry_ref[0:1, :] + jnp.sum(onehot, axis=0, keepdims=True)
    carry_ref[...] = jnp.broadcast_to(total, carry_ref.shape)
    cnt_ref[...] = jnp.broadcast_to(total, cnt_ref.shape)


def _merge(proj, y_attn, x2, mod3, w_pool_grp, pool_scale, w_up_pool, w_up_attn, w_out, g_post_mix, g_pre_ffn,
           w_router, b_router, seq):
    m_rows = x2.shape[0]
    tm = MERGE_TM
    halo = 16
    assert seq % tm == 0 and max(POOL_WINDOWS) <= halo
    per_seq = seq // tm
    wr = jnp.concatenate([w_router, jnp.zeros((D_MODEL, LANES - N_EXPERTS), F32)], axis=1).astype(BF16)
    br = jnp.concatenate([b_router, jnp.zeros((LANES - N_EXPERTS,), F32)]).reshape(1, LANES)
    const = lambda shape: pl.BlockSpec(shape, lambda i: (0,) * len(shape), pipeline_mode=pl.Buffered(1))
    kern = functools.partial(_merge_kernel, seq=seq)
    return pl.pallas_call(
        kern,
        grid=(m_rows // tm,),
        in_specs=[
            pl.BlockSpec((tm, POOL_WIDTH), lambda i: (i, U_OFF // POOL_WIDTH)),
            pl.BlockSpec((halo, POOL_WIDTH), lambda i: (jnp.maximum(i * (tm // halo) - 1, 0), U_OFF // POOL_WIDTH)),
            pl.BlockSpec((tm, ATTN_WIDTH), lambda i: (i, 0)),
            pl.BlockSpec((tm, N_BRANCHES * D_MODEL), lambda i: (i, 0)),
            pl.BlockSpec((tm, D_MODEL), lambda i: (i, 0)),
            pl.BlockSpec((1, N_MOD, D_MODEL), lambda i: (i // per_seq, 0, 0)),
            const((POOL_GROUPS, POOL_GROUP_DIM, POOL_GROUP_DIM)),
            const((1, POOL_WIDTH)),
            const((POOL_WIDTH, D_MODEL)),
            const((ATTN_WIDTH, D_MODEL)),
            const((D_MODEL, D_MODEL)),
            const((1, D_MODEL)),
            const((1, D_MODEL)),
            const((D_MODEL, LANES)),
            const((1, LANES)),
        ],
        out_specs=[
            pl.BlockSpec((tm, D_MODEL), lambda i: (i, 0)),
            pl.BlockSpec((tm, D_MODEL // 2), lambda i: (i, 0)),
            pl.BlockSpec((tm, LANES), lambda i: (i, 0)),
            pl.BlockSpec((8, LANES), lambda i: (0, 0)),
        ],
        out_shape=[
            jax.ShapeDtypeStruct((m_rows, D_MODEL), F32),
            jax.ShapeDtypeStruct((m_rows, D_MODEL // 2), U32),
            jax.ShapeDtypeStruct((m_rows, LANES), F32),
            jax.ShapeDtypeStruct((8, LANES), F32),
        ],
        scratch_shapes=[pltpu.VMEM((8, LANES), F32)],
        compiler_params=_cparams(("arbitrary",)),
        name="merge",
    )(
        proj, proj, y_attn, proj, x2, mod3,
        w_pool_grp.astype(BF16), pool_scale.reshape(1, POOL_WIDTH), w_up_pool.astype(BF16),
        w_up_attn.astype(BF16), w_out.astype(BF16), g_post_mix.reshape(1, D_MODEL), g_pre_ffn.reshape(1, D_MODEL),
        wr, br,
    )


def _dispatch_kernel(zf_ref, pos_ref, h_ref, xs_ref, zbuf, sem, zsem):
    tm = h_ref.shape[0]
    bm = zbuf.shape[0]

    @pl.when(pl.program_id(0) == 0)
    def _():
        zbuf[...] = jnp.zeros_like(zbuf)

        def zero_copy(b):
            return pltpu.make_async_copy(zbuf, xs_ref.at[pl.ds(b * bm, bm), :], zsem)

        def z_issue(b, carry):
            pl.when(zf_ref[b] != 0)(lambda: zero_copy(b).start())
            return carry

        def z_drain(b, carry):
            pl.when(zf_ref[b] != 0)(lambda: zero_copy(b).wait())
            return carry

        lax.fori_loop(0, zf_ref.shape[0], z_issue, 0)
        lax.fori_loop(0, zf_ref.shape[0], z_drain, 0)

    def row_copy(t, k):
        return pltpu.make_async_copy(
            h_ref.at[pl.ds(t, 1), :], xs_ref.at[pl.ds(pos_ref[t * TOP_K + k], 1), :], sem
        )

    def issue(t, carry):
        for k in range(TOP_K):
            row_copy(t, k).start(priority=k % 2)
        return carry

    lax.fori_loop(0, tm, issue, 0, unroll=ROW_DMA_UNROLL)
    for _ in range(TOP_K):
        pltpu.make_async_copy(h_ref, xs_ref.at[pl.ds(0, tm), :], sem).wait()


def _dispatch(zero_flag, pos_flat, hp, n_rows):
    m_rows, width = hp.shape
    tm = min(DISPATCH_TM, m_rows)
    assert m_rows % tm == 0
    grid_spec = pltpu.PrefetchScalarGridSpec(
        num_scalar_prefetch=1,
        grid=(m_rows // tm,),
        in_specs=[
            pl.BlockSpec((tm * TOP_K,), lambda i, zf: (i,), memory_space=pltpu.SMEM),
            pl.BlockSpec((tm, width), lambda i, zf: (i, 0)),
        ],
        out_specs=pl.BlockSpec(memory_space=pl.ANY),
        scratch_shapes=[pltpu.VMEM((MOE_BM, width), U32), pltpu.SemaphoreType.DMA(()), pltpu.SemaphoreType.DMA(())],
    )
    return pl.pallas_call(
        _dispatch_kernel,
        grid_spec=grid_spec,
        out_shape=jax.ShapeDtypeStruct((n_rows, width), U32),
        compiler_params=_cparams(("arbitrary",)),
        name="dispatch",
    )(zero_flag, pos_flat, hp)


def _expert_rows(e, j, bs_ref, nb_ref, nu_ref, x_hbm, o_hbm, xbuf, obuf, xsem, osem, compute, prepare):
    n_x, bm = xbuf.shape[0], xbuf.shape[1]
    tw = obuf.shape[2]
    n_blocks = x_hbm.shape[0] // bm
    b0 = bs_ref[e]
    nb = nb_ref[e]
    n_used = nu_ref[0]

    def x_copy(blk, slot):
        return pltpu.make_async_copy(x_hbm.at[pl.ds(blk * bm, bm), :], xbuf.at[slot], xsem.at[slot])

    def o_copy(blk, slot):
        return pltpu.make_async_copy(obuf.at[slot], o_hbm.at[pl.ds(blk * bm, bm), pl.ds(j * tw, tw)], osem.at[slot])

    @pl.when(nb > 0)
    def _():
        prepare()

        def body(rb, carry):
            g = b0 + rb
            slot = g % 2
            ahead = n_x - 1

            @pl.when(g == 0)
            def _():
                for a in range(ahead):
                    pl.when(a < n_used)(lambda a=a: x_copy(a, a).start())

            x_copy(g, g % n_x).wait()

            @pl.when(g + ahead < n_used)
            def _():
                x_copy(g + ahead, (g + ahead) % n_x).start()

            @pl.when(g >= 2)
            def _():
                o_copy(g - 2, slot).wait()

            obuf[slot] = compute(xbuf[g % n_x])
            o_copy(g, slot).start()

            @pl.when(g == n_used - 1)
            def _():
                @pl.when(g >= 1)
                def _():
                    o_copy(g - 1, 1 - slot).wait()

                o_copy(g, slot).wait()

            return carry

        lax.fori_loop(0, nb, body, 0)

    @pl.when(e == N_EXPERTS - 1)
    def _():
        obuf[0] = jnp.zeros(obuf.shape[1:], obuf.dtype)

        def z_issue(blk, carry):
            o_copy(blk, 0).start()
            return carry

        def z_drain(blk, carry):
            o_copy(blk, 0).wait()
            return carry

        lax.fori_loop(nu_ref[0], n_blocks, z_issue, 0)
        lax.fori_loop(nu_ref[0], n_blocks, z_drain, 0)


def _ffn1_kernel(bs_ref, nb_ref, nu_ref, x_hbm, wg_ref, wl_ref, bg_ref, bl_ref, o_hbm,
                 wg_s, wl_s, xbuf, obuf, xsem, osem):
    j, e = pl.program_id(0), pl.program_id(1)
    half = D_MODEL // 2

    def prepare():
        wg_s[...] = wg_ref[0].astype(BF16)
        wl_s[...] = wl_ref[0].astype(BF16)

    def compute(xp):
        lo, hi = (v.astype(BF16) for v in _unpack_bf16_pairs(xp))

        def proj(w_s, b_ref):
            return (
                jnp.dot(lo, w_s[:half, :], preferred_element_type=F32)
                + jnp.dot(hi, w_s[half:, :], preferred_element_type=F32)
                + b_ref[0]
            )

        glu = jnp.minimum(proj(wg_s, bg_ref), SWIGLU_LIMIT)
        lin = jnp.clip(proj(wl_s, bl_ref), -SWIGLU_LIMIT, SWIGLU_LIMIT)
        return (glu * jax.nn.sigmoid(SWIGLU_ALPHA * glu) * (lin + 1.0)).astype(BF16)

    _expert_rows(e, j, bs_ref, nb_ref, nu_ref, x_hbm, o_hbm, xbuf, obuf, xsem, osem, compute, prepare)


def _ffn1(blk_start, blk_count, n_used, xs, w1, b1):
    n_rows = xs.shape[0]
    bm, tn = MOE_BM, FFN1_TN
    nj = D_FF // tn
    b1r = b1.reshape(N_EXPERTS, 1, 2 * D_FF)
    grid_spec = pltpu.PrefetchScalarGridSpec(
        num_scalar_prefetch=3,
        grid=(nj, N_EXPERTS),
        in_specs=[
            pl.BlockSpec(memory_space=pl.ANY),
            pl.BlockSpec((1, D_MODEL, tn), lambda j, e, *_: (e, 0, j)),
            pl.BlockSpec((1, D_MODEL, tn), lambda j, e, *_: (e, 0, nj + j)),
            pl.BlockSpec((1, 1, tn), lambda j, e, *_: (e, 0, j)),
            pl.BlockSpec((1, 1, tn), lambda j, e, *_: (e, 0, nj + j)),
        ],
        out_specs=pl.BlockSpec(memory_space=pl.ANY),
        scratch_shapes=[
            pltpu.VMEM((D_MODEL, tn), BF16),
            pltpu.VMEM((D_MODEL, tn), BF16),
            pltpu.VMEM((FFN_X_SLOTS, bm, D_MODEL // 2), U32),
            pltpu.VMEM((2, bm, tn), BF16),
            pltpu.SemaphoreType.DMA((FFN_X_SLOTS,)),
            pltpu.SemaphoreType.DMA((2,)),
        ],
    )
    return pl.pallas_call(
        _ffn1_kernel,
        grid_spec=grid_spec,
        out_shape=jax.ShapeDtypeStruct((n_rows, D_FF), BF16),
        compiler_params=_cparams(("arbitrary", "arbitrary")),
        name="ffn1",
    )(blk_start, blk_count, n_used, xs, w1, w1, b1r, b1r)


def _ffn2_kernel(bs_ref, nb_ref, nu_ref, a_hbm, w_ref, b_ref, o_hbm, w_s, xbuf, obuf, xsem, osem):
    j, e = pl.program_id(0), pl.program_id(1)

    def prepare():
        w_s[...] = w_ref[0].astype(BF16)

    def compute(a):
        return _pack_bf16_pairs(jnp.dot(a, w_s[...], preferred_element_type=F32) + b_ref[0])

    _expert_rows(e, j, bs_ref, nb_ref, nu_ref, a_hbm, o_hbm, xbuf, obuf, xsem, osem, compute, prepare)


def _ffn2(blk_start, blk_count, n_used, act, w2, b2):
    n_rows = act.shape[0]
    bm, tn = MOE_BM, FFN2_TN
    nj = D_MODEL // tn
    b2r = b2.reshape(N_EXPERTS, 1, D_MODEL)
    grid_spec = pltpu.PrefetchScalarGridSpec(
        num_scalar_prefetch=3,
        grid=(nj, N_EXPERTS),
        in_specs=[
            pl.BlockSpec(memory_space=pl.ANY),
            pl.BlockSpec((1, D_FF, tn), lambda j, e, *_: (e, 0, j)),
            pl.BlockSpec((1, 1, tn), lambda j, e, *_: (e, 0, j)),
        ],
        out_specs=pl.BlockSpec(memory_space=pl.ANY),
        scratch_shapes=[
            pltpu.VMEM((D_FF, tn), BF16),
            pltpu.VMEM((FFN_X_SLOTS, bm, D_FF), BF16),
            pltpu.VMEM((2, bm, tn // 2), U32),
            pltpu.SemaphoreType.DMA((FFN_X_SLOTS,)),
            pltpu.SemaphoreType.DMA((2,)),
        ],
    )
    return pl.pallas_call(
        _ffn2_kernel,
        grid_spec=grid_spec,
        out_shape=jax.ShapeDtypeStruct((n_rows, D_MODEL // 2), U32),
        compiler_params=_cparams(("arbitrary", "arbitrary")),
        name="ffn2",
    )(blk_start, blk_count, n_used, act, w2, b2r)


def _combine_kernel(pos_ref, ys_ref, route_ref, x1_ref, mod_ref, g_ref, o_ref, buf, sem):
    tm = x1_ref.shape[0]

    def row_copy(t, k):
        return pltpu.make_async_copy(
            ys_ref.at[pl.ds(pos_ref[t * TOP_K + k], 1), :], buf.at[k, pl.ds(t, 1), :], sem
        )

    def issue(t, carry):
        for k in range(TOP_K):
            row_copy(t, k).start(priority=k % 2)
        return carry

    lax.fori_loop(0, tm, issue, 0, unroll=ROW_DMA_UNROLL)
    for k in range(TOP_K):
        pltpu.make_async_copy(ys_ref.at[pl.ds(0, tm), :], buf.at[k], sem).wait()
    route = route_ref[...]
    y_lo, y_hi = (route[:, 0:1] * v for v in _unpack_bf16_pairs(buf[0]))
    for k in range(1, TOP_K):
        lo, hi = _unpack_bf16_pairs(buf[k])
        y_lo = y_lo + route[:, k:k + 1] * lo
        y_hi = y_hi + route[:, k:k + 1] * hi
    hw = FFN2_TN // 2
    parts = []
    for t in range(D_MODEL // FFN2_TN):
        parts += [y_lo[:, t * hw:(t + 1) * hw], y_hi[:, t * hw:(t + 1) * hw]]
    y = jnp.concatenate(parts, axis=1)
    m = mod_ref[0]
    o_ref[...] = x1_ref[...] + m[5:6] * _rms(y, g_ref[...])


def _combine(pos_flat, ys, route, x1, mod3, g_post_ffn, seq):
    m_rows = x1.shape[0]
    tm = min(COMBINE_TM, seq)
    assert seq % tm == 0
    per_seq = seq // tm
    return pl.pallas_call(
        _combine_kernel,
        grid=(m_rows // tm,),
        in_specs=[
            pl.BlockSpec((tm * TOP_K,), lambda i: (i,), memory_space=pltpu.SMEM),
            pl.BlockSpec(memory_space=pl.ANY),
            pl.BlockSpec((tm, LANES), lambda i: (i, 0)),
            pl.BlockSpec((tm, D_MODEL), lambda i: (i, 0)),
            pl.BlockSpec((1, N_MOD, D_MODEL), lambda i: (i // per_seq, 0, 0)),
            pl.BlockSpec((1, D_MODEL), lambda i: (0, 0)),
        ],
        out_specs=pl.BlockSpec((tm, D_MODEL), lambda i: (i, 0)),
        out_shape=jax.ShapeDtypeStruct((m_rows, D_MODEL), F32),
        scratch_shapes=[pltpu.VMEM((TOP_K, tm, D_MODEL // 2), U32), pltpu.SemaphoreType.DMA(())],
        compiler_params=_cparams(("arbitrary",)),
        name="combine",
    )(pos_flat, ys, route, x1, mod3, g_post_ffn.reshape(1, D_MODEL))


def _layer(x2, mod3, bsz, seq, g_pre_mix, g_post_mix, w_in, w_pool_grp, pool_scale, w_up_pool, w_up_attn, w_out,
           g_pre_ffn, g_post_ffn, w_router, b_router, w1, b1, w2, b2):
    m_rows = x2.shape[0]
    proj, small = _in_proj(x2, mod3, g_pre_mix, w_in, seq)
    y_attn = _attn(proj, small, bsz, seq)
    x1, hp, route, cnt = _merge(proj, y_attn, x2, mod3, w_pool_grp, pool_scale, w_up_pool, w_up_attn, w_out,
                                g_post_mix, g_pre_ffn, w_router, b_router, seq)

    bm = MOE_BM
    n_slots = m_rows * TOP_K
    n_blocks = -(-n_slots // bm) + N_EXPERTS
    idx = route[:, TOP_K:2 * TOP_K].astype(I32)
    rank = route[:, 2 * TOP_K:3 * TOP_K].astype(I32)
    counts = cnt[0, :N_EXPERTS].astype(I32)
    padded = (counts + bm - 1) // bm * bm
    pad_end = jnp.cumsum(padded)
    pad_start = pad_end - padded
    pos_flat = (pad_start[idx] + rank).reshape(-1)
    n_used = (pad_end[-1] // bm).astype(I32).reshape(1)
    blk_ids = jnp.arange(n_blocks, dtype=I32)
    blk_start = (pad_start // bm).astype(I32)
    blk_count = (padded // bm).astype(I32)
    last_of_expert = jnp.any((blk_ids[:, None] + 1) * bm == pad_end[None, :], axis=1)
    zero_flag = (last_of_expert | (blk_ids >= n_used[0])).astype(I32)

    xs = _dispatch(zero_flag, pos_flat, hp, n_blocks * bm)
    act = _ffn1(blk_start, blk_count, n_used, xs, w1, b1)
    ys = _ffn2(blk_start, blk_count, n_used, act, w2, b2)
    return _combine(pos_flat, ys, route, x1, mod3, g_post_ffn, seq)


def kernel(x, c, w_ada, b_ada, g_pre_mix, g_post_mix, w_in, w_pool_grp, pool_scale, w_up_pool, w_up_attn, w_out,
           g_pre_ffn, g_post_ffn, w_router, b_router, w1, b1, w2, b2):
    bsz, seq, d = x.shape
    assert d == D_MODEL
    depth = w_ada.shape[0]
    x2 = x.reshape(bsz * seq, d)
    for layer in range(depth):
        mod3 = _ada(c, w_ada[layer], b_ada[layer]).reshape(bsz, N_MOD, D_MODEL)
        x2 = _layer(x2, mod3, bsz, seq, g_pre_mix[layer], g_post_mix[layer], w_in[layer], w_pool_grp[layer],
                    pool_scale[layer], w_up_pool[layer], w_up_attn[layer], w_out[layer], g_pre_ffn[layer],
                    g_post_ffn[layer], w_router[layer], b_router[layer], w1[layer], b1[layer], w2[layer], b2[layer])
    return x2.reshape(bsz, seq, d)
```

```python
import functools

import jax
import jax.numpy as jnp
import numpy as np
from jax import lax
from jax.experimental import pallas as pl
from jax.experimental.pallas import tpu as pltpu

F32 = jnp.float32
BF16 = jnp.bfloat16
I32 = jnp.int32
U32 = jnp.uint32

D_MODEL = 2048
POOL_WINDOWS = (2, 4, 8, 16)
POOL_GROUPS = 4
POOL_GROUP_DIM = D_MODEL // 8
POOL_WIDTH = POOL_GROUPS * POOL_GROUP_DIM
N_HEADS = 16
N_KV_HEADS = 4
HEAD_DIM = 128
ATTN_WIDTH = N_HEADS * HEAD_DIM
KV_WIDTH = N_KV_HEADS * HEAD_DIM
ROT_DIM = HEAD_DIM // 4
IDX_HEADS = 16
IDX_DIM = 64
IDX_ROT_DIM = IDX_DIM // 4
TOPK_MAX = 256
ROPE_THETA = 500000.0
N_BRANCHES = 2
N_EXPERTS = 32
TOP_K = 4
D_FF = D_MODEL
SWIGLU_ALPHA = 1.702
SWIGLU_LIMIT = 7.0
N_MOD = 6
EPS = 1e-6

LANES = 128
INT_MIN = np.int32(-(2 ** 31))

GL_OFF = 0
Q_OFF = GL_OFF + N_BRANCHES * D_MODEL
U_OFF = Q_OFF + ATTN_WIDTH
QI_OFF = U_OFF + POOL_WIDTH
K_OFF = QI_OFF + IDX_HEADS * IDX_DIM
V_OFF = K_OFF + KV_WIDTH
PROJ_WIDTH = V_OFF + KV_WIDTH

IN_TN = 1024
ATTN_TQ = 512
ATTN_KC = 512
MASK_NEG = -1e30
MERGE_TM = 256
MOE_BM = 512
FFN1_TN = 512
FFN2_TN = 1024
DISPATCH_TM = 1024
COMBINE_TM = 512
ROW_DMA_UNROLL = 8
FFN_X_SLOTS = 3
VMEM_LIMIT = 56 * 1024 * 1024


def _cparams(sem):
    return pltpu.CompilerParams(dimension_semantics=sem, vmem_limit_bytes=VMEM_LIMIT)


def _rms(x, g):
    return x * lax.rsqrt(jnp.mean(x * x, axis=-1, keepdims=True) + EPS) * g


def _pack_bf16_pairs(x):
    n = x.shape[1] // 2
    bits = lax.bitcast_convert_type(x.astype(BF16).astype(F32), U32)
    return bits[:, n:] | lax.shift_right_logical(bits[:, :n], np.uint32(16))


def _unpack_bf16_pairs(w):
    lo = lax.bitcast_convert_type(lax.shift_left(w, np.uint32(16)), F32)
    hi = lax.bitcast_convert_type(w & np.uint32(0xFFFF0000), F32)
    return lo, hi


def _ada_kernel(c_ref, w_ref, b_ref, o_ref):
    c = c_ref[...]
    ca = (c * jax.nn.sigmoid(c)).astype(BF16)
    o_ref[...] = jnp.dot(ca, w_ref[...].astype(BF16), preferred_element_type=F32) + b_ref[...]


def _ada(c, w_ada, b_ada):
    bsz = c.shape[0]
    n = w_ada.shape[1]
    tn = 1024
    return pl.pallas_call(
        _ada_kernel,
        grid=(n // tn,),
        in_specs=[
            pl.BlockSpec((bsz, D_MODEL), lambda j: (0, 0)),
            pl.BlockSpec((D_MODEL, tn), lambda j: (0, j)),
            pl.BlockSpec((1, tn), lambda j: (0, j)),
        ],
        out_specs=pl.BlockSpec((bsz, tn), lambda j: (0, j)),
        out_shape=jax.ShapeDtypeStruct((bsz, n), F32),
        compiler_params=_cparams(("arbitrary",)),
        name="ada",
    )(c, w_ada, b_ada.reshape(1, n))


def _rope_slice(xs, c, s1, s2, shift):
    return xs * c + pltpu.roll(xs, LANES - shift, 1) * s1 + pltpu.roll(xs, shift, 1) * s2


def _in_kernel(x_ref, mod_ref, g_ref, w_ref, ws_ref, tab_ref, o_ref, os_ref, h_ref, acc_ref):
    j = pl.program_id(1)
    half_a = ROT_DIM // 2
    half_i = IDX_ROT_DIM // 2

    @pl.when(j == 0)
    def _():
        m = mod_ref[0]
        h = _rms(x_ref[...], g_ref[...]) * (1.0 + m[1:2]) + m[0:1]
        hb = h.astype(BF16)
        h_ref[...] = hb
        small = jnp.dot(hb, ws_ref[...], preferred_element_type=F32)
        os_ref[...] = _rope_slice(small, tab_ref[6], tab_ref[7], tab_ref[8], half_i)

    acc_ref[...] = jnp.dot(h_ref[...], w_ref[...], preferred_element_type=F32)
    n_sl = IN_TN // LANES
    j_q0, j_q1 = Q_OFF // IN_TN, U_OFF // IN_TN
    j_qi = QI_OFF // IN_TN
    j_kv = K_OFF // IN_TN
    k_sl = KV_WIDTH // LANES

    @pl.when((j < j_q0) | (j == U_OFF // IN_TN))
    def _():
        o_ref[...] = acc_ref[...].astype(BF16)

    @pl.when((j >= j_q0) & (j < j_q1))
    def _():
        for s in range(n_sl):
            sl = slice(s * LANES, (s + 1) * LANES)
            o_ref[:, sl] = _rope_slice(acc_ref[:, sl], tab_ref[0], tab_ref[1], tab_ref[2], half_a).astype(BF16)

    @pl.when(j == j_qi)
    def _():
        for s in range(n_sl):
            sl = slice(s * LANES, (s + 1) * LANES)
            o_ref[:, sl] = _rope_slice(acc_ref[:, sl], tab_ref[3], tab_ref[4], tab_ref[5], half_i).astype(BF16)

    @pl.when(j == j_kv)
    def _():
        for s in range(k_sl):
            sl = slice(s * LANES, (s + 1) * LANES)
            o_ref[:, sl] = _rope_slice(acc_ref[:, sl], tab_ref[0], tab_ref[1], tab_ref[2], half_a).astype(BF16)
        o_ref[:, KV_WIDTH:] = acc_ref[:, KV_WIDTH:].astype(BF16)


def _rope_tables(seq):
    def tabs(rot_dim, period, n_rep):
        half = rot_dim // 2
        inv = np.float32(ROPE_THETA) ** (-np.arange(0, rot_dim, 2, dtype=np.float32) / np.float32(rot_dim))
        ang = np.arange(seq, dtype=np.float32)[:, None] * inv.astype(np.float32)[None, :]
        cos, sin = np.cos(ang).astype(np.float32), np.sin(ang).astype(np.float32)
        ones = np.ones((seq, period - 2 * half), np.float32)
        z_h = np.zeros((seq, half), np.float32)
        c = np.concatenate([cos, cos, ones], axis=1)
        s1 = np.concatenate([-sin, z_h, 0 * ones], axis=1)
        s2 = np.concatenate([z_h, sin, 0 * ones], axis=1)
        return [np.tile(t, (1, n_rep)) for t in (c, s1, s2)]

    ta = tabs(ROT_DIM, HEAD_DIM, LANES // HEAD_DIM)
    ti = tabs(IDX_ROT_DIM, IDX_DIM, LANES // IDX_DIM)
    ts = tabs(IDX_ROT_DIM, IDX_DIM, 1)
    wi_scale = np.float32(IDX_HEADS ** -0.5 * IDX_DIM ** -0.5)
    pad = LANES - IDX_DIM
    c_tail = np.where(np.arange(pad) < IDX_HEADS, wi_scale, np.float32(0.0)).astype(np.float32)
    ts = [
        np.concatenate([ts[0], np.broadcast_to(c_tail, (seq, pad))], axis=1),
        np.concatenate([ts[1], np.zeros((seq, pad), np.float32)], axis=1),
        np.concatenate([ts[2], np.zeros((seq, pad), np.float32)], axis=1),
    ]
    return jnp.asarray(np.stack(ta + ti + ts, axis=0))


def _in_proj(x2, mod3, g_pre, w_in, seq):
    m_rows = x2.shape[0]
    tm = min(1024, seq)
    assert seq % tm == 0 and m_rows % tm == 0
    per_seq = seq // tm
    offs = np.cumsum((POOL_WIDTH, ATTN_WIDTH, KV_WIDTH, KV_WIDTH, IDX_HEADS * IDX_DIM, IDX_DIM, IDX_HEADS))
    w_u, w_q, w_k, w_v, w_qi, w_ki, w_wi, w_gl = jnp.split(w_in, [int(o) for o in offs], axis=1)
    w_main = jnp.concatenate([w_gl, w_q, w_u, w_qi, w_k, w_v], axis=1).astype(BF16)
    w_small = jnp.concatenate(
        [w_ki, w_wi, jnp.zeros((D_MODEL, LANES - IDX_DIM - IDX_HEADS), F32)], axis=1
    ).astype(BF16)
    tabs = _rope_tables(seq)
    n_j = PROJ_WIDTH // IN_TN
    return pl.pallas_call(
        _in_kernel,
        grid=(m_rows // tm, n_j),
        in_specs=[
            pl.BlockSpec((tm, D_MODEL), lambda i, j: (i, 0)),
            pl.BlockSpec((1, N_MOD, D_MODEL), lambda i, j: (i // per_seq, 0, 0)),
            pl.BlockSpec((1, D_MODEL), lambda i, j: (0, 0)),
            pl.BlockSpec((D_MODEL, IN_TN), lambda i, j: (0, j)),
            pl.BlockSpec((D_MODEL, LANES), lambda i, j: (0, 0)),
            pl.BlockSpec((9, tm, LANES), lambda i, j: (0, i % per_seq, 0)),
        ],
        out_specs=[
            pl.BlockSpec((tm, IN_TN), lambda i, j: (i, j)),
            pl.BlockSpec((tm, LANES), lambda i, j: (i, 0)),
        ],
        out_shape=[
            jax.ShapeDtypeStruct((m_rows, PROJ_WIDTH), BF16),
            jax.ShapeDtypeStruct((m_rows, LANES), F32),
        ],
        scratch_shapes=[pltpu.VMEM((tm, D_MODEL), BF16), pltpu.VMEM((tm, IN_TN), F32)],
        compiler_params=_cparams(("arbitrary", "arbitrary")),
        name="in_proj",
    )(x2, mod3, g_pre.reshape(1, D_MODEL), w_main, w_small, tabs)


def _attn_kernel(q_ref, qi_ref, k_ref, v_ref, kis_ref, wis_ref, o_ref,
                 key_ref, plim_ref, qs_ref, m_ref, l_ref, acc_ref, *, tq, kc, seq, n_keep):
    i = pl.program_id(1)
    nk = ((i + 1) * tq + kc - 1) // kc
    nt = (((1,), (1,)), ((), ()))
    grp = N_HEADS // N_KV_HEADS
    row_io = lax.broadcasted_iota(I32, (kc, tq), 0)
    tpos = i * tq + lax.broadcasted_iota(I32, (kc, tq), 1)
    w_t = wis_ref[...].T

    def idx_chunk(c, carry):
        r0 = pl.multiple_of(c * kc, kc)
        ki = kis_ref[pl.ds(r0, kc), :IDX_DIM].astype(BF16)
        sc = jnp.zeros((kc, tq), F32)
        for h in range(IDX_HEADS):
            qh = qi_ref[:, h * IDX_DIM:(h + 1) * IDX_DIM]
            d = lax.dot_general(ki, qh, nt, preferred_element_type=F32)
            sc = sc + jnp.maximum(d, 0.0) * w_t[IDX_DIM + h:IDX_DIM + h + 1, :]
        bits = lax.bitcast_convert_type(sc, I32)
        key = jnp.where(bits < 0, bits ^ np.int32(0x7FFFFFFF), bits)
        key_ref[pl.ds(r0, kc), :] = jnp.where(r0 + row_io <= tpos, key, INT_MIN)
        return carry

    lax.fori_loop(0, nk, idx_chunk, 0)

    def count(pred):
        def body(c, acc):
            r0 = pl.multiple_of(c * kc, kc)
            hit = pred(key_ref[pl.ds(r0, kc), :], r0)
            return acc + jnp.sum(hit.astype(I32), axis=0, keepdims=True)

        return lax.fori_loop(0, nk, body, jnp.zeros((1, tq), I32))

    def bit_step(it, tu):
        cand = tu | lax.shift_left(np.int32(1), 31 - it)
        cnt = count(lambda kch, r0: kch >= (cand ^ INT_MIN))
        return jnp.where(cnt >= n_keep, cand, tu)

    tu = lax.fori_loop(0, 32, bit_step, jnp.zeros((1, tq), I32))
    thr = tu ^ INT_MIN
    cnt_gt = count(lambda kch, r0: kch > thr)
    cnt_ge = count(lambda kch, r0: kch >= thr)
    tie_i = ((cnt_ge > n_keep) & (tu != 0)).astype(I32)
    need = n_keep - cnt_gt

    plim_ref[...] = jnp.full(plim_ref.shape, seq, I32)

    @pl.when(jnp.max(tie_i) > 0)
    def _():
        n_bits = (seq - 1).bit_length()

        def idx_step(it, p):
            cand = p | lax.shift_left(np.int32(1), (n_bits - 1) - it)
            below = count(lambda kch, r0: (kch == thr) & (r0 + row_io < cand))
            return jnp.where(below < need, cand, p)

        p = lax.fori_loop(0, n_bits, idx_step, jnp.zeros((1, tq), I32))
        plim_ref[...] = jnp.broadcast_to(jnp.where(tie_i > 0, p, seq), plim_ref.shape)

    for g in range(N_KV_HEADS):
        for r in range(grp):
            hh = g * grp + r
            qs_ref[g, r * tq:(r + 1) * tq, :] = q_ref[:, hh * HEAD_DIM:(hh + 1) * HEAD_DIM]
    m_ref[...] = jnp.full(m_ref.shape, MASK_NEG, F32)
    l_ref[...] = jnp.zeros_like(l_ref)
    acc_ref[...] = jnp.zeros_like(acc_ref)
    c_exp = HEAD_DIM ** -0.5 * np.log2(np.e)

    def att_chunk(c, carry):
        r0 = pl.multiple_of(c * kc, kc)
        kch = key_ref[pl.ds(r0, kc), :]
        row = r0 + row_io
        sel = ((kch > thr) | ((kch == thr) & (row <= plim_ref[0:1, :]))) & (row <= tpos)
        bias_t = jnp.where(sel, 0.0, MASK_NEG).astype(F32).T
        for g in range(N_KV_HEADS):
            kg = k_ref[pl.ds(r0, kc), g * HEAD_DIM:(g + 1) * HEAD_DIM]
            vg = v_ref[pl.ds(r0, kc), g * HEAD_DIM:(g + 1) * HEAD_DIM]
            s = lax.dot_general(qs_ref[g], kg, nt, preferred_element_type=F32)
            s = (s.reshape(grp, tq, kc) + bias_t[None]).reshape(grp * tq, kc)
            m_old = m_ref[g]
            m_new = jnp.maximum(m_old, jnp.max(s, axis=-1, keepdims=True))
            alpha = jnp.exp2((m_old - m_new) * c_exp)
            p = jnp.exp2((s - m_new) * c_exp)
            l_ref[g] = alpha * l_ref[g] + jnp.sum(p, axis=-1, keepdims=True)
            acc_ref[g] = alpha * acc_ref[g] + jnp.dot(p.astype(BF16), vg, preferred_element_type=F32)
            m_ref[g] = m_new
        return carry

    lax.fori_loop(0, nk, att_chunk, 0)
    for g in range(N_KV_HEADS):
        o = acc_ref[g] / l_ref[g]
        for r in range(grp):
            hh = g * grp + r
            o_ref[:, hh * HEAD_DIM:(hh + 1) * HEAD_DIM] = o[r * tq:(r + 1) * tq, :].astype(BF16)


def _attn(proj, small, bsz, seq):
    tq = min(ATTN_TQ, seq)
    kc = min(ATTN_KC, seq)
    assert seq % tq == 0 and seq % kc == 0 and kc % tq == 0
    nq = seq // tq
    n_keep = min(TOPK_MAX, seq // 4)
    grp = N_HEADS // N_KV_HEADS
    kern = functools.partial(_attn_kernel, tq=tq, kc=kc, seq=seq, n_keep=n_keep)
    return pl.pallas_call(
        kern,
        grid=(bsz, nq),
        in_specs=[
            pl.BlockSpec((tq, ATTN_WIDTH), lambda b, i: (b * nq + i, Q_OFF // ATTN_WIDTH)),
            pl.BlockSpec((tq, IDX_HEADS * IDX_DIM), lambda b, i: (b * nq + i, QI_OFF // (IDX_HEADS * IDX_DIM))),
            pl.BlockSpec((seq, KV_WIDTH), lambda b, i: (b, K_OFF // KV_WIDTH)),
            pl.BlockSpec((seq, KV_WIDTH), lambda b, i: (b, V_OFF // KV_WIDTH)),
            pl.BlockSpec((seq, LANES), lambda b, i: (b, 0)),
            pl.BlockSpec((tq, LANES), lambda b, i: (b * nq + i, 0)),
        ],
        out_specs=pl.BlockSpec((tq, ATTN_WIDTH), lambda b, i: (b * nq + i, 0)),
        out_shape=jax.ShapeDtypeStruct((bsz * seq, ATTN_WIDTH), BF16),
        scratch_shapes=[
            pltpu.VMEM((seq, tq), I32),
            pltpu.VMEM((8, tq), I32),
            pltpu.VMEM((N_KV_HEADS, grp * tq, HEAD_DIM), BF16),
            pltpu.VMEM((N_KV_HEADS, grp * tq, 1), F32),
            pltpu.VMEM((N_KV_HEADS, grp * tq, 1), F32),
            pltpu.VMEM((N_KV_HEADS, grp * tq, HEAD_DIM), F32),
        ],
        compiler_params=_cparams(("arbitrary", "arbitrary")),
        name="attn",
    )(proj, proj, proj, proj, small, small)


def _merge_kernel(
    u_ref, halo_ref, ya_ref, gl_ref, x_ref, mod_ref, wgrp_ref, pscale_ref, wup_p_ref, wup_a_ref, wout_ref,
    gpost_ref, gffn_ref, wr_ref, br_ref,
    x1_ref, hp_ref, route_ref, cnt_ref,
    carry_ref, *, seq,
):
    tm = MERGE_TM
    i = pl.program_id(0)
    pos0 = (i * tm) % seq

    @pl.when(i == 0)
    def _():
        carry_ref[...] = jnp.zeros_like(carry_ref)

    halo_rows = halo_ref.shape[0]
    halo = jnp.where(pos0 == 0, 0.0, halo_ref[...].astype(F32))
    ext = jnp.concatenate([halo, u_ref[...].astype(F32)], axis=0)
    pos = pos0 + lax.broadcasted_iota(I32, (tm, 1), 0)
    ys = []
    for g, win in enumerate(POOL_WINDOWS):
        sl = slice(g * POOL_GROUP_DIM, (g + 1) * POOL_GROUP_DIM)
        e = ext[:, sl]
        acc = e
        span = 1
        while span < win:
            acc = acc + pltpu.roll(acc, span, 0)
            span *= 2
        cnt = jnp.minimum(pos + 1, win).astype(F32)
        mixed = (acc[halo_rows:] / cnt - e[halo_rows:]).astype(BF16)
        ys.append(jnp.dot(mixed, wgrp_ref[g], preferred_element_type=F32))
    y_pool = (jnp.concatenate(ys, axis=1) * pscale_ref[...]).astype(BF16)

    gates = jax.nn.sigmoid(gl_ref[...].astype(F32))
    up_p = jnp.dot(y_pool, wup_p_ref[...], preferred_element_type=F32)
    up_a = jnp.dot(ya_ref[...], wup_a_ref[...], preferred_element_type=F32)
    merged = (gates[:, :D_MODEL] * up_p + gates[:, D_MODEL:] * up_a).astype(BF16)
    y = jnp.dot(merged, wout_ref[...], preferred_element_type=F32)
    m = mod_ref[0]
    x1 = x_ref[...] + m[2:3] * _rms(y, gpost_ref[...])
    x1_ref[...] = x1

    h2f = _rms(x1, gffn_ref[...]) * (1.0 + m[4:5]) + m[3:4]
    h2 = h2f.astype(BF16)
    hp_ref[...] = _pack_bf16_pairs(h2f)

    lane = lax.broadcasted_iota(I32, (tm, LANES), 1)
    logits = jnp.dot(h2, wr_ref[...], preferred_element_type=F32) + br_ref[...]
    work = jnp.where(lane < N_EXPERTS, logits, -jnp.inf)
    vals, hots = [], []
    for _ in range(TOP_K):
        mx = jnp.max(work, axis=-1, keepdims=True)
        first = jnp.min(jnp.where(work == mx, lane, LANES), axis=-1, keepdims=True)
        hot = lane == first
        vals.append(mx)
        hots.append(hot)
        work = jnp.where(hot, -jnp.inf, work)
    ex = [jnp.exp(v - vals[0]) for v in vals]
    den = ex[0] + ex[1] + ex[2] + ex[3]

    onehot = jnp.zeros((tm, LANES), F32)
    for hot in hots:
        onehot = onehot + hot.astype(F32)
    r_i = lax.broadcasted_iota(I32, (tm, tm), 0)
    c_i = lax.broadcasted_iota(I32, (tm, tm), 1)
    tril = (c_i < r_i).astype(BF16)
    before = jnp.dot(tril, onehot.astype(BF16), preferred_element_type=F32) + carry_ref[0:1, :]
    out = jnp.zeros((tm, LANES), F32)
    lane_f = lane.astype(F32)
    for k in range(TOP_K):
        out = jnp.where(lane == k, ex[k] / den, out)
        idx_k = jnp.sum(jnp.where(hots[k], lane_f, 0.0), axis=-1, keepdims=True)
        out = jnp.where(lane == TOP_K + k, idx_k, out)
        rank_k = jnp.sum(jnp.where(hots[k], before, 0.0), axis=-1, keepdims=True)
        out = jnp.where(lane == 2 * TOP_K + k, rank_k, out)
    route_ref[...] = out
    total = carry_ref[0:1, :] + jnp.sum(onehot, axis=0, keepdims=True)
    carry_ref[...] = jnp.broadcast_to(total, carry_ref.shape)
    cnt_ref[...] = jnp.broadcast_to(total, cnt_ref.shape)


def _merge(proj, y_attn, x2, mod3, w_pool_grp, pool_scale, w_up_pool, w_up_attn, w_out, g_post_mix, g_pre_ffn,
           w_router, b_router, seq):
    m_rows = x2.shape[0]
    tm = MERGE_TM
    halo = 16
    assert seq % tm == 0 and max(POOL_WINDOWS) <= halo
    per_seq = seq // tm
    wr = jnp.concatenate([w_router, jnp.zeros((D_MODEL, LANES - N_EXPERTS), F32)], axis=1).astype(BF16)
    br = jnp.concatenate([b_router, jnp.zeros((LANES - N_EXPERTS,), F32)]).reshape(1, LANES)
    const = lambda shape: pl.BlockSpec(shape, lambda i: (0,) * len(shape), pipeline_mode=pl.Buffered(1))
    kern = functools.partial(_merge_kernel, seq=seq)
    return pl.pallas_call(
        kern,
        grid=(m_rows // tm,),
        in_specs=[
            pl.BlockSpec((tm, POOL_WIDTH), lambda i: (i, U_OFF // POOL_WIDTH)),
            pl.BlockSpec((halo, POOL_WIDTH), lambda i: (jnp.maximum(i * (tm // halo) - 1, 0), U_OFF // POOL_WIDTH)),
            pl.BlockSpec((tm, ATTN_WIDTH), lambda i: (i, 0)),
            pl.BlockSpec((tm, N_BRANCHES * D_MODEL), lambda i: (i, 0)),
            pl.BlockSpec((tm, D_MODEL), lambda i: (i, 0)),
            pl.BlockSpec((1, N_MOD, D_MODEL), lambda i: (i // per_seq, 0, 0)),
            const((POOL_GROUPS, POOL_GROUP_DIM, POOL_GROUP_DIM)),
            const((1, POOL_WIDTH)),
            const((POOL_WIDTH, D_MODEL)),
            const((ATTN_WIDTH, D_MODEL)),
            const((D_MODEL, D_MODEL)),
            const((1, D_MODEL)),
            const((1, D_MODEL)),
            const((D_MODEL, LANES)),
            const((1, LANES)),
        ],
        out_specs=[
            pl.BlockSpec((tm, D_MODEL), lambda i: (i, 0)),
            pl.BlockSpec((tm, D_MODEL // 2), lambda i: (i, 0)),
            pl.BlockSpec((tm, LANES), lambda i: (i, 0)),
            pl.BlockSpec((8, LANES), lambda i: (0, 0)),
        ],
        out_shape=[
            jax.ShapeDtypeStruct((m_rows, D_MODEL), F32),
            jax.ShapeDtypeStruct((m_rows, D_MODEL // 2), U32),
            jax.ShapeDtypeStruct((m_rows, LANES), F32),
            jax.ShapeDtypeStruct((8, LANES), F32),
        ],
        scratch_shapes=[pltpu.VMEM((8, LANES), F32)],
        compiler_params=_cparams(("arbitrary",)),
        name="merge",
    )(
        proj, proj, y_attn, proj, x2, mod3,
        w_pool_grp.astype(BF16), pool_scale.reshape(1, POOL_WIDTH), w_up_pool.astype(BF16),
        w_up_attn.astype(BF16), w_out.astype(BF16), g_post_mix.reshape(1, D_MODEL), g_pre_ffn.reshape(1, D_MODEL),
        wr, br,
    )


def _dispatch_kernel(zf_ref, pos_ref, h_ref, xs_ref, zbuf, sem, zsem):
    tm = h_ref.shape[0]
    bm = zbuf.shape[0]

    @pl.when(pl.program_id(0) == 0)
    def _():
        zbuf[...] = jnp.zeros_like(zbuf)

        def zero_copy(b):
            return pltpu.make_async_copy(zbuf, xs_ref.at[pl.ds(b * bm, bm), :], zsem)

        def z_issue(b, carry):
            pl.when(zf_ref[b] != 0)(lambda: zero_copy(b).start())
            return carry

        def z_drain(b, carry):
            pl.when(zf_ref[b] != 0)(lambda: zero_copy(b).wait())
            return carry

        lax.fori_loop(0, zf_ref.shape[0], z_issue, 0)
        lax.fori_loop(0, zf_ref.shape[0], z_drain, 0)

    def row_copy(t, k):
        return pltpu.make_async_copy(
            h_ref.at[pl.ds(t, 1), :], xs_ref.at[pl.ds(pos_ref[t * TOP_K + k], 1), :], sem
        )

    def issue(t, carry):
        for k in range(TOP_K):
            row_copy(t, k).start(priority=k % 2)
        return carry

    lax.fori_loop(0, tm, issue, 0, unroll=ROW_DMA_UNROLL)
    for _ in range(TOP_K):
        pltpu.make_async_copy(h_ref, xs_ref.at[pl.ds(0, tm), :], sem).wait()


def _dispatch(zero_flag, pos_flat, hp, n_rows):
    m_rows, width = hp.shape
    tm = min(DISPATCH_TM, m_rows)
    assert m_rows % tm == 0
    grid_spec = pltpu.PrefetchScalarGridSpec(
        num_scalar_prefetch=1,
        grid=(m_rows // tm,),
        in_specs=[
            pl.BlockSpec((tm * TOP_K,), lambda i, zf: (i,), memory_space=pltpu.SMEM),
            pl.BlockSpec((tm, width), lambda i, zf: (i, 0)),
        ],
        out_specs=pl.BlockSpec(memory_space=pl.ANY),
        scratch_shapes=[pltpu.VMEM((MOE_BM, width), U32), pltpu.SemaphoreType.DMA(()), pltpu.SemaphoreType.DMA(())],
    )
    return pl.pallas_call(
        _dispatch_kernel,
        grid_spec=grid_spec,
        out_shape=jax.ShapeDtypeStruct((n_rows, width), U32),
        compiler_params=_cparams(("arbitrary",)),
        name="dispatch",
    )(zero_flag, pos_flat, hp)


def _expert_rows(e, j, bs_ref, nb_ref, nu_ref, x_hbm, o_hbm, xbuf, obuf, xsem, osem, compute, prepare):
    n_x, bm = xbuf.shape[0], xbuf.shape[1]
    tw = obuf.shape[2]
    n_blocks = x_hbm.shape[0] // bm
    b0 = bs_ref[e]
    nb = nb_ref[e]
    n_used = nu_ref[0]

    def x_copy(blk, slot):
        return pltpu.make_async_copy(x_hbm.at[pl.ds(blk * bm, bm), :], xbuf.at[slot], xsem.at[slot])

    def o_copy(blk, slot):
        return pltpu.make_async_copy(obuf.at[slot], o_hbm.at[pl.ds(blk * bm, bm), pl.ds(j * tw, tw)], osem.at[slot])

    @pl.when(nb > 0)
    def _():
        prepare()

        def body(rb, carry):
            g = b0 + rb
            slot = g % 2
            ahead = n_x - 1

            @pl.when(g == 0)
            def _():
                for a in range(ahead):
                    pl.when(a < n_used)(lambda a=a: x_copy(a, a).start())

            x_copy(g, g % n_x).wait()

            @pl.when(g + ahead < n_used)
            def _():
                x_copy(g + ahead, (g + ahead) % n_x).start()

            @pl.when(g >= 2)
            def _():
                o_copy(g - 2, slot).wait()

            obuf[slot] = compute(xbuf[g % n_x])
            o_copy(g, slot).start()

            @pl.when(g == n_used - 1)
            def _():
                @pl.when(g >= 1)
                def _():
                    o_copy(g - 1, 1 - slot).wait()

                o_copy(g, slot).wait()

            return carry

        lax.fori_loop(0, nb, body, 0)

    @pl.when(e == N_EXPERTS - 1)
    def _():
        obuf[0] = jnp.zeros(obuf.shape[1:], obuf.dtype)

        def z_issue(blk, carry):
            o_copy(blk, 0).start()
            return carry

        def z_drain(blk, carry):
            o_copy(blk, 0).wait()
            return carry

        lax.fori_loop(nu_ref[0], n_blocks, z_issue, 0)
        lax.fori_loop(nu_ref[0], n_blocks, z_drain, 0)


def _ffn1_kernel(bs_ref, nb_ref, nu_ref, x_hbm, wg_ref, wl_ref, bg_ref, bl_ref, o_hbm,
                 wg_s, wl_s, xbuf, obuf, xsem, osem):
    j, e = pl.program_id(0), pl.program_id(1)
    half = D_MODEL // 2

    def prepare():
        wg_s[...] = wg_ref[0].astype(BF16)
        wl_s[...] = wl_ref[0].astype(BF16)

    def compute(xp):
        lo, hi = (v.astype(BF16) for v in _unpack_bf16_pairs(xp))

        def proj(w_s, b_ref):
            return (
                jnp.dot(lo, w_s[:half, :], preferred_element_type=F32)
                + jnp.dot(hi, w_s[half:, :], preferred_element_type=F32)
                + b_ref[0]
            )

        glu = jnp.minimum(proj(wg_s, bg_ref), SWIGLU_LIMIT)
        lin = jnp.clip(proj(wl_s, bl_ref), -SWIGLU_LIMIT, SWIGLU_LIMIT)
        return (glu * jax.nn.sigmoid(SWIGLU_ALPHA * glu) * (lin + 1.0)).astype(BF16)

    _expert_rows(e, j, bs_ref, nb_ref, nu_ref, x_hbm, o_hbm, xbuf, obuf, xsem, osem, compute, prepare)


def _ffn1(blk_start, blk_count, n_used, xs, w1, b1):
    n_rows = xs.shape[0]
    bm, tn = MOE_BM, FFN1_TN
    nj = D_FF // tn
    b1r = b1.reshape(N_EXPERTS, 1, 2 * D_FF)
    grid_spec = pltpu.PrefetchScalarGridSpec(
        num_scalar_prefetch=3,
        grid=(nj, N_EXPERTS),
        in_specs=[
            pl.BlockSpec(memory_space=pl.ANY),
            pl.BlockSpec((1, D_MODEL, tn), lambda j, e, *_: (e, 0, j)),
            pl.BlockSpec((1, D_MODEL, tn), lambda j, e, *_: (e, 0, nj + j)),
            pl.BlockSpec((1, 1, tn), lambda j, e, *_: (e, 0, j)),
            pl.BlockSpec((1, 1, tn), lambda j, e, *_: (e, 0, nj + j)),
        ],
        out_specs=pl.BlockSpec(memory_space=pl.ANY),
        scratch_shapes=[
            pltpu.VMEM((D_MODEL, tn), BF16),
            pltpu.VMEM((D_MODEL, tn), BF16),
            pltpu.VMEM((FFN_X_SLOTS, bm, D_MODEL // 2), U32),
            pltpu.VMEM((2, bm, tn), BF16),
            pltpu.SemaphoreType.DMA((FFN_X_SLOTS,)),
            pltpu.SemaphoreType.DMA((2,)),
        ],
    )
    return pl.pallas_call(
        _ffn1_kernel,
        grid_spec=grid_spec,
        out_shape=jax.ShapeDtypeStruct((n_rows, D_FF), BF16),
        compiler_params=_cparams(("arbitrary", "arbitrary")),
        name="ffn1",
    )(blk_start, blk_count, n_used, xs, w1, w1, b1r, b1r)


def _ffn2_kernel(bs_ref, nb_ref, nu_ref, a_hbm, w_ref, b_ref, o_hbm, w_s, xbuf, obuf, xsem, osem):
    j, e = pl.program_id(0), pl.program_id(1)

    def prepare():
        w_s[...] = w_ref[0].astype(BF16)

    def compute(a):
        return _pack_bf16_pairs(jnp.dot(a, w_s[...], preferred_element_type=F32) + b_ref[0])

    _expert_rows(e, j, bs_ref, nb_ref, nu_ref, a_hbm, o_hbm, xbuf, obuf, xsem, osem, compute, prepare)


def _ffn2(blk_start, blk_count, n_used, act, w2, b2):
    n_rows = act.shape[0]
    bm, tn = MOE_BM, FFN2_TN
    nj = D_MODEL // tn
    b2r = b2.reshape(N_EXPERTS, 1, D_MODEL)
    grid_spec = pltpu.PrefetchScalarGridSpec(
        num_scalar_prefetch=3,
        grid=(nj, N_EXPERTS),
        in_specs=[
            pl.BlockSpec(memory_space=pl.ANY),
            pl.BlockSpec((1, D_FF, tn), lambda j, e, *_: (e, 0, j)),
            pl.BlockSpec((1, 1, tn), lambda j, e, *_: (e, 0, j)),
        ],
        out_specs=pl.BlockSpec(memory_space=pl.ANY),
        scratch_shapes=[
            pltpu.VMEM((D_FF, tn), BF16),
            pltpu.VMEM((FFN_X_SLOTS, bm, D_FF), BF16),
            pltpu.VMEM((2, bm, tn // 2), U32),
            pltpu.SemaphoreType.DMA((FFN_X_SLOTS,)),
            pltpu.SemaphoreType.DMA((2,)),
        ],
    )
    return pl.pallas_call(
        _ffn2_kernel,
        grid_spec=grid_spec,
        out_shape=jax.ShapeDtypeStruct((n_rows, D_MODEL // 2), U32),
        compiler_params=_cparams(("arbitrary", "arbitrary")),
        name="ffn2",
    )(blk_start, blk_count, n_used, act, w2, b2r)


def _combine_kernel(pos_ref, ys_ref, route_ref, x1_ref, mod_ref, g_ref, o_ref, buf, sem):
    tm = x1_ref.shape[0]

    def row_copy(t, k):
        return pltpu.make_async_copy(
            ys_ref.at[pl.ds(pos_ref[t * TOP_K + k], 1), :], buf.at[k, pl.ds(t, 1), :], sem
        )

    def issue(t, carry):
        for k in range(TOP_K):
            row_copy(t, k).start(priority=k % 2)
        return carry

    lax.fori_loop(0, tm, issue, 0, unroll=ROW_DMA_UNROLL)
    for k in range(TOP_K):
        pltpu.make_async_copy(ys_ref.at[pl.ds(0, tm), :], buf.at[k], sem).wait()
    route = route_ref[...]
    y_lo, y_hi = (route[:, 0:1] * v for v in _unpack_bf16_pairs(buf[0]))
    for k in range(1, TOP_K):
        lo, hi = _unpack_bf16_pairs(buf[k])
        y_lo = y_lo + route[:, k:k + 1] * lo
        y_hi = y_hi + route[:, k:k + 1] * hi
    hw = FFN2_TN // 2
    parts = []
    for t in range(D_MODEL // FFN2_TN):
        parts += [y_lo[:, t * hw:(t + 1) * hw], y_hi[:, t * hw:(t + 1) * hw]]
    y = jnp.concatenate(parts, axis=1)
    m = mod_ref[0]
    o_ref[...] = x1_ref[...] + m[5:6] * _rms(y, g_ref[...])


def _combine(pos_flat, ys, route, x1, mod3, g_post_ffn, seq):
    m_rows = x1.shape[0]
    tm = min(COMBINE_TM, seq)
    assert seq % tm == 0
    per_seq = seq // tm
    return pl.pallas_call(
        _combine_kernel,
        grid=(m_rows // tm,),
        in_specs=[
            pl.BlockSpec((tm * TOP_K,), lambda i: (i,), memory_space=pltpu.SMEM),
            pl.BlockSpec(memory_space=pl.ANY),
            pl.BlockSpec((tm, LANES), lambda i: (i, 0)),
            pl.BlockSpec((tm, D_MODEL), lambda i: (i, 0)),
            pl.BlockSpec((1, N_MOD, D_MODEL), lambda i: (i // per_seq, 0, 0)),
            pl.BlockSpec((1, D_MODEL), lambda i: (0, 0)),
        ],
        out_specs=pl.BlockSpec((tm, D_MODEL), lambda i: (i, 0)),
        out_shape=jax.ShapeDtypeStruct((m_rows, D_MODEL), F32),
        scratch_shapes=[pltpu.VMEM((TOP_K, tm, D_MODEL // 2), U32), pltpu.SemaphoreType.DMA(())],
        compiler_params=_cparams(("arbitrary",)),
        name="combine",
    )(pos_flat, ys, route, x1, mod3, g_post_ffn.reshape(1, D_MODEL))


def _layer(x2, mod3, bsz, seq, g_pre_mix, g_post_mix, w_in, w_pool_grp, pool_scale, w_up_pool, w_up_attn, w_out,
           g_pre_ffn, g_post_ffn, w_router, b_router, w1, b1, w2, b2):
    m_rows = x2.shape[0]
    proj, small = _in_proj(x2, mod3, g_pre_mix, w_in, seq)
    y_attn = _attn(proj, small, bsz, seq)
    x1, hp, route, cnt = _merge(proj, y_attn, x2, mod3, w_pool_grp, pool_scale, w_up_pool, w_up_attn, w_out,
                                g_post_mix, g_pre_ffn, w_router, b_router, seq)

    bm = MOE_BM
    n_slots = m_rows * TOP_K
    n_blocks = -(-n_slots // bm) + N_EXPERTS
    idx = route[:, TOP_K:2 * TOP_K].astype(I32)
    rank = route[:, 2 * TOP_K:3 * TOP_K].astype(I32)
    counts = cnt[0, :N_EXPERTS].astype(I32)
    padded = (counts + bm - 1) // bm * bm
    pad_end = jnp.cumsum(padded)
    pad_start = pad_end - padded
    pos_flat = (pad_start[idx] + rank).reshape(-1)
    n_used = (pad_end[-1] // bm).astype(I32).reshape(1)
    blk_ids = jnp.arange(n_blocks, dtype=I32)
    blk_start = (pad_start // bm).astype(I32)
    blk_count = (padded // bm).astype(I32)
    last_of_expert = jnp.any((blk_ids[:, None] + 1) * bm == pad_end[None, :], axis=1)
    zero_flag = (last_of_expert | (blk_ids >= n_used[0])).astype(I32)

    xs = _dispatch(zero_flag, pos_flat, hp, n_blocks * bm)
    act = _ffn1(blk_start, blk_count, n_used, xs, w1, b1)
    ys = _ffn2(blk_start, blk_count, n_used, act, w2, b2)
    return _combine(pos_flat, ys, route, x1, mod3, g_post_ffn, seq)


def kernel(x, c, w_ada, b_ada, g_pre_mix, g_post_mix, w_in, w_pool_grp, pool_scale, w_up_pool, w_up_attn, w_out,
           g_pre_ffn, g_post_ffn, w_router, b_router, w1, b1, w2, b2):
    bsz, seq, d = x.shape
    assert d == D_MODEL
    depth = w_ada.shape[0]
    x2 = x.reshape(bsz * seq, d)
    for layer in range(depth):
        mod3 = _ada(c, w_ada[layer], b_ada[layer]).reshape(bsz, N_MOD, D_MODEL)
        x2 = _layer(x2, mod3, bsz, seq, g_pre_mix[layer], g_post_mix[layer], w_in[layer], w_pool_grp[layer],
                    pool_scale[layer], w_up_pool[layer], w_up_attn[layer], w_out[layer], g_pre_ffn[layer],
                    g_post_ffn[layer], w_router[layer], b_router[layer], w1[layer], b1[layer], w2[layer], b2[layer])
    return x2.reshape(bsz, seq, d)
```

```python
import functools

import jax
import jax.numpy as jnp
import numpy as np
from jax import lax
from jax.experimental import pallas as pl
from jax.experimental.pallas import tpu as pltpu

F32 = jnp.float32
BF16 = jnp.bfloat16
I32 = jnp.int32
U32 = jnp.uint32

D_MODEL = 2048
POOL_WINDOWS = (2, 4, 8, 16)
POOL_GROUPS = 4
POOL_GROUP_DIM = D_MODEL // 8
POOL_WIDTH = POOL_GROUPS * POOL_GROUP_DIM
N_HEADS = 16
N_KV_HEADS = 4
HEAD_DIM = 128
ATTN_WIDTH = N_HEADS * HEAD_DIM
KV_WIDTH = N_KV_HEADS * HEAD_DIM
ROT_DIM = HEAD_DIM // 4
IDX_HEADS = 16
IDX_DIM = 64
IDX_ROT_DIM = IDX_DIM // 4
TOPK_MAX = 256
ROPE_THETA = 500000.0
N_BRANCHES = 2
N_EXPERTS = 32
TOP_K = 4
D_FF = D_MODEL
SWIGLU_ALPHA = 1.702
SWIGLU_LIMIT = 7.0
N_MOD = 6
EPS = 1e-6

LANES = 128
INT_MIN = np.int32(-(2 ** 31))

GL_OFF = 0
Q_OFF = GL_OFF + N_BRANCHES * D_MODEL
U_OFF = Q_OFF + ATTN_WIDTH
QI_OFF = U_OFF + POOL_WIDTH
K_OFF = QI_OFF + IDX_HEADS * IDX_DIM
V_OFF = K_OFF + KV_WIDTH
PROJ_WIDTH = V_OFF + KV_WIDTH

IN_TN = 1024
ATTN_TQ = 512
ATTN_KC = 512
MASK_NEG = -1e30
MERGE_TM = 256
MOE_BM = 512
FFN1_TN = 512
FFN2_TN = 1024
DISPATCH_TM = 1024
COMBINE_TM = 512
ROW_DMA_UNROLL = 8
FFN_X_SLOTS = 3
VMEM_LIMIT = 56 * 1024 * 1024


def _cparams(sem):
    return pltpu.CompilerParams(dimension_semantics=sem, vmem_limit_bytes=VMEM_LIMIT)


def _rms(x, g):
    return x * lax.rsqrt(jnp.mean(x * x, axis=-1, keepdims=True) + EPS) * g


def _pack_bf16_pairs(x):
    n = x.shape[1] // 2
    bits = lax.bitcast_convert_type(x.astype(BF16).astype(F32), U32)
    return bits[:, n:] | lax.shift_right_logical(bits[:, :n], np.uint32(16))


def _unpack_bf16_pairs(w):
    lo = lax.bitcast_convert_type(lax.shift_left(w, np.uint32(16)), F32)
    hi = lax.bitcast_convert_type(w & np.uint32(0xFFFF0000), F32)
    return lo, hi


def _ada_kernel(c_ref, w_ref, b_ref, o_ref):
    c = c_ref[...]
    ca = (c * jax.nn.sigmoid(c)).astype(BF16)
    o_ref[...] = jnp.dot(ca, w_ref[...].astype(BF16), preferred_element_type=F32) + b_ref[...]


def _ada(c, w_ada, b_ada):
    bsz = c.shape[0]
    n = w_ada.shape[1]
    tn = 1024
    return pl.pallas_call(
        _ada_kernel,
        grid=(n // tn,),
        in_specs=[
            pl.BlockSpec((bsz, D_MODEL), lambda j: (0, 0)),
            pl.BlockSpec((D_MODEL, tn), lambda j: (0, j)),
            pl.BlockSpec((1, tn), lambda j: (0, j)),
        ],
        out_specs=pl.BlockSpec((bsz, tn), lambda j: (0, j)),
        out_shape=jax.ShapeDtypeStruct((bsz, n), F32),
        compiler_params=_cparams(("arbitrary",)),
        name="ada",
    )(c, w_ada, b_ada.reshape(1, n))


def _rope_slice(xs, c, s1, s2, shift):
    return xs * c + pltpu.roll(xs, LANES - shift, 1) * s1 + pltpu.roll(xs, shift, 1) * s2


def _in_kernel(x_ref, mod_ref, g_ref, w_ref, ws_ref, tab_ref, o_ref, os_ref, h_ref, acc_ref):
    j = pl.program_id(1)
    half_a = ROT_DIM // 2
    half_i = IDX_ROT_DIM // 2

    @pl.when(j == 0)
    def _():
        m = mod_ref[0]
        h = _rms(x_ref[...], g_ref[...]) * (1.0 + m[1:2]) + m[0:1]
        hb = h.astype(BF16)
        h_ref[...] = hb
        small = jnp.dot(hb, ws_ref[...], preferred_element_type=F32)
        os_ref[...] = _rope_slice(small, tab_ref[6], tab_ref[7], tab_ref[8], half_i)

    acc_ref[...] = jnp.dot(h_ref[...], w_ref[...], preferred_element_type=F32)
    n_sl = IN_TN // LANES
    j_q0, j_q1 = Q_OFF // IN_TN, U_OFF // IN_TN
    j_qi = QI_OFF // IN_TN
    j_kv = K_OFF // IN_TN
    k_sl = KV_WIDTH // LANES

    @pl.when((j < j_q0) | (j == U_OFF // IN_TN))
    def _():
        o_ref[...] = acc_ref[...].astype(BF16)

    @pl.when((j >= j_q0) & (j < j_q1))
    def _():
        for s in range(n_sl):
            sl = slice(s * LANES, (s + 1) * LANES)
            o_ref[:, sl] = _rope_slice(acc_ref[:, sl], tab_ref[0], tab_ref[1], tab_ref[2], half_a).astype(BF16)

    @pl.when(j == j_qi)
    def _():
        for s in range(n_sl):
            sl = slice(s * LANES, (s + 1) * LANES)
            o_ref[:, sl] = _rope_slice(acc_ref[:, sl], tab_ref[3], tab_ref[4], tab_ref[5], half_i).astype(BF16)

    @pl.when(j == j_kv)
    def _():
        for s in range(k_sl):
            sl = slice(s * LANES, (s + 1) * LANES)
            o_ref[:, sl] = _rope_slice(acc_ref[:, sl], tab_ref[0], tab_ref[1], tab_ref[2], half_a).astype(BF16)
        o_ref[:, KV_WIDTH:] = acc_ref[:, KV_WIDTH:].astype(BF16)


def _rope_tables(seq):
    def tabs(rot_dim, period, n_rep):
        half = rot_dim // 2
        inv = np.float32(ROPE_THETA) ** (-np.arange(0, rot_dim, 2, dtype=np.float32) / np.float32(rot_dim))
        ang = np.arange(seq, dtype=np.float32)[:, None] * inv.astype(np.float32)[None, :]
        cos, sin = np.cos(ang).astype(np.float32), np.sin(ang).astype(np.float32)
        ones = np.ones((seq, period - 2 * half), np.float32)
        z_h = np.zeros((seq, half), np.float32)
        c = np.concatenate([cos, cos, ones], axis=1)
        s1 = np.concatenate([-sin, z_h, 0 * ones], axis=1)
        s2 = np.concatenate([z_h, sin, 0 * ones], axis=1)
        return [np.tile(t, (1, n_rep)) for t in (c, s1, s2)]

    ta = tabs(ROT_DIM, HEAD_DIM, LANES // HEAD_DIM)
    ti = tabs(IDX_ROT_DIM, IDX_DIM, LANES // IDX_DIM)
    ts = tabs(IDX_ROT_DIM, IDX_DIM, 1)
    wi_scale = np.float32(IDX_HEADS ** -0.5 * IDX_DIM ** -0.5)
    pad = LANES - IDX_DIM
    c_tail = np.where(np.arange(pad) < IDX_HEADS, wi_scale, np.float32(0.0)).astype(np.float32)
    ts = [
        np.concatenate([ts[0], np.broadcast_to(c_tail, (seq, pad))], axis=1),
        np.concatenate([ts[1], np.zeros((seq, pad), np.float32)], axis=1),
        np.concatenate([ts[2], np.zeros((seq, pad), np.float32)], axis=1),
    ]
    return jnp.asarray(np.stack(ta + ti + ts, axis=0))


def _in_proj(x2, mod3, g_pre, w_in, seq):
    m_rows = x2.shape[0]
    tm = min(1024, seq)
    assert seq % tm == 0 and m_rows % tm == 0
    per_seq = seq // tm
    o_q, o_k, o_qi = POOL_WIDTH, POOL_WIDTH + ATTN_WIDTH, POOL_WIDTH + ATTN_WIDTH + 2 * KV_WIDTH
    o_ki = o_qi + IDX_HEADS * IDX_DIM
    o_gl = o_ki + IDX_DIM + IDX_HEADS
    assert all(o % IN_TN == 0 for o in (o_q, o_k, o_qi, o_ki)) and 2 * KV_WIDTH == IN_TN
    w_main = jnp.concatenate([w_in[:, :o_ki], w_in[:, o_gl:]], axis=1).astype(BF16)
    w_small = jnp.concatenate(
        [w_in[:, o_ki:o_gl], jnp.zeros((D_MODEL, LANES - IDX_DIM - IDX_HEADS), F32)], axis=1
    ).astype(BF16)
    tabs = _rope_tables(seq)
    n_j = PROJ_WIDTH // IN_TN
    n_gl = Q_OFF // IN_TN

    def src_tile(j):
        return jnp.where(j < n_gl, o_ki // IN_TN + j,
                         jnp.where(j < U_OFF // IN_TN, o_q // IN_TN + j - n_gl,
                                   jnp.where(j == U_OFF // IN_TN, 0,
                                             jnp.where(j == QI_OFF // IN_TN, o_qi // IN_TN, o_k // IN_TN))))

    return pl.pallas_call(
        _in_kernel,
        grid=(m_rows // tm, n_j),
        in_specs=[
            pl.BlockSpec((tm, D_MODEL), lambda i, j: (i, 0)),
            pl.BlockSpec((1, N_MOD, D_MODEL), lambda i, j: (i // per_seq, 0, 0)),
            pl.BlockSpec((1, D_MODEL), lambda i, j: (0, 0)),
            pl.BlockSpec((D_MODEL, IN_TN), lambda i, j: (0, src_tile(j))),
            pl.BlockSpec((D_MODEL, LANES), lambda i, j: (0, 0)),
            pl.BlockSpec((9, tm, LANES), lambda i, j: (0, i % per_seq, 0)),
        ],
        out_specs=[
            pl.BlockSpec((tm, IN_TN), lambda i, j: (i, j)),
            pl.BlockSpec((tm, LANES), lambda i, j: (i, 0)),
        ],
        out_shape=[
            jax.ShapeDtypeStruct((m_rows, PROJ_WIDTH), BF16),
            jax.ShapeDtypeStruct((m_rows, LANES), F32),
        ],
        scratch_shapes=[pltpu.VMEM((tm, D_MODEL), BF16), pltpu.VMEM((tm, IN_TN), F32)],
        compiler_params=_cparams(("arbitrary", "arbitrary")),
        name="in_proj",
    )(x2, mod3, g_pre.reshape(1, D_MODEL), w_main, w_small, tabs)


def _attn_kernel(q_ref, qi_ref, k_ref, v_ref, kis_ref, wis_ref, o_ref,
                 key_ref, plim_ref, qs_ref, m_ref, l_ref, acc_ref, *, tq, kc, seq, n_keep):
    i = pl.program_id(1)
    nk = ((i + 1) * tq + kc - 1) // kc
    nt = (((1,), (1,)), ((), ()))
    grp = N_HEADS // N_KV_HEADS
    row_io = lax.broadcasted_iota(I32, (kc, tq), 0)
    tpos = i * tq + lax.broadcasted_iota(I32, (kc, tq), 1)
    w_t = wis_ref[...].T

    def idx_chunk(c, carry):
        r0 = pl.multiple_of(c * kc, kc)
        ki = kis_ref[pl.ds(r0, kc), :IDX_DIM].astype(BF16)
        sc = jnp.zeros((kc, tq), F32)
        for h in range(IDX_HEADS):
            qh = qi_ref[:, h * IDX_DIM:(h + 1) * IDX_DIM]
            d = lax.dot_general(ki, qh, nt, preferred_element_type=F32)
            sc = sc + jnp.maximum(d, 0.0) * w_t[IDX_DIM + h:IDX_DIM + h + 1, :]
        bits = lax.bitcast_convert_type(sc, I32)
        key = jnp.where(bits < 0, bits ^ np.int32(0x7FFFFFFF), bits)
        key_ref[pl.ds(r0, kc), :] = jnp.where(r0 + row_io <= tpos, key, INT_MIN)
        return carry

    lax.fori_loop(0, nk, idx_chunk, 0)

    def count(pred):
        def body(c, acc):
            r0 = pl.multiple_of(c * kc, kc)
            hit = pred(key_ref[pl.ds(r0, kc), :], r0)
            return acc + jnp.sum(hit.astype(I32), axis=0, keepdims=True)

        return lax.fori_loop(0, nk, body, jnp.zeros((1, tq), I32))

    def bit_step(it, tu):
        cand = tu | lax.shift_left(np.int32(1), 31 - it)
        cnt = count(lambda kch, r0: kch >= (cand ^ INT_MIN))
        return jnp.where(cnt >= n_keep, cand, tu)

    tu = lax.fori_loop(0, 32, bit_step, jnp.zeros((1, tq), I32))
    thr = tu ^ INT_MIN
    cnt_gt = count(lambda kch, r0: kch > thr)
    cnt_ge = count(lambda kch, r0: kch >= thr)
    tie_i = ((cnt_ge > n_keep) & (tu != 0)).astype(I32)
    need = n_keep - cnt_gt

    plim_ref[...] = jnp.full(plim_ref.shape, seq, I32)

    @pl.when(jnp.max(tie_i) > 0)
    def _():
        n_bits = (seq - 1).bit_length()

        def idx_step(it, p):
            cand = p | lax.shift_left(np.int32(1), (n_bits - 1) - it)
            below = count(lambda kch, r0: (kch == thr) & (r0 + row_io < cand))
            return jnp.where(below < need, cand, p)

        p = lax.fori_loop(0, n_bits, idx_step, jnp.zeros((1, tq), I32))
        plim_ref[...] = jnp.broadcast_to(jnp.where(tie_i > 0, p, seq), plim_ref.shape)

    for g in range(N_KV_HEADS):
        for r in range(grp):
            hh = g * grp + r
            qs_ref[g, r * tq:(r + 1) * tq, :] = q_ref[:, hh * HEAD_DIM:(hh + 1) * HEAD_DIM]
    m_ref[...] = jnp.full(m_ref.shape, MASK_NEG, F32)
    l_ref[...] = jnp.zeros_like(l_ref)
    acc_ref[...] = jnp.zeros_like(acc_ref)
    c_exp = HEAD_DIM ** -0.5 * np.log2(np.e)

    def att_chunk(c, carry):
        r0 = pl.multiple_of(c * kc, kc)
        kch = key_ref[pl.ds(r0, kc), :]
        row = r0 + row_io
        sel = ((kch > thr) | ((kch == thr) & (row <= plim_ref[0:1, :]))) & (row <= tpos)
        bias_t = jnp.where(sel, 0.0, MASK_NEG).astype(F32).T
        for g in range(N_KV_HEADS):
            kg = k_ref[pl.ds(r0, kc), g * HEAD_DIM:(g + 1) * HEAD_DIM]
            vg = v_ref[pl.ds(r0, kc), g * HEAD_DIM:(g + 1) * HEAD_DIM]
            s = lax.dot_general(qs_ref[g], kg, nt, preferred_element_type=F32)
            s = (s.reshape(grp, tq, kc) + bias_t[None]).reshape(grp * tq, kc)
            m_old = m_ref[g]
            m_new = jnp.maximum(m_old, jnp.max(s, axis=-1, keepdims=True))
            alpha = jnp.exp2((m_old - m_new) * c_exp)
            p = jnp.exp2((s - m_new) * c_exp)
            l_ref[g] = alpha * l_ref[g] + jnp.sum(p, axis=-1, keepdims=True)
            acc_ref[g] = alpha * acc_ref[g] + jnp.dot(p.astype(BF16), vg, preferred_element_type=F32)
            m_ref[g] = m_new
        return carry

    lax.fori_loop(0, nk, att_chunk, 0)
    for g in range(N_KV_HEADS):
        o = acc_ref[g] / l_ref[g]
        for r in range(grp):
            hh = g * grp + r
            o_ref[:, hh * HEAD_DIM:(hh + 1) * HEAD_DIM] = o[r * tq:(r + 1) * tq, :].astype(BF16)


def _attn(proj, small, bsz, seq):
    tq = min(ATTN_TQ, seq)
    kc = min(ATTN_KC, seq)
    assert seq % tq == 0 and seq % kc == 0 and kc % tq == 0
    nq = seq // tq
    n_keep = min(TOPK_MAX, seq // 4)
    grp = N_HEADS // N_KV_HEADS
    kern = functools.partial(_attn_kernel, tq=tq, kc=kc, seq=seq, n_keep=n_keep)
    return pl.pallas_call(
        kern,
        grid=(bsz, nq),
        in_specs=[
            pl.BlockSpec((tq, ATTN_WIDTH), lambda b, i: (b * nq + i, Q_OFF // ATTN_WIDTH)),
            pl.BlockSpec((tq, IDX_HEADS * IDX_DIM), lambda b, i: (b * nq + i, QI_OFF // (IDX_HEADS * IDX_DIM))),
            pl.BlockSpec((seq, KV_WIDTH), lambda b, i: (b, K_OFF // KV_WIDTH)),
            pl.BlockSpec((seq, KV_WIDTH), lambda b, i: (b, V_OFF // KV_WIDTH)),
            pl.BlockSpec((seq, LANES), lambda b, i: (b, 0)),
            pl.BlockSpec((tq, LANES), lambda b, i: (b * nq + i, 0)),
        ],
        out_specs=pl.BlockSpec((tq, ATTN_WIDTH), lambda b, i: (b * nq + i, 0)),
        out_shape=jax.ShapeDtypeStruct((bsz * seq, ATTN_WIDTH), BF16),
        scratch_shapes=[
            pltpu.VMEM((seq, tq), I32),
            pltpu.VMEM((8, tq), I32),
            pltpu.VMEM((N_KV_HEADS, grp * tq, HEAD_DIM), BF16),
            pltpu.VMEM((N_KV_HEADS, grp * tq, 1), F32),
            pltpu.VMEM((N_KV_HEADS, grp * tq, 1), F32),
            pltpu.VMEM((N_KV_HEADS, grp * tq, HEAD_DIM), F32),
        ],
        compiler_params=_cparams(("arbitrary", "arbitrary")),
        name="attn",
    )(proj, proj, proj, proj, small, small)


def _merge_kernel(
    u_ref, halo_ref, ya_ref, gl_ref, x_ref, mod_ref, wgrp_ref, pscale_ref, wup_p_ref, wup_a_ref, wout_ref,
    gpost_ref, gffn_ref, wr_ref, br_ref,
    x1_ref, hp_ref, route_ref, cnt_ref,
    carry_ref, *, seq,
):
    tm = MERGE_TM
    i = pl.program_id(0)
    pos0 = (i * tm) % seq

    @pl.when(i == 0)
    def _():
        carry_ref[...] = jnp.zeros_like(carry_ref)

    halo_rows = halo_ref.shape[0]
    halo = jnp.where(pos0 == 0, 0.0, halo_ref[...].astype(F32))
    ext = jnp.concatenate([halo, u_ref[...].astype(F32)], axis=0)
    pos = pos0 + lax.broadcasted_iota(I32, (tm, 1), 0)
    ys = []
    for g, win in enumerate(POOL_WINDOWS):
        sl = slice(g * POOL_GROUP_DIM, (g + 1) * POOL_GROUP_DIM)
        e = ext[:, sl]
        acc = e
        span = 1
        while span < win:
            acc = acc + pltpu.roll(acc, span, 0)
            span *= 2
        cnt = jnp.minimum(pos + 1, win).astype(F32)
        mixed = (acc[halo_rows:] / cnt - e[halo_rows:]).astype(BF16)
        ys.append(jnp.dot(mixed, wgrp_ref[g], preferred_element_type=F32))
    y_pool = (jnp.concatenate(ys, axis=1) * pscale_ref[...]).astype(BF16)

    gates = jax.nn.sigmoid(gl_ref[...].astype(F32))
    up_p = jnp.dot(y_pool, wup_p_ref[...], preferred_element_type=F32)
    up_a = jnp.dot(ya_ref[...], wup_a_ref[...], preferred_element_type=F32)
    merged = (gates[:, :D_MODEL] * up_p + gates[:, D_MODEL:] * up_a).astype(BF16)
    y = jnp.dot(merged, wout_ref[...], preferred_element_type=F32)
    m = mod_ref[0]
    x1 = x_ref[...] + m[2:3] * _rms(y, gpost_ref[...])
    x1_ref[...] = x1

    h2f = _rms(x1, gffn_ref[...]) * (1.0 + m[4:5]) + m[3:4]
    h2 = h2f.astype(BF16)
    hp_ref[...] = _pack_bf16_pairs(h2f)

    lane = lax.broadcasted_iota(I32, (tm, LANES), 1)
    logits = jnp.dot(h2, wr_ref[...], preferred_element_type=F32) + br_ref[...]
    work = jnp.where(lane < N_EXPERTS, logits, -jnp.inf)
    vals, hots = [], []
    for _ in range(TOP_K):
        mx = jnp.max(work, axis=-1, keepdims=True)
        first = jnp.min(jnp.where(work == mx, lane, LANES), axis=-1, keepdims=True)
        hot = lane == first
        vals.append(mx)
        hots.append(hot)
        work = jnp.where(hot, -jnp.inf, work)
    ex = [jnp.exp(v - vals[0]) for v in vals]
    den = ex[0] + ex[1] + ex[2] + ex[3]

    onehot = jnp.zeros((tm, LANES), F32)
    for hot in hots:
        onehot = onehot + hot.astype(F32)
    r_i = lax.broadcasted_iota(I32, (tm, tm), 0)
    c_i = lax.broadcasted_iota(I32, (tm, tm), 1)
    tril = (c_i < r_i).astype(BF16)
    before = jnp.dot(tril, onehot.astype(BF16), preferred_element_type=F32) + carry_ref[0:1, :]
    out = jnp.zeros((tm, LANES), F32)
    lane_f = lane.astype(F32)
    for k in range(TOP_K):
        out = jnp.where(lane == k, ex[k] / den, out)
        idx_k = jnp.sum(jnp.where(hots[k], lane_f, 0.0), axis=-1, keepdims=True)
        out = jnp.where(lane == TOP_K + k, idx_k, out)
        rank_k = jnp.sum(jnp.where(hots[k], before, 0.0), axis=-1, keepdims=True)
        out = jnp.where(lane == 2 * TOP_K + k, rank_k, out)
    route_ref[...] = out
    total = carry_ref[0:1, :] + jnp.sum(onehot, axis=0, keepdims=True)
    carry_ref[...] = jnp.broadcast_to(total, carry_ref.shape)
    cnt_ref[...] = jnp.broadcast_to(total, cnt_ref.shape)


def _merge(proj, y_attn, x2, mod3, w_pool_grp, pool_scale, w_up_pool, w_up_attn, w_out, g_post_mix, g_pre_ffn,
           w_router, b_router, seq):
    m_rows = x2.shape[0]
    tm = MERGE_TM
    halo = 16
    assert seq % tm == 0 and max(POOL_WINDOWS) <= halo
    per_seq = seq // tm
    wr = jnp.concatenate([w_router, jnp.zeros((D_MODEL, LANES - N_EXPERTS), F32)], axis=1).astype(BF16)
    br = jnp.concatenate([b_router, jnp.zeros((LANES - N_EXPERTS,), F32)]).reshape(1, LANES)
    const = lambda shape: pl.BlockSpec(shape, lambda i: (0,) * len(shape), pipeline_mode=pl.Buffered(1))
    kern = functools.partial(_merge_kernel, seq=seq)
    return pl.pallas_call(
        kern,
        grid=(m_rows // tm,),
        in_specs=[
            pl.BlockSpec((tm, POOL_WIDTH), lambda i: (i, U_OFF // POOL_WIDTH)),
            pl.BlockSpec((halo, POOL_WIDTH), lambda i: (jnp.maximum(i * (tm // halo) - 1, 0), U_OFF // POOL_WIDTH)),
            pl.BlockSpec((tm, ATTN_WIDTH), lambda i: (i, 0)),
            pl.BlockSpec((tm, N_BRANCHES * D_MODEL), lambda i: (i, 0)),
            pl.BlockSpec((tm, D_MODEL), lambda i: (i, 0)),
            pl.BlockSpec((1, N_MOD, D_MODEL), lambda i: (i // per_seq, 0, 0)),
            const((POOL_GROUPS, POOL_GROUP_DIM, POOL_GROUP_DIM)),
            const((1, POOL_WIDTH)),
            const((POOL_WIDTH, D_MODEL)),
            const((ATTN_WIDTH, D_MODEL)),
            const((D_MODEL, D_MODEL)),
            const((1, D_MODEL)),
            const((1, D_MODEL)),
            const((D_MODEL, LANES)),
            const((1, LANES)),
        ],
        out_specs=[
            pl.BlockSpec((tm, D_MODEL), lambda i: (i, 0)),
            pl.BlockSpec((tm, D_MODEL // 2), lambda i: (i, 0)),
            pl.BlockSpec((tm, LANES), lambda i: (i, 0)),
            pl.BlockSpec((8, LANES), lambda i: (0, 0)),
        ],
        out_shape=[
            jax.ShapeDtypeStruct((m_rows, D_MODEL), F32),
            jax.ShapeDtypeStruct((m_rows, D_MODEL // 2), U32),
            jax.ShapeDtypeStruct((m_rows, LANES), F32),
            jax.ShapeDtypeStruct((8, LANES), F32),
        ],
        scratch_shapes=[pltpu.VMEM((8, LANES), F32)],
        compiler_params=_cparams(("arbitrary",)),
        name="merge",
    )(
        proj, proj, y_attn, proj, x2, mod3,
        w_pool_grp.astype(BF16), pool_scale.reshape(1, POOL_WIDTH), w_up_pool.astype(BF16),
        w_up_attn.astype(BF16), w_out.astype(BF16), g_post_mix.reshape(1, D_MODEL), g_pre_ffn.reshape(1, D_MODEL),
        wr, br,
    )


def _dispatch_kernel(zf_ref, pos_ref, h_ref, xs_ref, zbuf, sem, zsem):
    tm = h_ref.shape[0]
    bm = zbuf.shape[0]

    @pl.when(pl.program_id(0) == 0)
    def _():
        zbuf[...] = jnp.zeros_like(zbuf)

        def zero_copy(b):
            return pltpu.make_async_copy(zbuf, xs_ref.at[pl.ds(b * bm, bm), :], zsem)

        def z_issue(b, carry):
            pl.when(zf_ref[b] != 0)(lambda: zero_copy(b).start())
            return carry

        def z_drain(b, carry):
            pl.when(zf_ref[b] != 0)(lambda: zero_copy(b).wait())
            return carry

        lax.fori_loop(0, zf_ref.shape[0], z_issue, 0)
        lax.fori_loop(0, zf_ref.shape[0], z_drain, 0)

    def row_copy(t, k):
        return pltpu.make_async_copy(
            h_ref.at[pl.ds(t, 1), :], xs_ref.at[pl.ds(pos_ref[k * tm + t], 1), :], sem
        )

    def issue(t, carry):
        for k in range(TOP_K):
            row_copy(t, k).start(priority=k % 2)
        return carry

    lax.fori_loop(0, tm, issue, 0, unroll=ROW_DMA_UNROLL)
    for _ in range(TOP_K):
        pltpu.make_async_copy(h_ref, xs_ref.at[pl.ds(0, tm), :], sem).wait()


def _tile_major(pos_t, tm):
    return pos_t.reshape(TOP_K, pos_t.shape[1] // tm, tm).transpose(1, 0, 2).reshape(-1)


def _dispatch(zero_flag, pos_t, hp, n_rows):
    m_rows, width = hp.shape
    tm = min(DISPATCH_TM, m_rows)
    assert m_rows % tm == 0
    pos_flat = _tile_major(pos_t, tm)
    grid_spec = pltpu.PrefetchScalarGridSpec(
        num_scalar_prefetch=1,
        grid=(m_rows // tm,),
        in_specs=[
            pl.BlockSpec((tm * TOP_K,), lambda i, zf: (i,), memory_space=pltpu.SMEM),
            pl.BlockSpec((tm, width), lambda i, zf: (i, 0)),
        ],
        out_specs=pl.BlockSpec(memory_space=pl.ANY),
        scratch_shapes=[pltpu.VMEM((MOE_BM, width), U32), pltpu.SemaphoreType.DMA(()), pltpu.SemaphoreType.DMA(())],
    )
    return pl.pallas_call(
        _dispatch_kernel,
        grid_spec=grid_spec,
        out_shape=jax.ShapeDtypeStruct((n_rows, width), U32),
        compiler_params=_cparams(("arbitrary",)),
        name="dispatch",
    )(zero_flag, pos_flat, hp)


def _expert_rows(e, j, bs_ref, nb_ref, nu_ref, x_hbm, o_hbm, xbuf, obuf, xsem, osem, compute, prepare):
    n_x, bm = xbuf.shape[0], xbuf.shape[1]
    tw = obuf.shape[2]
    n_blocks = x_hbm.shape[0] // bm
    b0 = bs_ref[e]
    nb = nb_ref[e]
    n_used = nu_ref[0]

    def x_copy(blk, slot):
        return pltpu.make_async_copy(x_hbm.at[pl.ds(blk * bm, bm), :], xbuf.at[slot], xsem.at[slot])

    def o_copy(blk, slot):
        return pltpu.make_async_copy(obuf.at[slot], o_hbm.at[pl.ds(blk * bm, bm), pl.ds(j * tw, tw)], osem.at[slot])

    @pl.when(nb > 0)
    def _():
        prepare()

        def body(rb, carry):
            g = b0 + rb
            slot = g % 2
            ahead = n_x - 1

            @pl.when(g == 0)
            def _():
                for a in range(ahead):
                    pl.when(a < n_used)(lambda a=a: x_copy(a, a).start())

            x_copy(g, g % n_x).wait()

            @pl.when(g + ahead < n_used)
            def _():
                x_copy(g + ahead, (g + ahead) % n_x).start()

            @pl.when(g >= 2)
            def _():
                o_copy(g - 2, slot).wait()

            obuf[slot] = compute(xbuf[g % n_x])
            o_copy(g, slot).start()

            @pl.when(g == n_used - 1)
            def _():
                @pl.when(g >= 1)
                def _():
                    o_copy(g - 1, 1 - slot).wait()

                o_copy(g, slot).wait()

            return carry

        lax.fori_loop(0, nb, body, 0)

    @pl.when(e == N_EXPERTS - 1)
    def _():
        obuf[0] = jnp.zeros(obuf.shape[1:], obuf.dtype)

        def z_issue(blk, carry):
            o_copy(blk, 0).start()
            return carry

        def z_drain(blk, carry):
            o_copy(blk, 0).wait()
            return carry

        lax.fori_loop(nu_ref[0], n_blocks, z_issue, 0)
        lax.fori_loop(nu_ref[0], n_blocks, z_drain, 0)


def _ffn1_kernel(bs_ref, nb_ref, nu_ref, x_hbm, wg_ref, wl_ref, bg_ref, bl_ref, o_hbm,
                 wg_s, wl_s, xbuf, obuf, xsem, osem):
    j, e = pl.program_id(0), pl.program_id(1)
    half = D_MODEL // 2

    def prepare():
        wg_s[...] = wg_ref[0].astype(BF16)
        wl_s[...] = wl_ref[0].astype(BF16)

    def compute(xp):
        lo, hi = (v.astype(BF16) for v in _unpack_bf16_pairs(xp))

        def proj(w_s, b_ref):
            return (
                jnp.dot(lo, w_s[:half, :], preferred_element_type=F32)
                + jnp.dot(hi, w_s[half:, :], preferred_element_type=F32)
                + b_ref[0]
            )

        glu = jnp.minimum(proj(wg_s, bg_ref), SWIGLU_LIMIT)
        lin = jnp.clip(proj(wl_s, bl_ref), -SWIGLU_LIMIT, SWIGLU_LIMIT)
        return (glu * jax.nn.sigmoid(SWIGLU_ALPHA * glu) * (lin + 1.0)).astype(BF16)

    _expert_rows(e, j, bs_ref, nb_ref, nu_ref, x_hbm, o_hbm, xbuf, obuf, xsem, osem, compute, prepare)


def _ffn1(blk_start, blk_count, n_used, xs, w1, b1):
    n_rows = xs.shape[0]
    bm, tn = MOE_BM, FFN1_TN
    nj = D_FF // tn
    b1r = b1.reshape(N_EXPERTS, 1, 2 * D_FF)
    grid_spec = pltpu.PrefetchScalarGridSpec(
        num_scalar_prefetch=3,
        grid=(nj, N_EXPERTS),
        in_specs=[
            pl.BlockSpec(memory_space=pl.ANY),
            pl.BlockSpec((1, D_MODEL, tn), lambda j, e, *_: (e, 0, j)),
            pl.BlockSpec((1, D_MODEL, tn), lambda j, e, *_: (e, 0, nj + j)),
            pl.BlockSpec((1, 1, tn), lambda j, e, *_: (e, 0, j)),
            pl.BlockSpec((1, 1, tn), lambda j, e, *_: (e, 0, nj + j)),
        ],
        out_specs=pl.BlockSpec(memory_space=pl.ANY),
        scratch_shapes=[
            pltpu.VMEM((D_MODEL, tn), BF16),
            pltpu.VMEM((D_MODEL, tn), BF16),
            pltpu.VMEM((FFN_X_SLOTS, bm, D_MODEL // 2), U32),
            pltpu.VMEM((2, bm, tn), BF16),
            pltpu.SemaphoreType.DMA((FFN_X_SLOTS,)),
            pltpu.SemaphoreType.DMA((2,)),
        ],
    )
    return pl.pallas_call(
        _ffn1_kernel,
        grid_spec=grid_spec,
        out_shape=jax.ShapeDtypeStruct((n_rows, D_FF), BF16),
        compiler_params=_cparams(("arbitrary", "arbitrary")),
        name="ffn1",
    )(blk_start, blk_count, n_used, xs, w1, w1, b1r, b1r)


def _ffn2_kernel(bs_ref, nb_ref, nu_ref, a_hbm, w_ref, b_ref, o_hbm, w_s, xbuf, obuf, xsem, osem):
    j, e = pl.program_id(0), pl.program_id(1)

    def prepare():
        w_s[...] = w_ref[0].astype(BF16)

    def compute(a):
        return _pack_bf16_pairs(jnp.dot(a, w_s[...], preferred_element_type=F32) + b_ref[0])

    _expert_rows(e, j, bs_ref, nb_ref, nu_ref, a_hbm, o_hbm, xbuf, obuf, xsem, osem, compute, prepare)


def _ffn2(blk_start, blk_count, n_used, act, w2, b2):
    n_rows = act.shape[0]
    bm, tn = MOE_BM, FFN2_TN
    nj = D_MODEL // tn
    b2r = b2.reshape(N_EXPERTS, 1, D_MODEL)
    grid_spec = pltpu.PrefetchScalarGridSpec(
        num_scalar_prefetch=3,
        grid=(nj, N_EXPERTS),
        in_specs=[
            pl.BlockSpec(memory_space=pl.ANY),
            pl.BlockSpec((1, D_FF, tn), lambda j, e, *_: (e, 0, j)),
            pl.BlockSpec((1, 1, tn), lambda j, e, *_: (e, 0, j)),
        ],
        out_specs=pl.BlockSpec(memory_space=pl.ANY),
        scratch_shapes=[
            pltpu.VMEM((D_FF, tn), BF16),
            pltpu.VMEM((FFN_X_SLOTS, bm, D_FF), BF16),
            pltpu.VMEM((2, bm, tn // 2), U32),
            pltpu.SemaphoreType.DMA((FFN_X_SLOTS,)),
            pltpu.SemaphoreType.DMA((2,)),
        ],
    )
    return pl.pallas_call(
        _ffn2_kernel,
        grid_spec=grid_spec,
        out_shape=jax.ShapeDtypeStruct((n_rows, D_MODEL // 2), U32),
        compiler_params=_cparams(("arbitrary", "arbitrary")),
        name="ffn2",
    )(blk_start, blk_count, n_used, act, w2, b2r)


def _combine_kernel(pos_ref, ys_ref, route_ref, x1_ref, mod_ref, g_ref, o_ref, buf, sem):
    tm = x1_ref.shape[0]

    def row_copy(t, k):
        return pltpu.make_async_copy(
            ys_ref.at[pl.ds(pos_ref[k * tm + t], 1), :], buf.at[k, pl.ds(t, 1), :], sem
        )

    def issue(t, carry):
        for k in range(TOP_K):
            row_copy(t, k).start(priority=k % 2)
        return carry

    lax.fori_loop(0, tm, issue, 0, unroll=ROW_DMA_UNROLL)
    for k in range(TOP_K):
        pltpu.make_async_copy(ys_ref.at[pl.ds(0, tm), :], buf.at[k], sem).wait()
    route = route_ref[...]
    y_lo, y_hi = (route[:, 0:1] * v for v in _unpack_bf16_pairs(buf[0]))
    for k in range(1, TOP_K):
        lo, hi = _unpack_bf16_pairs(buf[k])
        y_lo = y_lo + route[:, k:k + 1] * lo
        y_hi = y_hi + route[:, k:k + 1] * hi
    hw = FFN2_TN // 2
    parts = []
    for t in range(D_MODEL // FFN2_TN):
        parts += [y_lo[:, t * hw:(t + 1) * hw], y_hi[:, t * hw:(t + 1) * hw]]
    y = jnp.concatenate(parts, axis=1)
    m = mod_ref[0]
    o_ref[...] = x1_ref[...] + m[5:6] * _rms(y, g_ref[...])


def _combine(pos_t, ys, route, x1, mod3, g_post_ffn, seq):
    m_rows = x1.shape[0]
    tm = min(COMBINE_TM, seq)
    assert seq % tm == 0
    pos_flat = _tile_major(pos_t, tm)
    per_seq = seq // tm
    return pl.pallas_call(
        _combine_kernel,
        grid=(m_rows // tm,),
        in_specs=[
            pl.BlockSpec((tm * TOP_K,), lambda i: (i,), memory_space=pltpu.SMEM),
            pl.BlockSpec(memory_space=pl.ANY),
            pl.BlockSpec((tm, LANES), lambda i: (i, 0)),
            pl.BlockSpec((tm, D_MODEL), lambda i: (i, 0)),
            pl.BlockSpec((1, N_MOD, D_MODEL), lambda i: (i // per_seq, 0, 0)),
            pl.BlockSpec((1, D_MODEL), lambda i: (0, 0)),
        ],
        out_specs=pl.BlockSpec((tm, D_MODEL), lambda i: (i, 0)),
        out_shape=jax.ShapeDtypeStruct((m_rows, D_MODEL), F32),
        scratch_shapes=[pltpu.VMEM((TOP_K, tm, D_MODEL // 2), U32), pltpu.SemaphoreType.DMA(())],
        compiler_params=_cparams(("arbitrary",)),
        name="combine",
    )(pos_flat, ys, route, x1, mod3, g_post_ffn.reshape(1, D_MODEL))


def _layer(x2, mod3, bsz, seq, g_pre_mix, g_post_mix, w_in, w_pool_grp, pool_scale, w_up_pool, w_up_attn, w_out,
           g_pre_ffn, g_post_ffn, w_router, b_router, w1, b1, w2, b2):
    m_rows = x2.shape[0]
    proj, small = _in_proj(x2, mod3, g_pre_mix, w_in, seq)
    y_attn = _attn(proj, small, bsz, seq)
    x1, hp, route, cnt = _merge(proj, y_attn, x2, mod3, w_pool_grp, pool_scale, w_up_pool, w_up_attn, w_out,
                                g_post_mix, g_pre_ffn, w_router, b_router, seq)

    bm = MOE_BM
    n_slots = m_rows * TOP_K
    n_blocks = -(-n_slots // bm) + N_EXPERTS
    idx_t = route[:, TOP_K:2 * TOP_K].T.astype(I32)
    rank_t = route[:, 2 * TOP_K:3 * TOP_K].T.astype(I32)
    counts = cnt[0, :N_EXPERTS].astype(I32)
    padded = (counts + bm - 1) // bm * bm
    pad_end = jnp.cumsum(padded)
    pad_start = pad_end - padded
    pos_t = pad_start[idx_t] + rank_t
    n_used = (pad_end[-1] // bm).astype(I32).reshape(1)
    blk_ids = jnp.arange(n_blocks, dtype=I32)
    blk_start = (pad_start // bm).astype(I32)
    blk_count = (padded // bm).astype(I32)
    last_of_expert = jnp.any((blk_ids[:, None] + 1) * bm == pad_end[None, :], axis=1)
    zero_flag = (last_of_expert | (blk_ids >= n_used[0])).astype(I32)

    xs = _dispatch(zero_flag, pos_t, hp, n_blocks * bm)
    act = _ffn1(blk_start, blk_count, n_used, xs, w1, b1)
    ys = _ffn2(blk_start, blk_count, n_used, act, w2, b2)
    return _combine(pos_t, ys, route, x1, mod3, g_post_ffn, seq)


def kernel(x, c, w_ada, b_ada, g_pre_mix, g_post_mix, w_in, w_pool_grp, pool_scale, w_up_pool, w_up_attn, w_out,
           g_pre_ffn, g_post_ffn, w_router, b_router, w1, b1, w2, b2):
    bsz, seq, d = x.shape
    assert d == D_MODEL
    depth = w_ada.shape[0]
    x2 = x.reshape(bsz * seq, d)
    for layer in range(depth):
        mod3 = _ada(c, w_ada[layer], b_ada[layer]).reshape(bsz, N_MOD, D_MODEL)
        x2 = _layer(x2, mod3, bsz, seq, g_pre_mix[layer], g_post_mix[layer], w_in[layer], w_pool_grp[layer],
                    pool_scale[layer], w_up_pool[layer], w_up_attn[layer], w_out[layer], g_pre_ffn[layer],
                    g_post_ffn[layer], w_router[layer], b_router[layer], w1[layer], b1[layer], w2[layer], b2[layer])
    return x2.reshape(bsz, seq, d)
```

```python
import functools

import jax
import jax.numpy as jnp
import numpy as np
from jax import lax
from jax.experimental import pallas as pl
from jax.experimental.pallas import tpu as pltpu

F32 = jnp.float32
BF16 = jnp.bfloat16
I32 = jnp.int32
U32 = jnp.uint32

D_MODEL = 2048
POOL_WINDOWS = (2, 4, 8, 16)
POOL_GROUPS = 4
POOL_GROUP_DIM = D_MODEL // 8
POOL_WIDTH = POOL_GROUPS * POOL_GROUP_DIM
N_HEADS = 16
N_KV_HEADS = 4
HEAD_DIM = 128
ATTN_WIDTH = N_HEADS * HEAD_DIM
KV_WIDTH = N_KV_HEADS * HEAD_DIM
ROT_DIM = HEAD_DIM // 4
IDX_HEADS = 16
IDX_DIM = 64
IDX_ROT_DIM = IDX_DIM // 4
TOPK_MAX = 256
ROPE_THETA = 500000.0
N_BRANCHES = 2
N_EXPERTS = 32
TOP_K = 4
D_FF = D_MODEL
SWIGLU_ALPHA = 1.702
SWIGLU_LIMIT = 7.0
N_MOD = 6
EPS = 1e-6

LANES = 128
INT_MIN = np.int32(-(2 ** 31))

GL_OFF = 0
Q_OFF = GL_OFF + N_BRANCHES * D_MODEL
U_OFF = Q_OFF + ATTN_WIDTH
QI_OFF = U_OFF + POOL_WIDTH
K_OFF = QI_OFF + IDX_HEADS * IDX_DIM
V_OFF = K_OFF + KV_WIDTH
PROJ_WIDTH = V_OFF + KV_WIDTH

IN_TN = 1024
ATTN_TQ = 512
ATTN_KC = 512
MASK_NEG = -1e30
MERGE_TM = 256
MOE_BM = 512
FFN1_TN = 512
FFN2_TN = 1024
DISPATCH_TM = 1024
COMBINE_TM = 512
ROW_DMA_UNROLL = 8
FFN_X_SLOTS = 3
VMEM_LIMIT = 56 * 1024 * 1024


def _cparams(sem):
    return pltpu.CompilerParams(dimension_semantics=sem, vmem_limit_bytes=VMEM_LIMIT)


def _rms(x, g):
    return x * lax.rsqrt(jnp.mean(x * x, axis=-1, keepdims=True) + EPS) * g


def _pack_bf16_pairs(x):
    n = x.shape[1] // 2
    bits = lax.bitcast_convert_type(x.astype(BF16).astype(F32), U32)
    return bits[:, n:] | lax.shift_right_logical(bits[:, :n], np.uint32(16))


def _unpack_bf16_pairs(w):
    lo = lax.bitcast_convert_type(lax.shift_left(w, np.uint32(16)), F32)
    hi = lax.bitcast_convert_type(w & np.uint32(0xFFFF0000), F32)
    return lo, hi


def _ada_kernel(c_ref, w_ref, b_ref, o_ref):
    c = c_ref[...]
    ca = (c * jax.nn.sigmoid(c)).astype(BF16)
    o_ref[...] = jnp.dot(ca, w_ref[...].astype(BF16), preferred_element_type=F32) + b_ref[...]


def _ada(c, w_ada, b_ada):
    bsz = c.shape[0]
    n = w_ada.shape[1]
    tn = 1024
    return pl.pallas_call(
        _ada_kernel,
        grid=(n // tn,),
        in_specs=[
            pl.BlockSpec((bsz, D_MODEL), lambda j: (0, 0)),
            pl.BlockSpec((D_MODEL, tn), lambda j: (0, j)),
            pl.BlockSpec((1, tn), lambda j: (0, j)),
        ],
        out_specs=pl.BlockSpec((bsz, tn), lambda j: (0, j)),
        out_shape=jax.ShapeDtypeStruct((bsz, n), F32),
        compiler_params=_cparams(("arbitrary",)),
        name="ada",
    )(c, w_ada, b_ada.reshape(1, n))


def _rope_slice(xs, c, s1, s2, shift):
    return xs * c + pltpu.roll(xs, LANES - shift, 1) * s1 + pltpu.roll(xs, shift, 1) * s2


def _in_kernel(x_ref, mod_ref, g_ref, w_ref, ws_ref, tab_ref, o_ref, os_ref, h_ref, acc_ref):
    j = pl.program_id(1)
    half_a = ROT_DIM // 2
    half_i = IDX_ROT_DIM // 2

    @pl.when(j == 0)
    def _():
        m = mod_ref[0]
        h = _rms(x_ref[...], g_ref[...]) * (1.0 + m[1:2]) + m[0:1]
        hb = h.astype(BF16)
        h_ref[...] = hb
        small = jnp.dot(hb, ws_ref[...], preferred_element_type=F32)
        os_ref[...] = _rope_slice(small, tab_ref[6], tab_ref[7], tab_ref[8], half_i)

    acc_ref[...] = jnp.dot(h_ref[...], w_ref[...], preferred_element_type=F32)
    n_sl = IN_TN // LANES
    j_q0, j_q1 = Q_OFF // IN_TN, U_OFF // IN_TN
    j_qi = QI_OFF // IN_TN
    j_kv = K_OFF // IN_TN
    k_sl = KV_WIDTH // LANES

    @pl.when((j < j_q0) | (j == U_OFF // IN_TN))
    def _():
        o_ref[...] = acc_ref[...].astype(BF16)

    @pl.when((j >= j_q0) & (j < j_q1))
    def _():
        for s in range(n_sl):
            sl = slice(s * LANES, (s + 1) * LANES)
            o_ref[:, sl] = _rope_slice(acc_ref[:, sl], tab_ref[0], tab_ref[1], tab_ref[2], half_a).astype(BF16)

    @pl.when(j == j_qi)
    def _():
        for s in range(n_sl):
            sl = slice(s * LANES, (s + 1) * LANES)
            o_ref[:, sl] = _rope_slice(acc_ref[:, sl], tab_ref[3], tab_ref[4], tab_ref[5], half_i).astype(BF16)

    @pl.when(j == j_kv)
    def _():
        for s in range(k_sl):
            sl = slice(s * LANES, (s + 1) * LANES)
            o_ref[:, sl] = _rope_slice(acc_ref[:, sl], tab_ref[0], tab_ref[1], tab_ref[2], half_a).astype(BF16)
        o_ref[:, KV_WIDTH:] = acc_ref[:, KV_WIDTH:].astype(BF16)


def _rope_tables(seq):
    def tabs(rot_dim, period, n_rep):
        half = rot_dim // 2
        inv = np.float32(ROPE_THETA) ** (-np.arange(0, rot_dim, 2, dtype=np.float32) / np.float32(rot_dim))
        ang = np.arange(seq, dtype=np.float32)[:, None] * inv.astype(np.float32)[None, :]
        cos, sin = np.cos(ang).astype(np.float32), np.sin(ang).astype(np.float32)
        ones = np.ones((seq, period - 2 * half), np.float32)
        z_h = np.zeros((seq, half), np.float32)
        c = np.concatenate([cos, cos, ones], axis=1)
        s1 = np.concatenate([-sin, z_h, 0 * ones], axis=1)
        s2 = np.concatenate([z_h, sin, 0 * ones], axis=1)
        return [np.tile(t, (1, n_rep)) for t in (c, s1, s2)]

    ta = tabs(ROT_DIM, HEAD_DIM, LANES // HEAD_DIM)
    ti = tabs(IDX_ROT_DIM, IDX_DIM, LANES // IDX_DIM)
    ts = tabs(IDX_ROT_DIM, IDX_DIM, 1)
    wi_scale = np.float32(IDX_HEADS ** -0.5 * IDX_DIM ** -0.5)
    pad = LANES - IDX_DIM
    c_tail = np.where(np.arange(pad) < IDX_HEADS, wi_scale, np.float32(0.0)).astype(np.float32)
    ts = [
        np.concatenate([ts[0], np.broadcast_to(c_tail, (seq, pad))], axis=1),
        np.concatenate([ts[1], np.zeros((seq, pad), np.float32)], axis=1),
        np.concatenate([ts[2], np.zeros((seq, pad), np.float32)], axis=1),
    ]
    return jnp.asarray(np.stack(ta + ti + ts, axis=0))


def _in_proj(x2, mod3, g_pre, w_in, seq):
    m_rows = x2.shape[0]
    tm = min(1024, seq)
    assert seq % tm == 0 and m_rows % tm == 0
    per_seq = seq // tm
    o_q, o_k, o_qi = POOL_WIDTH, POOL_WIDTH + ATTN_WIDTH, POOL_WIDTH + ATTN_WIDTH + 2 * KV_WIDTH
    o_ki = o_qi + IDX_HEADS * IDX_DIM
    o_gl = o_ki + IDX_DIM + IDX_HEADS
    assert all(o % IN_TN == 0 for o in (o_q, o_k, o_qi, o_ki)) and 2 * KV_WIDTH == IN_TN
    w_main = jnp.concatenate([w_in[:, :o_ki], w_in[:, o_gl:]], axis=1).astype(BF16)
    w_small = jnp.concatenate(
        [w_in[:, o_ki:o_gl], jnp.zeros((D_MODEL, LANES - IDX_DIM - IDX_HEADS), F32)], axis=1
    ).astype(BF16)
    tabs = _rope_tables(seq)
    n_j = PROJ_WIDTH // IN_TN
    n_gl = Q_OFF // IN_TN

    def src_tile(j):
        return jnp.where(j < n_gl, o_ki // IN_TN + j,
                         jnp.where(j < U_OFF // IN_TN, o_q // IN_TN + j - n_gl,
                                   jnp.where(j == U_OFF // IN_TN, 0,
                                             jnp.where(j == QI_OFF // IN_TN, o_qi // IN_TN, o_k // IN_TN))))

    return pl.pallas_call(
        _in_kernel,
        grid=(m_rows // tm, n_j),
        in_specs=[
            pl.BlockSpec((tm, D_MODEL), lambda i, j: (i, 0)),
            pl.BlockSpec((1, N_MOD, D_MODEL), lambda i, j: (i // per_seq, 0, 0)),
            pl.BlockSpec((1, D_MODEL), lambda i, j: (0, 0)),
            pl.BlockSpec((D_MODEL, IN_TN), lambda i, j: (0, src_tile(j))),
            pl.BlockSpec((D_MODEL, LANES), lambda i, j: (0, 0)),
            pl.BlockSpec((9, tm, LANES), lambda i, j: (0, i % per_seq, 0)),
        ],
        out_specs=[
            pl.BlockSpec((tm, IN_TN), lambda i, j: (i, j)),
            pl.BlockSpec((tm, LANES), lambda i, j: (i, 0)),
        ],
        out_shape=[
            jax.ShapeDtypeStruct((m_rows, PROJ_WIDTH), BF16),
            jax.ShapeDtypeStruct((m_rows, LANES), F32),
        ],
        scratch_shapes=[pltpu.VMEM((tm, D_MODEL), BF16), pltpu.VMEM((tm, IN_TN), F32)],
        compiler_params=_cparams(("arbitrary", "arbitrary")),
        name="in_proj",
    )(x2, mod3, g_pre.reshape(1, D_MODEL), w_main, w_small, tabs)


def _attn_kernel(q_ref, qi_ref, k_ref, v_ref, kis_ref, wis_ref, o_ref,
                 key_ref, plim_ref, qs_ref, m_ref, l_ref, acc_ref, *, tq, kc, seq, n_keep):
    i = pl.program_id(1)
    nk = ((i + 1) * tq + kc - 1) // kc
    nt = (((1,), (1,)), ((), ()))
    grp = N_HEADS // N_KV_HEADS
    row_io = lax.broadcasted_iota(I32, (kc, tq), 0)
    tpos = i * tq + lax.broadcasted_iota(I32, (kc, tq), 1)
    w_t = wis_ref[...].T

    def idx_chunk(c, carry):
        r0 = pl.multiple_of(c * kc, kc)
        ki = kis_ref[pl.ds(r0, kc), :IDX_DIM].astype(BF16)
        sc = jnp.zeros((kc, tq), F32)
        for h in range(IDX_HEADS):
            qh = qi_ref[:, h * IDX_DIM:(h + 1) * IDX_DIM]
            d = lax.dot_general(ki, qh, nt, preferred_element_type=F32)
            sc = sc + jnp.maximum(d, 0.0) * w_t[IDX_DIM + h:IDX_DIM + h + 1, :]
        bits = lax.bitcast_convert_type(sc, I32)
        key = jnp.where(bits < 0, bits ^ np.int32(0x7FFFFFFF), bits)
        key_ref[pl.ds(r0, kc), :] = jnp.where(r0 + row_io <= tpos, key, INT_MIN)
        return carry

    lax.fori_loop(0, nk, idx_chunk, 0)

    def count(pred):
        def body(c, acc):
            r0 = pl.multiple_of(c * kc, kc)
            hit = pred(key_ref[pl.ds(r0, kc), :], r0)
            return acc + jnp.sum(hit.astype(I32), axis=0, keepdims=True)

        return lax.fori_loop(0, nk, body, jnp.zeros((1, tq), I32))

    def bit_step(it, tu):
        cand = tu | lax.shift_left(np.int32(1), 31 - it)
        cnt = count(lambda kch, r0: kch >= (cand ^ INT_MIN))
        return jnp.where(cnt >= n_keep, cand, tu)

    tu = lax.fori_loop(0, 32, bit_step, jnp.zeros((1, tq), I32))
    thr = tu ^ INT_MIN
    cnt_gt = count(lambda kch, r0: kch > thr)
    cnt_ge = count(lambda kch, r0: kch >= thr)
    tie_i = ((cnt_ge > n_keep) & (tu != 0)).astype(I32)
    need = n_keep - cnt_gt

    plim_ref[...] = jnp.full(plim_ref.shape, seq, I32)

    @pl.when(jnp.max(tie_i) > 0)
    def _():
        n_bits = (seq - 1).bit_length()

        def idx_step(it, p):
            cand = p | lax.shift_left(np.int32(1), (n_bits - 1) - it)
            below = count(lambda kch, r0: (kch == thr) & (r0 + row_io < cand))
            return jnp.where(below < need, cand, p)

        p = lax.fori_loop(0, n_bits, idx_step, jnp.zeros((1, tq), I32))
        plim_ref[...] = jnp.broadcast_to(jnp.where(tie_i > 0, p, seq), plim_ref.shape)

    for g in range(N_KV_HEADS):
        for r in range(grp):
            hh = g * grp + r
            qs_ref[g, r * tq:(r + 1) * tq, :] = q_ref[:, hh * HEAD_DIM:(hh + 1) * HEAD_DIM]
    m_ref[...] = jnp.full(m_ref.shape, MASK_NEG, F32)
    l_ref[...] = jnp.zeros_like(l_ref)
    acc_ref[...] = jnp.zeros_like(acc_ref)
    c_exp = HEAD_DIM ** -0.5 * np.log2(np.e)

    def att_chunk(c, carry):
        r0 = pl.multiple_of(c * kc, kc)
        kch = key_ref[pl.ds(r0, kc), :]
        row = r0 + row_io
        sel = ((kch > thr) | ((kch == thr) & (row <= plim_ref[0:1, :]))) & (row <= tpos)
        bias_t = jnp.where(sel, 0.0, MASK_NEG).astype(F32).T
        for g in range(N_KV_HEADS):
            kg = k_ref[pl.ds(r0, kc), g * HEAD_DIM:(g + 1) * HEAD_DIM]
            vg = v_ref[pl.ds(r0, kc), g * HEAD_DIM:(g + 1) * HEAD_DIM]
            s = lax.dot_general(qs_ref[g], kg, nt, preferred_element_type=F32)
            s = (s.reshape(grp, tq, kc) + bias_t[None]).reshape(grp * tq, kc)
            m_old = m_ref[g]
            m_new = jnp.maximum(m_old, jnp.max(s, axis=-1, keepdims=True))
            alpha = jnp.exp2((m_old - m_new) * c_exp)
            p = jnp.exp2((s - m_new) * c_exp)
            l_ref[g] = alpha * l_ref[g] + jnp.sum(p, axis=-1, keepdims=True)
            acc_ref[g] = alpha * acc_ref[g] + jnp.dot(p.astype(BF16), vg, preferred_element_type=F32)
            m_ref[g] = m_new
        return carry

    lax.fori_loop(0, nk, att_chunk, 0)
    for g in range(N_KV_HEADS):
        o = acc_ref[g] / l_ref[g]
        for r in range(grp):
            hh = g * grp + r
            o_ref[:, hh * HEAD_DIM:(hh + 1) * HEAD_DIM] = o[r * tq:(r + 1) * tq, :].astype(BF16)


def _attn(proj, small, bsz, seq):
    tq = min(ATTN_TQ, seq)
    kc = min(ATTN_KC, seq)
    assert seq % tq == 0 and seq % kc == 0 and kc % tq == 0
    nq = seq // tq
    n_keep = min(TOPK_MAX, seq // 4)
    grp = N_HEADS // N_KV_HEADS
    kern = functools.partial(_attn_kernel, tq=tq, kc=kc, seq=seq, n_keep=n_keep)
    return pl.pallas_call(
        kern,
        grid=(bsz, nq),
        in_specs=[
            pl.BlockSpec((tq, ATTN_WIDTH), lambda b, i: (b * nq + i, Q_OFF // ATTN_WIDTH)),
            pl.BlockSpec((tq, IDX_HEADS * IDX_DIM), lambda b, i: (b * nq + i, QI_OFF // (IDX_HEADS * IDX_DIM))),
            pl.BlockSpec((seq, KV_WIDTH), lambda b, i: (b, K_OFF // KV_WIDTH)),
            pl.BlockSpec((seq, KV_WIDTH), lambda b, i: (b, V_OFF // KV_WIDTH)),
            pl.BlockSpec((seq, LANES), lambda b, i: (b, 0)),
            pl.BlockSpec((tq, LANES), lambda b, i: (b * nq + i, 0)),
        ],
        out_specs=pl.BlockSpec((tq, ATTN_WIDTH), lambda b, i: (b * nq + i, 0)),
        out_shape=jax.ShapeDtypeStruct((bsz * seq, ATTN_WIDTH), BF16),
        scratch_shapes=[
            pltpu.VMEM((seq, tq), I32),
            pltpu.VMEM((8, tq), I32),
            pltpu.VMEM((N_KV_HEADS, grp * tq, HEAD_DIM), BF16),
            pltpu.VMEM((N_KV_HEADS, grp * tq, 1), F32),
            pltpu.VMEM((N_KV_HEADS, grp * tq, 1), F32),
            pltpu.VMEM((N_KV_HEADS, grp * tq, HEAD_DIM), F32),
        ],
        compiler_params=_cparams(("arbitrary", "arbitrary")),
        name="attn",
    )(proj, proj, proj, proj, small, small)


def _merge_kernel(
    u_ref, halo_ref, ya_ref, gl_ref, x_ref, mod_ref, wgrp_ref, pscale_ref, wup_p_ref, wup_a_ref, wout_ref,
    gpost_ref, gffn_ref, wr_ref, br_ref,
    x1_ref, hp_ref, route_ref, cnt_ref,
    carry_ref, *, seq,
):
    tm = MERGE_TM
    i = pl.program_id(0)
    pos0 = (i * tm) % seq

    @pl.when(i == 0)
    def _():
        carry_ref[...] = jnp.zeros_like(carry_ref)

    halo_rows = halo_ref.shape[0]
    halo = jnp.where(pos0 == 0, 0.0, halo_ref[...].astype(F32))
    ext = jnp.concatenate([halo, u_ref[...].astype(F32)], axis=0)
    pos = pos0 + lax.broadcasted_iota(I32, (tm, 1), 0)
    ys = []
    for g, win in enumerate(POOL_WINDOWS):
        sl = slice(g * POOL_GROUP_DIM, (g + 1) * POOL_GROUP_DIM)
        e = ext[:, sl]
        acc = e
        span = 1
        while span < win:
            acc = acc + pltpu.roll(acc, span, 0)
            span *= 2
        cnt = jnp.minimum(pos + 1, win).astype(F32)
        mixed = (acc[halo_rows:] / cnt - e[halo_rows:]).astype(BF16)
        ys.append(jnp.dot(mixed, wgrp_ref[g], preferred_element_type=F32))
    y_pool = (jnp.concatenate(ys, axis=1) * pscale_ref[...]).astype(BF16)

    gates = jax.nn.sigmoid(gl_ref[...].astype(F32))
    up_p = jnp.dot(y_pool, wup_p_ref[...], preferred_element_type=F32)
    up_a = jnp.dot(ya_ref[...], wup_a_ref[...], preferred_element_type=F32)
    merged = (gates[:, :D_MODEL] * up_p + gates[:, D_MODEL:] * up_a).astype(BF16)
    y = jnp.dot(merged, wout_ref[...], preferred_element_type=F32)
    m = mod_ref[0]
    x1 = x_ref[...] + m[2:3] * _rms(y, gpost_ref[...])
    x1_ref[...] = x1

    h2f = _rms(x1, gffn_ref[...]) * (1.0 + m[4:5]) + m[3:4]
    h2 = h2f.astype(BF16)
    hp_ref[...] = _pack_bf16_pairs(h2f)

    lane = lax.broadcasted_iota(I32, (tm, LANES), 1)
    logits = jnp.dot(h2, wr_ref[...], preferred_element_type=F32) + br_ref[...]
    work = jnp.where(lane < N_EXPERTS, logits, -jnp.inf)
    vals, hots = [], []
    for _ in range(TOP_K):
        mx = jnp.max(work, axis=-1, keepdims=True)
        first = jnp.min(jnp.where(work == mx, lane, LANES), axis=-1, keepdims=True)
        hot = lane == first
        vals.append(mx)
        hots.append(hot)
        work = jnp.where(hot, -jnp.inf, work)
    ex = [jnp.exp(v - vals[0]) for v in vals]
    den = ex[0] + ex[1] + ex[2] + ex[3]

    onehot = jnp.zeros((tm, LANES), F32)
    for hot in hots:
        onehot = onehot + hot.astype(F32)
    r_i = lax.broadcasted_iota(I32, (tm, tm), 0)
    c_i = lax.broadcasted_iota(I32, (tm, tm), 1)
    tril = (c_i < r_i).astype(BF16)
    before = jnp.dot(tril, onehot.astype(BF16), preferred_element_type=F32) + carry_ref[0:1, :]
    out = jnp.zeros((tm, LANES), F32)
    lane_f = lane.astype(F32)
    for k in range(TOP_K):
        out = jnp.where(lane == k, ex[k] / den, out)
        idx_k = jnp.sum(jnp.where(hots[k], lane_f, 0.0), axis=-1, keepdims=True)
        out = jnp.where(lane == TOP_K + k, idx_k, out)
        rank_k = jnp.sum(jnp.where(hots[k], before, 0.0), axis=-1, keepdims=True)
        out = jnp.where(lane == 2 * TOP_K + k, rank_k, out)
    route_ref[...] = out
    total = carry_ref[0:1, :] + jnp.sum(onehot, axis=0, keepdims=True)
    carry_ref[...] = jnp.broadcast_to(total, carry_ref.shape)
    cnt_ref[...] = jnp.broadcast_to(total, cnt_ref.shape)


def _merge(proj, y_attn, x2, mod3, w_pool_grp, pool_scale, w_up_pool, w_up_attn, w_out, g_post_mix, g_pre_ffn,
           w_router, b_router, seq):
    m_rows = x2.shape[0]
    tm = MERGE_TM
    halo = 16
    assert seq % tm == 0 and max(POOL_WINDOWS) <= halo
    per_seq = seq // tm
    wr = jnp.concatenate([w_router, jnp.zeros((D_MODEL, LANES - N_EXPERTS), F32)], axis=1).astype(BF16)
    br = jnp.concatenate([b_router, jnp.zeros((LANES - N_EXPERTS,), F32)]).reshape(1, LANES)
    const = lambda shape: pl.BlockSpec(shape, lambda i: (0,) * len(shape), pipeline_mode=pl.Buffered(1))
    kern = functools.partial(_merge_kernel, seq=seq)
    return pl.pallas_call(
        kern,
        grid=(m_rows // tm,),
        in_specs=[
            pl.BlockSpec((tm, POOL_WIDTH), lambda i: (i, U_OFF // POOL_WIDTH)),
            pl.BlockSpec((halo, POOL_WIDTH), lambda i: (jnp.maximum(i * (tm // halo) - 1, 0), U_OFF // POOL_WIDTH)),
            pl.BlockSpec((tm, ATTN_WIDTH), lambda i: (i, 0)),
            pl.BlockSpec((tm, N_BRANCHES * D_MODEL), lambda i: (i, 0)),
            pl.BlockSpec((tm, D_MODEL), lambda i: (i, 0)),
            pl.BlockSpec((1, N_MOD, D_MODEL), lambda i: (i // per_seq, 0, 0)),
            const((POOL_GROUPS, POOL_GROUP_DIM, POOL_GROUP_DIM)),
            const((1, POOL_WIDTH)),
            const((POOL_WIDTH, D_MODEL)),
            const((ATTN_WIDTH, D_MODEL)),
            const((D_MODEL, D_MODEL)),
            const((1, D_MODEL)),
            const((1, D_MODEL)),
            const((D_MODEL, LANES)),
            const((1, LANES)),
        ],
        out_specs=[
            pl.BlockSpec((tm, D_MODEL), lambda i: (i, 0)),
            pl.BlockSpec((tm, D_MODEL // 2), lambda i: (i, 0)),
            pl.BlockSpec((tm, LANES), lambda i: (i, 0)),
            pl.BlockSpec((8, LANES), lambda i: (0, 0)),
        ],
        out_shape=[
            jax.ShapeDtypeStruct((m_rows, D_MODEL), F32),
            jax.ShapeDtypeStruct((m_rows, D_MODEL // 2), U32),
            jax.ShapeDtypeStruct((m_rows, LANES), F32),
            jax.ShapeDtypeStruct((8, LANES), F32),
        ],
        scratch_shapes=[pltpu.VMEM((8, LANES), F32)],
        compiler_params=_cparams(("arbitrary",)),
        name="merge",
    )(
        proj, proj, y_attn, proj, x2, mod3,
        w_pool_grp.astype(BF16), pool_scale.reshape(1, POOL_WIDTH), w_up_pool.astype(BF16),
        w_up_attn.astype(BF16), w_out.astype(BF16), g_post_mix.reshape(1, D_MODEL), g_pre_ffn.reshape(1, D_MODEL),
        wr, br,
    )


def _dispatch_kernel(zf_ref, pos_ref, h_ref, xs_ref, zbuf, sem, zsem):
    tm = h_ref.shape[0]
    bm = zbuf.shape[0]

    @pl.when(pl.program_id(0) == 0)
    def _():
        zbuf[...] = jnp.zeros_like(zbuf)

        def zero_copy(b):
            return pltpu.make_async_copy(zbuf, xs_ref.at[pl.ds(b * bm, bm), :], zsem)

        def z_issue(b, carry):
            pl.when(zf_ref[b] != 0)(lambda: zero_copy(b).start())
            return carry

        def z_drain(b, carry):
            pl.when(zf_ref[b] != 0)(lambda: zero_copy(b).wait())
            return carry

        lax.fori_loop(0, zf_ref.shape[0], z_issue, 0)
        lax.fori_loop(0, zf_ref.shape[0], z_drain, 0)

    def row_copy(t, k):
        return pltpu.make_async_copy(
            h_ref.at[pl.ds(t, 1), :], xs_ref.at[pl.ds(pos_ref[k * tm + t], 1), :], sem
        )

    def issue(t, carry):
        for k in range(TOP_K):
            row_copy(t, k).start(priority=k % 2)
        return carry

    lax.fori_loop(0, tm, issue, 0, unroll=ROW_DMA_UNROLL)
    for _ in range(TOP_K):
        pltpu.make_async_copy(h_ref, xs_ref.at[pl.ds(0, tm), :], sem).wait()


def _tile_major(pos_t, tm):
    return pos_t.reshape(TOP_K, pos_t.shape[1] // tm, tm).transpose(1, 0, 2).reshape(-1)


def _dispatch(zero_flag, pos_t, hp, n_rows):
    m_rows, width = hp.shape
    tm = min(DISPATCH_TM, m_rows)
    assert m_rows % tm == 0
    pos_flat = _tile_major(pos_t, tm)
    grid_spec = pltpu.PrefetchScalarGridSpec(
        num_scalar_prefetch=1,
        grid=(m_rows // tm,),
        in_specs=[
            pl.BlockSpec((tm * TOP_K,), lambda i, zf: (i,), memory_space=pltpu.SMEM),
            pl.BlockSpec((tm, width), lambda i, zf: (i, 0)),
        ],
        out_specs=pl.BlockSpec(memory_space=pl.ANY),
        scratch_shapes=[pltpu.VMEM((MOE_BM, width), U32), pltpu.SemaphoreType.DMA(()), pltpu.SemaphoreType.DMA(())],
    )
    return pl.pallas_call(
        _dispatch_kernel,
        grid_spec=grid_spec,
        out_shape=jax.ShapeDtypeStruct((n_rows, width), U32),
        compiler_params=_cparams(("arbitrary",)),
        name="dispatch",
    )(zero_flag, pos_flat, hp)


def _expert_rows(e, j, bs_ref, nb_ref, nu_ref, x_hbm, o_hbm, xbuf, obuf, xsem, osem, compute, prepare):
    n_x, bm = xbuf.shape[0], xbuf.shape[1]
    tw = obuf.shape[2]
    n_blocks = x_hbm.shape[0] // bm
    b0 = bs_ref[e]
    nb = nb_ref[e]
    n_used = nu_ref[0]

    def x_copy(blk, slot):
        return pltpu.make_async_copy(x_hbm.at[pl.ds(blk * bm, bm), :], xbuf.at[slot], xsem.at[slot])

    def o_copy(blk, slot):
        return pltpu.make_async_copy(obuf.at[slot], o_hbm.at[pl.ds(blk * bm, bm), pl.ds(j * tw, tw)], osem.at[slot])

    @pl.when(nb > 0)
    def _():
        prepare()

        def body(rb, carry):
            g = b0 + rb
            slot = g % 2
            ahead = n_x - 1

            @pl.when(g == 0)
            def _():
                for a in range(ahead):
                    pl.when(a < n_used)(lambda a=a: x_copy(a, a).start())

            x_copy(g, g % n_x).wait()

            @pl.when(g + ahead < n_used)
            def _():
                x_copy(g + ahead, (g + ahead) % n_x).start()

            @pl.when(g >= 2)
            def _():
                o_copy(g - 2, slot).wait()

            obuf[slot] = compute(xbuf[g % n_x])
            o_copy(g, slot).start()

            @pl.when(g == n_used - 1)
            def _():
                @pl.when(g >= 1)
                def _():
                    o_copy(g - 1, 1 - slot).wait()

                o_copy(g, slot).wait()

            return carry

        lax.fori_loop(0, nb, body, 0)

    @pl.when(e == N_EXPERTS - 1)
    def _():
        obuf[0] = jnp.zeros(obuf.shape[1:], obuf.dtype)

        def z_issue(blk, carry):
            o_copy(blk, 0).start()
            return carry

        def z_drain(blk, carry):
            o_copy(blk, 0).wait()
            return carry

        lax.fori_loop(nu_ref[0], n_blocks, z_issue, 0)
        lax.fori_loop(nu_ref[0], n_blocks, z_drain, 0)


def _ffn1_kernel(bs_ref, nb_ref, nu_ref, x_hbm, wg_ref, wl_ref, bg_ref, bl_ref, o_hbm,
                 wg_s, wl_s, xbuf, obuf, xsem, osem):
    j, e = pl.program_id(0), pl.program_id(1)
    half = D_MODEL // 2

    def prepare():
        wg_s[...] = wg_ref[0].astype(BF16)
        wl_s[...] = wl_ref[0].astype(BF16)

    def compute(xp):
        lo, hi = (v.astype(BF16) for v in _unpack_bf16_pairs(xp))

        def proj(w_s, b_ref):
            return (
                jnp.dot(lo, w_s[:half, :], preferred_element_type=F32)
                + jnp.dot(hi, w_s[half:, :], preferred_element_type=F32)
                + b_ref[0]
            )

        glu = jnp.minimum(proj(wg_s, bg_ref), SWIGLU_LIMIT)
        lin = jnp.clip(proj(wl_s, bl_ref), -SWIGLU_LIMIT, SWIGLU_LIMIT)
        return (glu * jax.nn.sigmoid(SWIGLU_ALPHA * glu) * (lin + 1.0)).astype(BF16)

    _expert_rows(e, j, bs_ref, nb_ref, nu_ref, x_hbm, o_hbm, xbuf, obuf, xsem, osem, compute, prepare)


def _ffn1(blk_start, blk_count, n_used, xs, w1, b1):
    n_rows = xs.shape[0]
    bm, tn = MOE_BM, FFN1_TN
    nj = D_FF // tn
    b1r = b1.reshape(N_EXPERTS, 1, 2 * D_FF)
    grid_spec = pltpu.PrefetchScalarGridSpec(
        num_scalar_prefetch=3,
        grid=(nj, N_EXPERTS),
        in_specs=[
            pl.BlockSpec(memory_space=pl.ANY),
            pl.BlockSpec((1, D_MODEL, tn), lambda j, e, *_: (e, 0, j)),
            pl.BlockSpec((1, D_MODEL, tn), lambda j, e, *_: (e, 0, nj + j)),
            pl.BlockSpec((1, 1, tn), lambda j, e, *_: (e, 0, j)),
            pl.BlockSpec((1, 1, tn), lambda j, e, *_: (e, 0, nj + j)),
        ],
        out_specs=pl.BlockSpec(memory_space=pl.ANY),
        scratch_shapes=[
            pltpu.VMEM((D_MODEL, tn), BF16),
            pltpu.VMEM((D_MODEL, tn), BF16),
            pltpu.VMEM((FFN_X_SLOTS, bm, D_MODEL // 2), U32),
            pltpu.VMEM((2, bm, tn), BF16),
            pltpu.SemaphoreType.DMA((FFN_X_SLOTS,)),
            pltpu.SemaphoreType.DMA((2,)),
        ],
    )
    return pl.pallas_call(
        _ffn1_kernel,
        grid_spec=grid_spec,
        out_shape=jax.ShapeDtypeStruct((n_rows, D_FF), BF16),
        compiler_params=_cparams(("arbitrary", "arbitrary")),
        name="ffn1",
    )(blk_start, blk_count, n_used, xs, w1, w1, b1r, b1r)


def _ffn2_kernel(bs_ref, nb_ref, nu_ref, a_hbm, w_ref, b_ref, o_hbm, w_s, xbuf, obuf, xsem, osem):
    j, e = pl.program_id(0), pl.program_id(1)

    def prepare():
        w_s[...] = w_ref[0].astype(BF16)

    def compute(a):
        return _pack_bf16_pairs(jnp.dot(a, w_s[...], preferred_element_type=F32) + b_ref[0])

    _expert_rows(e, j, bs_ref, nb_ref, nu_ref, a_hbm, o_hbm, xbuf, obuf, xsem, osem, compute, prepare)


def _ffn2(blk_start, blk_count, n_used, act, w2, b2):
    n_rows = act.shape[0]
    bm, tn = MOE_BM, FFN2_TN
    nj = D_MODEL // tn
    b2r = b2.reshape(N_EXPERTS, 1, D_MODEL)
    grid_spec = pltpu.PrefetchScalarGridSpec(
        num_scalar_prefetch=3,
        grid=(nj, N_EXPERTS),
        in_specs=[
            pl.BlockSpec(memory_space=pl.ANY),
            pl.BlockSpec((1, D_FF, tn), lambda j, e, *_: (e, 0, j)),
            pl.BlockSpec((1, 1, tn), lambda j, e, *_: (e, 0, j)),
        ],
        out_specs=pl.BlockSpec(memory_space=pl.ANY),
        scratch_shapes=[
            pltpu.VMEM((D_FF, tn), BF16),
            pltpu.VMEM((FFN_X_SLOTS, bm, D_FF), BF16),
            pltpu.VMEM((2, bm, tn // 2), U32),
            pltpu.SemaphoreType.DMA((FFN_X_SLOTS,)),
            pltpu.SemaphoreType.DMA((2,)),
        ],
    )
    return pl.pallas_call(
        _ffn2_kernel,
        grid_spec=grid_spec,
        out_shape=jax.ShapeDtypeStruct((n_rows, D_MODEL // 2), U32),
        compiler_params=_cparams(("arbitrary", "arbitrary")),
        name="ffn2",
    )(blk_start, blk_count, n_used, act, w2, b2r)


def _combine_kernel(pos_ref, ys_ref, route_ref, x1_ref, mod_ref, g_ref, o_ref, buf, sem):
    tm = x1_ref.shape[0]

    def row_copy(t, k):
        return pltpu.make_async_copy(
            ys_ref.at[pl.ds(pos_ref[k * tm + t], 1), :], buf.at[k, pl.ds(t, 1), :], sem
        )

    def issue(t, carry):
        for k in range(TOP_K):
            row_copy(t, k).start(priority=k % 2)
        return carry

    lax.fori_loop(0, tm, issue, 0, unroll=ROW_DMA_UNROLL)
    for k in range(TOP_K):
        pltpu.make_async_copy(ys_ref.at[pl.ds(0, tm), :], buf.at[k], sem).wait()
    route = route_ref[...]
    y_lo, y_hi = (route[:, 0:1] * v for v in _unpack_bf16_pairs(buf[0]))
    for k in range(1, TOP_K):
        lo, hi = _unpack_bf16_pairs(buf[k])
        y_lo = y_lo + route[:, k:k + 1] * lo
        y_hi = y_hi + route[:, k:k + 1] * hi
    hw = FFN2_TN // 2
    parts = []
    for t in range(D_MODEL // FFN2_TN):
        parts += [y_lo[:, t * hw:(t + 1) * hw], y_hi[:, t * hw:(t + 1) * hw]]
    y = jnp.concatenate(parts, axis=1)
    m = mod_ref[0]
    o_ref[...] = x1_ref[...] + m[5:6] * _rms(y, g_ref[...])


def _combine(pos_t, ys, route, x1, mod3, g_post_ffn, seq):
    m_rows = x1.shape[0]
    tm = min(COMBINE_TM, seq)
    assert seq % tm == 0
    pos_flat = _tile_major(pos_t, tm)
    per_seq = seq // tm
    return pl.pallas_call(
        _combine_kernel,
        grid=(m_rows // tm,),
        in_specs=[
            pl.BlockSpec((tm * TOP_K,), lambda i: (i,), memory_space=pltpu.SMEM),
            pl.BlockSpec(memory_space=pl.ANY),
            pl.BlockSpec((tm, LANES), lambda i: (i, 0)),
            pl.BlockSpec((tm, D_MODEL), lambda i: (i, 0)),
            pl.BlockSpec((1, N_MOD, D_MODEL), lambda i: (i // per_seq, 0, 0)),
            pl.BlockSpec((1, D_MODEL), lambda i: (0, 0)),
        ],
        out_specs=pl.BlockSpec((tm, D_MODEL), lambda i: (i, 0)),
        out_shape=jax.ShapeDtypeStruct((m_rows, D_MODEL), F32),
        scratch_shapes=[pltpu.VMEM((TOP_K, tm, D_MODEL // 2), U32), pltpu.SemaphoreType.DMA(())],
        compiler_params=_cparams(("arbitrary",)),
        name="combine",
    )(pos_flat, ys, route, x1, mod3, g_post_ffn.reshape(1, D_MODEL))


def _layer(x2, mod3, bsz, seq, g_pre_mix, g_post_mix, w_in, w_pool_grp, pool_scale, w_up_pool, w_up_attn, w_out,
           g_pre_ffn, g_post_ffn, w_router, b_router, w1, b1, w2, b2):
    m_rows = x2.shape[0]
    proj, small = _in_proj(x2, mod3, g_pre_mix, w_in, seq)
    y_attn = _attn(proj, small, bsz, seq)
    x1, hp, route, cnt = _merge(proj, y_attn, x2, mod3, w_pool_grp, pool_scale, w_up_pool, w_up_attn, w_out,
                                g_post_mix, g_pre_ffn, w_router, b_router, seq)

    bm = MOE_BM
    n_slots = m_rows * TOP_K
    n_blocks = -(-n_slots // bm) + N_EXPERTS
    idx_t = route[:, TOP_K:2 * TOP_K].T.astype(I32)
    rank_t = route[:, 2 * TOP_K:3 * TOP_K].T.astype(I32)
    counts = cnt[0, :N_EXPERTS].astype(I32)
    padded = (counts + bm - 1) // bm * bm
    pad_end = jnp.cumsum(padded)
    pad_start = pad_end - padded
    experts = jnp.arange(N_EXPERTS, dtype=I32)[:, None, None]
    pos_t = rank_t + jnp.sum(jnp.where(idx_t[None] == experts, pad_start[:, None, None], 0), axis=0)
    n_used = (pad_end[-1] // bm).astype(I32).reshape(1)
    blk_ids = jnp.arange(n_blocks, dtype=I32)
    blk_start = (pad_start // bm).astype(I32)
    blk_count = (padded // bm).astype(I32)
    last_of_expert = jnp.any((blk_ids[:, None] + 1) * bm == pad_end[None, :], axis=1)
    zero_flag = (last_of_expert | (blk_ids >= n_used[0])).astype(I32)

    xs = _dispatch(zero_flag, pos_t, hp, n_blocks * bm)
    act = _ffn1(blk_start, blk_count, n_used, xs, w1, b1)
    ys = _ffn2(blk_start, blk_count, n_used, act, w2, b2)
    return _combine(pos_t, ys, route, x1, mod3, g_post_ffn, seq)


def kernel(x, c, w_ada, b_ada, g_pre_mix, g_post_mix, w_in, w_pool_grp, pool_scale, w_up_pool, w_up_attn, w_out,
           g_pre_ffn, g_post_ffn, w_router, b_router, w1, b1, w2, b2):
    bsz, seq, d = x.shape
    assert d == D_MODEL
    depth = w_ada.shape[0]
    x2 = x.reshape(bsz * seq, d)
    for layer in range(depth):
        mod3 = _ada(c, w_ada[layer], b_ada[layer]).reshape(bsz, N_MOD, D_MODEL)
        x2 = _layer(x2, mod3, bsz, seq, g_pre_mix[layer], g_post_mix[layer], w_in[layer], w_pool_grp[layer],
                    pool_scale[layer], w_up_pool[layer], w_up_attn[layer], w_out[layer], g_pre_ffn[layer],
                    g_post_ffn[layer], w_router[layer], b_router[layer], w1[layer], b1[layer], w2[layer], b2[layer])
    return x2.reshape(bsz, seq, d)
```

```python
import functools

import jax
import jax.numpy as jnp
import numpy as np
from jax import lax
from jax.experimental import pallas as pl
from jax.experimental.pallas import tpu as pltpu

F32 = jnp.float32
BF16 = jnp.bfloat16
I32 = jnp.int32
U32 = jnp.uint32

D_MODEL = 2048
POOL_WINDOWS = (2, 4, 8, 16)
POOL_GROUPS = 4
POOL_GROUP_DIM = D_MODEL // 8
POOL_WIDTH = POOL_GROUPS * POOL_GROUP_DIM
N_HEADS = 16
N_KV_HEADS = 4
HEAD_DIM = 128
ATTN_WIDTH = N_HEADS * HEAD_DIM
KV_WIDTH = N_KV_HEADS * HEAD_DIM
ROT_DIM = HEAD_DIM // 4
IDX_HEADS = 16
IDX_DIM = 64
IDX_ROT_DIM = IDX_DIM // 4
TOPK_MAX = 256
ROPE_THETA = 500000.0
N_BRANCHES = 2
N_EXPERTS = 32
TOP_K = 4
D_FF = D_MODEL
SWIGLU_ALPHA = 1.702
SWIGLU_LIMIT = 7.0
N_MOD = 6
EPS = 1e-6

LANES = 128
INT_MIN = np.int32(-(2 ** 31))

GL_OFF = 0
Q_OFF = GL_OFF + N_BRANCHES * D_MODEL
U_OFF = Q_OFF + ATTN_WIDTH
QI_OFF = U_OFF + POOL_WIDTH
K_OFF = QI_OFF + IDX_HEADS * IDX_DIM
V_OFF = K_OFF + KV_WIDTH
PROJ_WIDTH = V_OFF + KV_WIDTH

IN_TN = 1024
ATTN_TQ = 512
ATTN_KC = 512
MASK_NEG = -1e30
MERGE_TM = 256
MOE_BM = 512
FFN1_TN = 512
FFN2_TN = 1024
DISPATCH_TM = 1024
COMBINE_TM = 512
ROW_DMA_UNROLL = 8
FFN_X_SLOTS = 3
FFN_O_SLOTS = 3
VMEM_LIMIT = 56 * 1024 * 1024


def _cparams(sem):
    return pltpu.CompilerParams(dimension_semantics=sem, vmem_limit_bytes=VMEM_LIMIT)


def _rms(x, g):
    return x * lax.rsqrt(jnp.mean(x * x, axis=-1, keepdims=True) + EPS) * g


def _pack_bf16_pairs(x):
    n = x.shape[1] // 2
    bits = lax.bitcast_convert_type(x.astype(BF16).astype(F32), U32)
    return bits[:, n:] | lax.shift_right_logical(bits[:, :n], np.uint32(16))


def _unpack_bf16_pairs(w):
    lo = lax.bitcast_convert_type(lax.shift_left(w, np.uint32(16)), F32)
    hi = lax.bitcast_convert_type(w & np.uint32(0xFFFF0000), F32)
    return lo, hi


def _ada_kernel(c_ref, w_ref, b_ref, o_ref):
    c = c_ref[...]
    ca = (c * jax.nn.sigmoid(c)).astype(BF16)
    o_ref[...] = jnp.dot(ca, w_ref[...].astype(BF16), preferred_element_type=F32) + b_ref[...]


def _ada(c, w_ada, b_ada):
    bsz = c.shape[0]
    n = w_ada.shape[1]
    tn = 1024
    return pl.pallas_call(
        _ada_kernel,
        grid=(n // tn,),
        in_specs=[
            pl.BlockSpec((bsz, D_MODEL), lambda j: (0, 0)),
            pl.BlockSpec((D_MODEL, tn), lambda j: (0, j)),
            pl.BlockSpec((1, tn), lambda j: (0, j)),
        ],
        out_specs=pl.BlockSpec((bsz, tn), lambda j: (0, j)),
        out_shape=jax.ShapeDtypeStruct((bsz, n), F32),
        compiler_params=_cparams(("arbitrary",)),
        name="ada",
    )(c, w_ada, b_ada.reshape(1, n))


def _rope_slice(xs, c, s1, s2, shift):
    return xs * c + pltpu.roll(xs, LANES - shift, 1) * s1 + pltpu.roll(xs, shift, 1) * s2


def _in_kernel(x_ref, mod_ref, g_ref, w_ref, ws_ref, tab_ref, o_ref, os_ref, h_ref, acc_ref):
    j = pl.program_id(1)
    half_a = ROT_DIM // 2
    half_i = IDX_ROT_DIM // 2

    @pl.when(j == 0)
    def _():
        m = mod_ref[0]
        h = _rms(x_ref[...], g_ref[...]) * (1.0 + m[1:2]) + m[0:1]
        hb = h.astype(BF16)
        h_ref[...] = hb
        small = jnp.dot(hb, ws_ref[...], preferred_element_type=F32)
        os_ref[...] = _rope_slice(small, tab_ref[6], tab_ref[7], tab_ref[8], half_i)

    acc_ref[...] = jnp.dot(h_ref[...], w_ref[...], preferred_element_type=F32)
    n_sl = IN_TN // LANES
    j_q0, j_q1 = Q_OFF // IN_TN, U_OFF // IN_TN
    j_qi = QI_OFF // IN_TN
    j_kv = K_OFF // IN_TN
    k_sl = KV_WIDTH // LANES

    @pl.when((j < j_q0) | (j == U_OFF // IN_TN))
    def _():
        o_ref[...] = acc_ref[...].astype(BF16)

    @pl.when((j >= j_q0) & (j < j_q1))
    def _():
        for s in range(n_sl):
            sl = slice(s * LANES, (s + 1) * LANES)
            o_ref[:, sl] = _rope_slice(acc_ref[:, sl], tab_ref[0], tab_ref[1], tab_ref[2], half_a).astype(BF16)

    @pl.when(j == j_qi)
    def _():
        for s in range(n_sl):
            sl = slice(s * LANES, (s + 1) * LANES)
            o_ref[:, sl] = _rope_slice(acc_ref[:, sl], tab_ref[3], tab_ref[4], tab_ref[5], half_i).astype(BF16)

    @pl.when(j == j_kv)
    def _():
        for s in range(k_sl):
            sl = slice(s * LANES, (s + 1) * LANES)
            o_ref[:, sl] = _rope_slice(acc_ref[:, sl], tab_ref[0], tab_ref[1], tab_ref[2], half_a).astype(BF16)
        o_ref[:, KV_WIDTH:] = acc_ref[:, KV_WIDTH:].astype(BF16)


def _rope_tables(seq):
    def tabs(rot_dim, period, n_rep):
        half = rot_dim // 2
        inv = np.float32(ROPE_THETA) ** (-np.arange(0, rot_dim, 2, dtype=np.float32) / np.float32(rot_dim))
        ang = np.arange(seq, dtype=np.float32)[:, None] * inv.astype(np.float32)[None, :]
        cos, sin = np.cos(ang).astype(np.float32), np.sin(ang).astype(np.float32)
        ones = np.ones((seq, period - 2 * half), np.float32)
        z_h = np.zeros((seq, half), np.float32)
        c = np.concatenate([cos, cos, ones], axis=1)
        s1 = np.concatenate([-sin, z_h, 0 * ones], axis=1)
        s2 = np.concatenate([z_h, sin, 0 * ones], axis=1)
        return [np.tile(t, (1, n_rep)) for t in (c, s1, s2)]

    ta = tabs(ROT_DIM, HEAD_DIM, LANES // HEAD_DIM)
    ti = tabs(IDX_ROT_DIM, IDX_DIM, LANES // IDX_DIM)
    ts = tabs(IDX_ROT_DIM, IDX_DIM, 1)
    wi_scale = np.float32(IDX_HEADS ** -0.5 * IDX_DIM ** -0.5)
    pad = LANES - IDX_DIM
    c_tail = np.where(np.arange(pad) < IDX_HEADS, wi_scale, np.float32(0.0)).astype(np.float32)
    ts = [
        np.concatenate([ts[0], np.broadcast_to(c_tail, (seq, pad))], axis=1),
        np.concatenate([ts[1], np.zeros((seq, pad), np.float32)], axis=1),
        np.concatenate([ts[2], np.zeros((seq, pad), np.float32)], axis=1),
    ]
    return jnp.asarray(np.stack(ta + ti + ts, axis=0))


def _in_proj(x2, mod3, g_pre, w_in, seq):
    m_rows = x2.shape[0]
    tm = min(1024, seq)
    assert seq % tm == 0 and m_rows % tm == 0
    per_seq = seq // tm
    offs = np.cumsum((POOL_WIDTH, ATTN_WIDTH, KV_WIDTH, KV_WIDTH, IDX_HEADS * IDX_DIM, IDX_DIM, IDX_HEADS))
    w_u, w_q, w_k, w_v, w_qi, w_ki, w_wi, w_gl = jnp.split(w_in, [int(o) for o in offs], axis=1)
    w_main = jnp.concatenate([w_gl, w_q, w_u, w_qi, w_k, w_v], axis=1).astype(BF16)
    w_small = jnp.concatenate(
        [w_ki, w_wi, jnp.zeros((D_MODEL, LANES - IDX_DIM - IDX_HEADS), F32)], axis=1
    ).astype(BF16)
    tabs = _rope_tables(seq)
    n_j = PROJ_WIDTH // IN_TN
    return pl.pallas_call(
        _in_kernel,
        grid=(m_rows // tm, n_j),
        in_specs=[
            pl.BlockSpec((tm, D_MODEL), lambda i, j: (i, 0)),
            pl.BlockSpec((1, N_MOD, D_MODEL), lambda i, j: (i // per_seq, 0, 0)),
            pl.BlockSpec((1, D_MODEL), lambda i, j: (0, 0)),
            pl.BlockSpec((D_MODEL, IN_TN), lambda i, j: (0, j)),
            pl.BlockSpec((D_MODEL, LANES), lambda i, j: (0, 0)),
            pl.BlockSpec((9, tm, LANES), lambda i, j: (0, i % per_seq, 0)),
        ],
        out_specs=[
            pl.BlockSpec((tm, IN_TN), lambda i, j: (i, j)),
            pl.BlockSpec((tm, LANES), lambda i, j: (i, 0)),
        ],
        out_shape=[
            jax.ShapeDtypeStruct((m_rows, PROJ_WIDTH), BF16),
            jax.ShapeDtypeStruct((m_rows, LANES), F32),
        ],
        scratch_shapes=[pltpu.VMEM((tm, D_MODEL), BF16), pltpu.VMEM((tm, IN_TN), F32)],
        compiler_params=_cparams(("arbitrary", "arbitrary")),
        name="in_proj",
    )(x2, mod3, g_pre.reshape(1, D_MODEL), w_main, w_small, tabs)


def _attn_kernel(q_ref, qi_ref, k_ref, v_ref, kis_ref, wis_ref, o_ref,
                 key_ref, plim_ref, qs_ref, m_ref, l_ref, acc_ref, *, tq, kc, seq, n_keep):
    i = pl.program_id(1)
    nk = ((i + 1) * tq + kc - 1) // kc
    nt = (((1,), (1,)), ((), ()))
    grp = N_HEADS // N_KV_HEADS
    row_io = lax.broadcasted_iota(I32, (kc, tq), 0)
    tpos = i * tq + lax.broadcasted_iota(I32, (kc, tq), 1)
    w_t = wis_ref[...].T

    def idx_chunk(c, carry):
        r0 = pl.multiple_of(c * kc, kc)
        ki = kis_ref[pl.ds(r0, kc), :IDX_DIM].astype(BF16)
        sc = jnp.zeros((kc, tq), F32)
        for h in range(IDX_HEADS):
            qh = qi_ref[:, h * IDX_DIM:(h + 1) * IDX_DIM]
            d = lax.dot_general(ki, qh, nt, preferred_element_type=F32)
            sc = sc + jnp.maximum(d, 0.0) * w_t[IDX_DIM + h:IDX_DIM + h + 1, :]
        bits = lax.bitcast_convert_type(sc, I32)
        key = jnp.where(bits < 0, bits ^ np.int32(0x7FFFFFFF), bits)
        key_ref[pl.ds(r0, kc), :] = jnp.where(r0 + row_io <= tpos, key, INT_MIN)
        return carry

    lax.fori_loop(0, nk, idx_chunk, 0)

    def count(pred):
        def body(c, acc):
            r0 = pl.multiple_of(c * kc, kc)
            hit = pred(key_ref[pl.ds(r0, kc), :], r0)
            return acc + jnp.sum(hit.astype(I32), axis=0, keepdims=True)

        return lax.fori_loop(0, nk, body, jnp.zeros((1, tq), I32))

    def bit_step(it, tu):
        cand = tu | lax.shift_left(np.int32(1), 31 - it)
        cnt = count(lambda kch, r0: kch >= (cand ^ INT_MIN))
        return jnp.where(cnt >= n_keep, cand, tu)

    tu = lax.fori_loop(0, 32, bit_step, jnp.zeros((1, tq), I32))
    thr = tu ^ INT_MIN
    cnt_gt = count(lambda kch, r0: kch > thr)
    cnt_ge = count(lambda kch, r0: kch >= thr)
    tie_i = ((cnt_ge > n_keep) & (tu != 0)).astype(I32)
    need = n_keep - cnt_gt

    plim_ref[...] = jnp.full(plim_ref.shape, seq, I32)

    @pl.when(jnp.max(tie_i) > 0)
    def _():
        n_bits = (seq - 1).bit_length()

        def idx_step(it, p):
            cand = p | lax.shift_left(np.int32(1), (n_bits - 1) - it)
            below = count(lambda kch, r0: (kch == thr) & (r0 + row_io < cand))
            return jnp.where(below < need, cand, p)

        p = lax.fori_loop(0, n_bits, idx_step, jnp.zeros((1, tq), I32))
        plim_ref[...] = jnp.broadcast_to(jnp.where(tie_i > 0, p, seq), plim_ref.shape)

    for g in range(N_KV_HEADS):
        for r in range(grp):
            hh = g * grp + r
            qs_ref[g, r * tq:(r + 1) * tq, :] = q_ref[:, hh * HEAD_DIM:(hh + 1) * HEAD_DIM]
    m_ref[...] = jnp.full(m_ref.shape, MASK_NEG, F32)
    l_ref[...] = jnp.zeros_like(l_ref)
    acc_ref[...] = jnp.zeros_like(acc_ref)
    c_exp = HEAD_DIM ** -0.5 * np.log2(np.e)

    def att_chunk(c, carry):
        r0 = pl.multiple_of(c * kc, kc)
        kch = key_ref[pl.ds(r0, kc), :]
        row = r0 + row_io
        sel = ((kch > thr) | ((kch == thr) & (row <= plim_ref[0:1, :]))) & (row <= tpos)
        bias_t = jnp.where(sel, 0.0, MASK_NEG).astype(F32).T
        for g in range(N_KV_HEADS):
            kg = k_ref[pl.ds(r0, kc), g * HEAD_DIM:(g + 1) * HEAD_DIM]
            vg = v_ref[pl.ds(r0, kc), g * HEAD_DIM:(g + 1) * HEAD_DIM]
            s = lax.dot_general(qs_ref[g], kg, nt, preferred_element_type=F32)
            s = (s.reshape(grp, tq, kc) + bias_t[None]).reshape(grp * tq, kc)
            m_old = m_ref[g]
            m_new = jnp.maximum(m_old, jnp.max(s, axis=-1, keepdims=True))
            alpha = jnp.exp2((m_old - m_new) * c_exp)
            p = jnp.exp2((s - m_new) * c_exp)
            l_ref[g] = alpha * l_ref[g] + jnp.sum(p, axis=-1, keepdims=True)
            acc_ref[g] = alpha * acc_ref[g] + jnp.dot(p.astype(BF16), vg, preferred_element_type=F32)
            m_ref[g] = m_new
        return carry

    lax.fori_loop(0, nk, att_chunk, 0)
    for g in range(N_KV_HEADS):
        o = acc_ref[g] / l_ref[g]
        for r in range(grp):
            hh = g * grp + r
            o_ref[:, hh * HEAD_DIM:(hh + 1) * HEAD_DIM] = o[r * tq:(r + 1) * tq, :].astype(BF16)


def _attn(proj, small, bsz, seq):
    tq = min(ATTN_TQ, seq)
    kc = min(ATTN_KC, seq)
    assert seq % tq == 0 and seq % kc == 0 and kc % tq == 0
    nq = seq // tq
    n_keep = min(TOPK_MAX, seq // 4)
    grp = N_HEADS // N_KV_HEADS
    kern = functools.partial(_attn_kernel, tq=tq, kc=kc, seq=seq, n_keep=n_keep)
    return pl.pallas_call(
        kern,
        grid=(bsz, nq),
        in_specs=[
            pl.BlockSpec((tq, ATTN_WIDTH), lambda b, i: (b * nq + i, Q_OFF // ATTN_WIDTH)),
            pl.BlockSpec((tq, IDX_HEADS * IDX_DIM), lambda b, i: (b * nq + i, QI_OFF // (IDX_HEADS * IDX_DIM))),
            pl.BlockSpec((seq, KV_WIDTH), lambda b, i: (b, K_OFF // KV_WIDTH)),
            pl.BlockSpec((seq, KV_WIDTH), lambda b, i: (b, V_OFF // KV_WIDTH)),
            pl.BlockSpec((seq, LANES), lambda b, i: (b, 0)),
            pl.BlockSpec((tq, LANES), lambda b, i: (b * nq + i, 0)),
        ],
        out_specs=pl.BlockSpec((tq, ATTN_WIDTH), lambda b, i: (b * nq + i, 0)),
        out_shape=jax.ShapeDtypeStruct((bsz * seq, ATTN_WIDTH), BF16),
        scratch_shapes=[
            pltpu.VMEM((seq, tq), I32),
            pltpu.VMEM((8, tq), I32),
            pltpu.VMEM((N_KV_HEADS, grp * tq, HEAD_DIM), BF16),
            pltpu.VMEM((N_KV_HEADS, grp * tq, 1), F32),
            pltpu.VMEM((N_KV_HEADS, grp * tq, 1), F32),
            pltpu.VMEM((N_KV_HEADS, grp * tq, HEAD_DIM), F32),
        ],
        compiler_params=_cparams(("arbitrary", "arbitrary")),
        name="attn",
    )(proj, proj, proj, proj, small, small)


def _merge_kernel(
    u_ref, halo_ref, ya_ref, gl_ref, x_ref, mod_ref, wgrp_ref, pscale_ref, wup_p_ref, wup_a_ref, wout_ref,
    gpost_ref, gffn_ref, wr_ref, br_ref,
    x1_ref, hp_ref, route_ref, cnt_ref,
    carry_ref, *, seq,
):
    tm = MERGE_TM
    i = pl.program_id(0)
    pos0 = (i * tm) % seq

    @pl.when(i == 0)
    def _():
        carry_ref[...] = jnp.zeros_like(carry_ref)

    halo_rows = halo_ref.shape[0]
    halo = jnp.where(pos0 == 0, 0.0, halo_ref[...].astype(F32))
    ext = jnp.concatenate([halo, u_ref[...].astype(F32)], axis=0)
    pos = pos0 + lax.broadcasted_iota(I32, (tm, 1), 0)
    ys = []
    for g, win in enumerate(POOL_WINDOWS):
        sl = slice(g * POOL_GROUP_DIM, (g + 1) * POOL_GROUP_DIM)
        e = ext[:, sl]
        acc = e
        span = 1
        while span < win:
            acc = acc + pltpu.roll(acc, span, 0)
            span *= 2
        cnt = jnp.minimum(pos + 1, win).astype(F32)
        mixed = (acc[halo_rows:] / cnt - e[halo_rows:]).astype(BF16)
        ys.append(jnp.dot(mixed, wgrp_ref[g], preferred_element_type=F32))
    y_pool = (jnp.concatenate(ys, axis=1) * pscale_ref[...]).astype(BF16)

    gates = jax.nn.sigmoid(gl_ref[...].astype(F32))
    up_p = jnp.dot(y_pool, wup_p_ref[...], preferred_element_type=F32)
    up_a = jnp.dot(ya_ref[...], wup_a_ref[...], preferred_element_type=F32)
    merged = (gates[:, :D_MODEL] * up_p + gates[:, D_MODEL:] * up_a).astype(BF16)
    y = jnp.dot(merged, wout_ref[...], preferred_element_type=F32)
    m = mod_ref[0]
    x1 = x_ref[...] + m[2:3] * _rms(y, gpost_ref[...])
    x1_ref[...] = x1

    h2f = _rms(x1, gffn_ref[...]) * (1.0 + m[4:5]) + m[3:4]
    h2 = h2f.astype(BF16)
    hp_ref[...] = _pack_bf16_pairs(h2f)

    lane = lax.broadcasted_iota(I32, (tm, LANES), 1)
    logits = jnp.dot(h2, wr_ref[...], preferred_element_type=F32) + br_ref[...]
    work = jnp.where(lane < N_EXPERTS, logits, -jnp.inf)
    vals, hots = [], []
    for _ in range(TOP_K):
        mx = jnp.max(work, axis=-1, keepdims=True)
        first = jnp.min(jnp.where(work == mx, lane, LANES), axis=-1, keepdims=True)
        hot = lane == first
        vals.append(mx)
        hots.append(hot)
        work = jnp.where(hot, -jnp.inf, work)
    ex = [jnp.exp(v - vals[0]) for v in vals]
    den = ex[0] + ex[1] + ex[2] + ex[3]

    onehot = jnp.zeros((tm, LANES), F32)
    for hot in hots:
        onehot = onehot + hot.astype(F32)
    r_i = lax.broadcasted_iota(I32, (tm, tm), 0)
    c_i = lax.broadcasted_iota(I32, (tm, tm), 1)
    tril = (c_i < r_i).astype(BF16)
    before = jnp.dot(tril, onehot.astype(BF16), preferred_element_type=F32) + carry_ref[0:1, :]
    out = jnp.zeros((tm, LANES), F32)
    lane_f = lane.astype(F32)
    for k in range(TOP_K):
        out = jnp.where(lane == k, ex[k] / den, out)
        idx_k = jnp.sum(jnp.where(hots[k], lane_f, 0.0), axis=-1, keepdims=True)
        out = jnp.where(lane == TOP_K + k, idx_k, out)
        rank_k = jnp.sum(jnp.where(hots[k], before, 0.0), axis=-1, keepdims=True)
        out = jnp.where(lane == 2 * TOP_K + k, rank_k, out)
    route_ref[...] = out
    total = carry_ref[0:1, :] + jnp.sum(onehot, axis=0, keepdims=True)
    carry_ref[...] = jnp.broadcast_to(total, carry_ref.shape)
    cnt_ref[...] = jnp.broadcast_to(total, cnt_ref.shape)


def _merge(proj, y_attn, x2, mod3, w_pool_grp, pool_scale, w_up_pool, w_up_attn, w_out, g_post_mix, g_pre_ffn,
           w_router, b_router, seq):
    m_rows = x2.shape[0]
    tm = MERGE_TM
    halo = 16
    assert seq % tm == 0 and max(POOL_WINDOWS) <= halo
    per_seq = seq // tm
    wr = jnp.concatenate([w_router, jnp.zeros((D_MODEL, LANES - N_EXPERTS), F32)], axis=1).astype(BF16)
    br = jnp.concatenate([b_router, jnp.zeros((LANES - N_EXPERTS,), F32)]).reshape(1, LANES)
    const = lambda shape: pl.BlockSpec(shape, lambda i: (0,) * len(shape), pipeline_mode=pl.Buffered(1))
    kern = functools.partial(_merge_kernel, seq=seq)
    return pl.pallas_call(
        kern,
        grid=(m_rows // tm,),
        in_specs=[
            pl.BlockSpec((tm, POOL_WIDTH), lambda i: (i, U_OFF // POOL_WIDTH)),
            pl.BlockSpec((halo, POOL_WIDTH), lambda i: (jnp.maximum(i * (tm // halo) - 1, 0), U_OFF // POOL_WIDTH)),
            pl.BlockSpec((tm, ATTN_WIDTH), lambda i: (i, 0)),
            pl.BlockSpec((tm, N_BRANCHES * D_MODEL), lambda i: (i, 0)),
            pl.BlockSpec((tm, D_MODEL), lambda i: (i, 0)),
            pl.BlockSpec((1, N_MOD, D_MODEL), lambda i: (i // per_seq, 0, 0)),
            const((POOL_GROUPS, POOL_GROUP_DIM, POOL_GROUP_DIM)),
            const((1, POOL_WIDTH)),
            const((POOL_WIDTH, D_MODEL)),
            const((ATTN_WIDTH, D_MODEL)),
            const((D_MODEL, D_MODEL)),
            const((1, D_MODEL)),
            const((1, D_MODEL)),
            const((D_MODEL, LANES)),
            const((1, LANES)),
        ],
        out_specs=[
            pl.BlockSpec((tm, D_MODEL), lambda i: (i, 0)),
            pl.BlockSpec((tm, D_MODEL // 2), lambda i: (i, 0)),
            pl.BlockSpec((tm, LANES), lambda i: (i, 0)),
            pl.BlockSpec((8, LANES), lambda i: (0, 0)),
        ],
        out_shape=[
            jax.ShapeDtypeStruct((m_rows, D_MODEL), F32),
            jax.ShapeDtypeStruct((m_rows, D_MODEL // 2), U32),
            jax.ShapeDtypeStruct((m_rows, LANES), F32),
            jax.ShapeDtypeStruct((8, LANES), F32),
        ],
        scratch_shapes=[pltpu.VMEM((8, LANES), F32)],
        compiler_params=_cparams(("arbitrary",)),
        name="merge",
    )(
        proj, proj, y_attn, proj, x2, mod3,
        w_pool_grp.astype(BF16), pool_scale.reshape(1, POOL_WIDTH), w_up_pool.astype(BF16),
        w_up_attn.astype(BF16), w_out.astype(BF16), g_post_mix.reshape(1, D_MODEL), g_pre_ffn.reshape(1, D_MODEL),
        wr, br,
    )


def _dispatch_kernel(zf_ref, pos_ref, h_ref, xs_ref, zbuf, sem, zsem):
    tm = h_ref.shape[0]
    bm = zbuf.shape[0]

    @pl.when(pl.program_id(0) == 0)
    def _():
        zbuf[...] = jnp.zeros_like(zbuf)

        def zero_copy(b):
            return pltpu.make_async_copy(zbuf, xs_ref.at[pl.ds(b * bm, bm), :], zsem)

        def z_issue(b, carry):
            pl.when(zf_ref[b] != 0)(lambda: zero_copy(b).start())
            return carry

        def z_drain(b, carry):
            pl.when(zf_ref[b] != 0)(lambda: zero_copy(b).wait())
            return carry

        lax.fori_loop(0, zf_ref.shape[0], z_issue, 0)
        lax.fori_loop(0, zf_ref.shape[0], z_drain, 0)

    def row_copy(t, k):
        return pltpu.make_async_copy(
            h_ref.at[pl.ds(t, 1), :], xs_ref.at[pl.ds(pos_ref[t * TOP_K + k], 1), :], sem
        )

    def issue(t, carry):
        for k in range(TOP_K):
            row_copy(t, k).start(priority=k % 2)
        return carry

    lax.fori_loop(0, tm, issue, 0, unroll=ROW_DMA_UNROLL)
    for _ in range(TOP_K):
        pltpu.make_async_copy(h_ref, xs_ref.at[pl.ds(0, tm), :], sem).wait()


def _dispatch(zero_flag, pos_flat, hp, n_rows):
    m_rows, width = hp.shape
    tm = min(DISPATCH_TM, m_rows)
    assert m_rows % tm == 0
    grid_spec = pltpu.PrefetchScalarGridSpec(
        num_scalar_prefetch=1,
        grid=(m_rows // tm,),
        in_specs=[
            pl.BlockSpec((tm * TOP_K,), lambda i, zf: (i,), memory_space=pltpu.SMEM),
            pl.BlockSpec((tm, width), lambda i, zf: (i, 0)),
        ],
        out_specs=pl.BlockSpec(memory_space=pl.ANY),
        scratch_shapes=[pltpu.VMEM((MOE_BM, width), U32), pltpu.SemaphoreType.DMA(()), pltpu.SemaphoreType.DMA(())],
    )
    return pl.pallas_call(
        _dispatch_kernel,
        grid_spec=grid_spec,
        out_shape=jax.ShapeDtypeStruct((n_rows, width), U32),
        compiler_params=_cparams(("arbitrary",)),
        name="dispatch",
    )(zero_flag, pos_flat, hp)


def _expert_rows(e, j, bs_ref, nb_ref, nu_ref, x_hbm, o_hbm, xbuf, obuf, xsem, osem, compute, prepare):
    n_x, bm = xbuf.shape[0], xbuf.shape[1]
    tw = obuf.shape[2]
    n_blocks = x_hbm.shape[0] // bm
    b0 = bs_ref[e]
    nb = nb_ref[e]
    n_used = nu_ref[0]

    def x_copy(blk, slot):
        return pltpu.make_async_copy(x_hbm.at[pl.ds(blk * bm, bm), :], xbuf.at[slot], xsem.at[slot])

    def o_copy(blk, slot):
        return pltpu.make_async_copy(obuf.at[slot], o_hbm.at[pl.ds(blk * bm, bm), pl.ds(j * tw, tw)], osem.at[slot])

    @pl.when(nb > 0)
    def _():
        prepare()

        def body(rb, carry):
            g = b0 + rb
            n_o = obuf.shape[0]
            slot = g % n_o
            ahead = n_x - 1

            @pl.when(g == 0)
            def _():
                for a in range(ahead):
                    pl.when(a < n_used)(lambda a=a: x_copy(a, a).start())

            x_copy(g, g % n_x).wait()

            @pl.when(g + ahead < n_used)
            def _():
                x_copy(g + ahead, (g + ahead) % n_x).start()

            @pl.when(g >= n_o)
            def _():
                o_copy(g - n_o, slot).wait()

            obuf[slot] = compute(xbuf[g % n_x])
            o_copy(g, slot).start()

            @pl.when(g == n_used - 1)
            def _():
                for back in range(n_o - 1, -1, -1):
                    pl.when(g >= back)(lambda back=back: o_copy(g - back, (g - back) % n_o).wait())

            return carry

        lax.fori_loop(0, nb, body, 0)

    @pl.when(e == N_EXPERTS - 1)
    def _():
        obuf[0] = jnp.zeros(obuf.shape[1:], obuf.dtype)

        def z_issue(blk, carry):
            o_copy(blk, 0).start()
            return carry

        def z_drain(blk, carry):
            o_copy(blk, 0).wait()
            return carry

        lax.fori_loop(nu_ref[0], n_blocks, z_issue, 0)
        lax.fori_loop(nu_ref[0], n_blocks, z_drain, 0)


def _ffn1_kernel(bs_ref, nb_ref, nu_ref, x_hbm, wg_ref, wl_ref, bg_ref, bl_ref, o_hbm,
                 wg_s, wl_s, xbuf, obuf, xsem, osem):
    j, e = pl.program_id(0), pl.program_id(1)
    half = D_MODEL // 2

    def prepare():
        wg_s[...] = wg_ref[0].astype(BF16)
        wl_s[...] = wl_ref[0].astype(BF16)

    def compute(xp):
        lo, hi = (v.astype(BF16) for v in _unpack_bf16_pairs(xp))

        def proj(w_s, b_ref):
            return (
                jnp.dot(lo, w_s[:half, :], preferred_element_type=F32)
                + jnp.dot(hi, w_s[half:, :], preferred_element_type=F32)
                + b_ref[0]
            )

        glu = jnp.minimum(proj(wg_s, bg_ref), SWIGLU_LIMIT)
        lin = jnp.clip(proj(wl_s, bl_ref), -SWIGLU_LIMIT, SWIGLU_LIMIT)
        return (glu * jax.nn.sigmoid(SWIGLU_ALPHA * glu) * (lin + 1.0)).astype(BF16)

    _expert_rows(e, j, bs_ref, nb_ref, nu_ref, x_hbm, o_hbm, xbuf, obuf, xsem, osem, compute, prepare)


def _ffn1(blk_start, blk_count, n_used, xs, w1, b1):
    n_rows = xs.shape[0]
    bm, tn = MOE_BM, FFN1_TN
    nj = D_FF // tn
    b1r = b1.reshape(N_EXPERTS, 1, 2 * D_FF)
    grid_spec = pltpu.PrefetchScalarGridSpec(
        num_scalar_prefetch=3,
        grid=(nj, N_EXPERTS),
        in_specs=[
            pl.BlockSpec(memory_space=pl.ANY),
            pl.BlockSpec((1, D_MODEL, tn), lambda j, e, *_: (e, 0, j)),
            pl.BlockSpec((1, D_MODEL, tn), lambda j, e, *_: (e, 0, nj + j)),
            pl.BlockSpec((1, 1, tn), lambda j, e, *_: (e, 0, j)),
            pl.BlockSpec((1, 1, tn), lambda j, e, *_: (e, 0, nj + j)),
        ],
        out_specs=pl.BlockSpec(memory_space=pl.ANY),
        scratch_shapes=[
            pltpu.VMEM((D_MODEL, tn), BF16),
            pltpu.VMEM((D_MODEL, tn), BF16),
            pltpu.VMEM((FFN_X_SLOTS, bm, D_MODEL // 2), U32),
            pltpu.VMEM((FFN_O_SLOTS, bm, tn), BF16),
            pltpu.SemaphoreType.DMA((FFN_X_SLOTS,)),
            pltpu.SemaphoreType.DMA((FFN_O_SLOTS,)),
        ],
    )
    return pl.pallas_call(
        _ffn1_kernel,
        grid_spec=grid_spec,
        out_shape=jax.ShapeDtypeStruct((n_rows, D_FF), BF16),
        compiler_params=_cparams(("arbitrary", "arbitrary")),
        name="ffn1",
    )(blk_start, blk_count, n_used, xs, w1, w1, b1r, b1r)


def _ffn2_kernel(bs_ref, nb_ref, nu_ref, a_hbm, w_ref, b_ref, o_hbm, w_s, xbuf, obuf, xsem, osem):
    j, e = pl.program_id(0), pl.program_id(1)

    def prepare():
        w_s[...] = w_ref[0].astype(BF16)

    def compute(a):
        return _pack_bf16_pairs(jnp.dot(a, w_s[...], preferred_element_type=F32) + b_ref[0])

    _expert_rows(e, j, bs_ref, nb_ref, nu_ref, a_hbm, o_hbm, xbuf, obuf, xsem, osem, compute, prepare)


def _ffn2(blk_start, blk_count, n_used, act, w2, b2):
    n_rows = act.shape[0]
    bm, tn = MOE_BM, FFN2_TN
    nj = D_MODEL // tn
    b2r = b2.reshape(N_EXPERTS, 1, D_MODEL)
    grid_spec = pltpu.PrefetchScalarGridSpec(
        num_scalar_prefetch=3,
        grid=(nj, N_EXPERTS),
        in_specs=[
            pl.BlockSpec(memory_space=pl.ANY),
            pl.BlockSpec((1, D_FF, tn), lambda j, e, *_: (e, 0, j)),
            pl.BlockSpec((1, 1, tn), lambda j, e, *_: (e, 0, j)),
        ],
        out_specs=pl.BlockSpec(memory_space=pl.ANY),
        scratch_shapes=[
            pltpu.VMEM((D_FF, tn), BF16),
            pltpu.VMEM((FFN_X_SLOTS, bm, D_FF), BF16),
            pltpu.VMEM((FFN_O_SLOTS, bm, tn // 2), U32),
            pltpu.SemaphoreType.DMA((FFN_X_SLOTS,)),
            pltpu.SemaphoreType.DMA((FFN_O_SLOTS,)),
        ],
    )
    return pl.pallas_call(
        _ffn2_kernel,
        grid_spec=grid_spec,
        out_shape=jax.ShapeDtypeStruct((n_rows, D_MODEL // 2), U32),
        compiler_params=_cparams(("arbitrary", "arbitrary")),
        name="ffn2",
    )(blk_start, blk_count, n_used, act, w2, b2r)


def _combine_kernel(pos_ref, ys_ref, route_ref, x1_ref, mod_ref, g_ref, o_ref, buf, sem):
    tm = x1_ref.shape[0]

    def row_copy(t, k):
        return pltpu.make_async_copy(
            ys_ref.at[pl.ds(pos_ref[t * TOP_K + k], 1), :], buf.at[k, pl.ds(t, 1), :], sem
        )

    def issue(t, carry):
        for k in range(TOP_K):
            row_copy(t, k).start(priority=k % 2)
        return carry

    lax.fori_loop(0, tm, issue, 0, unroll=ROW_DMA_UNROLL)
    for k in range(TOP_K):
        pltpu.make_async_copy(ys_ref.at[pl.ds(0, tm), :], buf.at[k], sem).wait()
    route = route_ref[...]
    y_lo, y_hi = (route[:, 0:1] * v for v in _unpack_bf16_pairs(buf[0]))
    for k in range(1, TOP_K):
        lo, hi = _unpack_bf16_pairs(buf[k])
        y_lo = y_lo + route[:, k:k + 1] * lo
        y_hi = y_hi + route[:, k:k + 1] * hi
    hw = FFN2_TN // 2
    parts = []
    for t in range(D_MODEL // FFN2_TN):
        parts += [y_lo[:, t * hw:(t + 1) * hw], y_hi[:, t * hw:(t + 1) * hw]]
    y = jnp.concatenate(parts, axis=1)
    m = mod_ref[0]
    o_ref[...] = x1_ref[...] + m[5:6] * _rms(y, g_ref[...])


def _combine(pos_flat, ys, route, x1, mod3, g_post_ffn, seq):
    m_rows = x1.shape[0]
    tm = min(COMBINE_TM, seq)
    assert seq % tm == 0
    per_seq = seq // tm
    return pl.pallas_call(
        _combine_kernel,
        grid=(m_rows // tm,),
        in_specs=[
            pl.BlockSpec((tm * TOP_K,), lambda i: (i,), memory_space=pltpu.SMEM),
            pl.BlockSpec(memory_space=pl.ANY),
            pl.BlockSpec((tm, LANES), lambda i: (i, 0)),
            pl.BlockSpec((tm, D_MODEL), lambda i: (i, 0)),
            pl.BlockSpec((1, N_MOD, D_MODEL), lambda i: (i // per_seq, 0, 0)),
            pl.BlockSpec((1, D_MODEL), lambda i: (0, 0)),
        ],
        out_specs=pl.BlockSpec((tm, D_MODEL), lambda i: (i, 0)),
        out_shape=jax.ShapeDtypeStruct((m_rows, D_MODEL), F32),
        scratch_shapes=[pltpu.VMEM((TOP_K, tm, D_MODEL // 2), U32), pltpu.SemaphoreType.DMA(())],
        compiler_params=_cparams(("arbitrary",)),
        name="combine",
    )(pos_flat, ys, route, x1, mod3, g_post_ffn.reshape(1, D_MODEL))


def _layer(x2, mod3, bsz, seq, g_pre_mix, g_post_mix, w_in, w_pool_grp, pool_scale, w_up_pool, w_up_attn, w_out,
           g_pre_ffn, g_post_ffn, w_router, b_router, w1, b1, w2, b2):
    m_rows = x2.shape[0]
    proj, small = _in_proj(x2, mod3, g_pre_mix, w_in, seq)
    y_attn = _attn(proj, small, bsz, seq)
    x1, hp, route, cnt = _merge(proj, y_attn, x2, mod3, w_pool_grp, pool_scale, w_up_pool, w_up_attn, w_out,
                                g_post_mix, g_pre_ffn, w_router, b_router, seq)

    bm = MOE_BM
    n_slots = m_rows * TOP_K
    n_blocks = -(-n_slots // bm) + N_EXPERTS
    idx = route[:, TOP_K:2 * TOP_K].astype(I32)
    rank = route[:, 2 * TOP_K:3 * TOP_K].astype(I32)
    counts = cnt[0, :N_EXPERTS].astype(I32)
    padded = (counts + bm - 1) // bm * bm
    pad_end = jnp.cumsum(padded)
    pad_start = pad_end - padded
    pos_flat = (pad_start[idx] + rank).reshape(-1)
    n_used = (pad_end[-1] // bm).astype(I32).reshape(1)
    blk_ids = jnp.arange(n_blocks, dtype=I32)
    blk_start = (pad_start // bm).astype(I32)
    blk_count = (padded // bm).astype(I32)
    last_of_expert = jnp.any((blk_ids[:, None] + 1) * bm == pad_end[None, :], axis=1)
    zero_flag = (last_of_expert | (blk_ids >= n_used[0])).astype(I32)

    xs = _dispatch(zero_flag, pos_flat, hp, n_blocks * bm)
    act = _ffn1(blk_start, blk_count, n_used, xs, w1, b1)
    ys = _ffn2(blk_start, blk_count, n_used, act, w2, b2)
    return _combine(pos_flat, ys, route, x1, mod3, g_post_ffn, seq)


def kernel(x, c, w_ada, b_ada, g_pre_mix, g_post_mix, w_in, w_pool_grp, pool_scale, w_up_pool, w_up_attn, w_out,
           g_pre_ffn, g_post_ffn, w_router, b_router, w1, b1, w2, b2):
    bsz, seq, d = x.shape
    assert d == D_MODEL
    depth = w_ada.shape[0]
    x2 = x.reshape(bsz * seq, d)
    for layer in range(depth):
        mod3 = _ada(c, w_ada[layer], b_ada[layer]).reshape(bsz, N_MOD, D_MODEL)
        x2 = _layer(x2, mod3, bsz, seq, g_pre_mix[layer], g_post_mix[layer], w_in[layer], w_pool_grp[layer],
                    pool_scale[layer], w_up_pool[layer], w_up_attn[layer], w_out[layer], g_pre_ffn[layer],
                    g_post_ffn[layer], w_router[layer], b_router[layer], w1[layer], b1[layer], w2[layer], b2[layer])
    return x2.reshape(bsz, seq, d)
```
